```python
import math
import jax, jax.numpy as jnp
from jax import lax
import numpy as np

D_MODEL = 2048
BATCH = 4
SEQ = 4096
DEPTH = 2

N_META = 16
BLOCK = 128
WINDOW = 128
META_PAD = BLOCK - N_META
HEAD_DIM = 64
N_Q_HEADS = 16
N_KV_HEADS = 4
Q_PER_KV = N_Q_HEADS // N_KV_HEADS
ATTN_WIDTH = N_Q_HEADS * HEAD_DIM
KV_WIDTH = N_KV_HEADS * HEAD_DIM
SSM_WIDTH = D_MODEL // 2
SSM_GROUP = 16
SSM_GROUPS = SSM_WIDTH // SSM_GROUP
SSM_STATE = 64
DT_MIN = 1e-3
DT_MAX = 1e-1
D_FF = 4 * D_MODEL
N_BUCKETS = 32
MAX_DISTANCE = 128
DN_ALPHA = (2 * DEPTH) ** 0.25
DN_BETA = (8 * DEPTH) ** -0.25
LN_EPS = 1e-5
NEG_INF = -1e30
IN_WIDTH = ATTN_WIDTH + 2 * KV_WIDTH + SSM_WIDTH + 2 * D_MODEL
SPLITS = [ATTN_WIDTH, ATTN_WIDTH + KV_WIDTH, ATTN_WIDTH + 2 * KV_WIDTH,
          ATTN_WIDTH + 2 * KV_WIDTH + SSM_WIDTH]

kernel_name = 'hybrid_s5_swa_gated_deepnorm'


def layer_norm(x, g, b):
    xf = x.astype(jnp.float32)
    mu = jnp.mean(xf, axis=-1, keepdims=True)
    xc = xf - mu
    var = jnp.mean(xc * xc, axis=-1, keepdims=True)
    y = xc * lax.rsqrt(var + LN_EPS) * g.astype(jnp.float32) + b.astype(jnp.float32)
    return y.astype(x.dtype)


def t5_bucket(dist):
    n = jnp.maximum(dist, 0)
    max_exact = N_BUCKETS // 2
    nf = jnp.maximum(n, 1).astype(jnp.float32)
    large = max_exact + (jnp.log(nf / max_exact) / math.log(MAX_DISTANCE / max_exact)
                         * (N_BUCKETS - max_exact)).astype(jnp.int32)
    large = jnp.minimum(large, N_BUCKETS - 1)
    return jnp.where(n < max_exact, n, large)


def band_layout(n_blocks):
    blk = jnp.arange(n_blocks, dtype=jnp.int32)[:, None]
    j = jnp.arange(BLOCK, dtype=jnp.int32)[None, :]
    q_pos = blk * BLOCK + j
    k_meta = jnp.broadcast_to(j, (n_blocks, BLOCK))
    k_prev = (blk - 1) * BLOCK + j
    k_pos = jnp.concatenate([k_meta, k_prev, q_pos], axis=1)
    dist = q_pos[:, :, None] - k_pos[:, None, :]
    kp = k_pos[:, None, :]
    is_meta_seg = (jnp.arange(3 * BLOCK) < BLOCK)[None, None, :]
    meta_ok = kp >= META_PAD
    real_ok = (kp >= BLOCK) & (dist < WINDOW)
    valid = (dist >= 0) & jnp.where(is_meta_seg, meta_ok, real_ok)
    return dist, valid


def sliding_window_attention(q, k, v, attn_bias, sinks):
    b, lp, _ = q.shape
    nb = lp // BLOCK
    q = q.reshape(b, nb, BLOCK, N_KV_HEADS, Q_PER_KV, HEAD_DIM)

    def bands(t):
        t = t.reshape(b, nb, BLOCK, N_KV_HEADS, HEAD_DIM)
        meta = jnp.broadcast_to(t[:, :1], t.shape)
        prev = jnp.pad(t[:, :-1], ((0, 0), (1, 0), (0, 0), (0, 0), (0, 0)))
        return jnp.concatenate([meta, prev, t], axis=2)

    kb, vb = bands(k), bands(v)
    s = jnp.einsum('bnqkgd,bnskd->bnkgqs', q, kb).astype(jnp.float32) * (HEAD_DIM ** -0.5) + attn_bias
    sink = sinks.astype(jnp.float32).reshape(1, 1, N_KV_HEADS, Q_PER_KV, 1, 1)
    m = jnp.maximum(jnp.max(s, axis=-1, keepdims=True), sink)
    p = jnp.exp(s - m)
    p = p / (jnp.sum(p, axis=-1, keepdims=True) + jnp.exp(sink - m))
    o = jnp.einsum('bnkgqs,bnskd->bnqkgd', p.astype(vb.dtype), vb)
    return o.reshape(b, lp, ATTN_WIDTH)


def s5_mixer(u, lam_re, lam_im, log_step, b_re, b_im, c_re, c_im, d_skip, w_glu):
    bsz, L, _ = u.shape
    uf = u.astype(jnp.float32).reshape(bsz, L, SSM_GROUPS, SSM_GROUP)
    lr, li = lam_re.astype(jnp.float32), lam_im.astype(jnp.float32)
    dt = jnp.exp(log_step.astype(jnp.float32))[:, None]
    decay = jnp.exp(lr * dt)
    ar, ai = decay * jnp.cos(li * dt), decay * jnp.sin(li * dt)
    den = lr * lr + li * li
    nr, ni = ar - 1.0, ai
    zr = (nr * lr + ni * li) / den
    zi = (ni * lr - nr * li) / den
    br_, bi_ = b_re.astype(jnp.float32), b_im.astype(jnp.float32)
    bbar_re = zr[..., None] * br_ - zi[..., None] * bi_
    bbar_im = zr[..., None] * bi_ + zi[..., None] * br_
    xr = jnp.einsum('blgp,gnp->blgn', uf, bbar_re)
    xi = jnp.einsum('blgp,gnp->blgn', uf, bbar_im)
    a_r = jnp.broadcast_to(ar, (1, L) + ar.shape)
    a_i = jnp.broadcast_to(ai, (1, L) + ai.shape)

    def combine(e1, e2):
        a1r, a1i, b1r, b1i = e1
        a2r, a2i, b2r, b2i = e2
        return (a1r * a2r - a1i * a2i,
                a1r * a2i + a1i * a2r,
                a2r * b1r - a2i * b1i + b2r,
                a2r * b1i + a2i * b1r + b2i)

    _, _, hr, hi = lax.associative_scan(combine, (a_r, a_i, xr, xi), axis=1)
    y = (jnp.einsum('blgn,gpn->blgp', hr, c_re.astype(jnp.float32))
         - jnp.einsum('blgn,gpn->blgp', hi, c_im.astype(jnp.float32))
         + d_skip.astype(jnp.float32) * uf)
    y = jax.nn.gelu(y.reshape(bsz, L, SSM_WIDTH))
    y = y * jax.nn.sigmoid(y @ w_glu.astype(jnp.float32))
    return y.astype(u.dtype)


def hybrid_layer(h, attn_bias, in_proj, gate_b, sinks, lam_re, lam_im, log_step, b_re, b_im,
                 c_re, c_im, d_skip, w_glu, w_attn_up, w_ssm_up, w_out, ln_mix_g, ln_mix_b,
                 w_mlp_up, w_mlp_down, ln_mlp_g, ln_mlp_b):
    z = h @ in_proj
    q, k, v, u, g = jnp.split(z, SPLITS, axis=-1)
    gates = jax.nn.sigmoid((g + gate_b).astype(jnp.float32)).astype(h.dtype)
    g_attn, g_ssm = jnp.split(gates, 2, axis=-1)
    pad = ((0, 0), (META_PAD, 0), (0, 0))
    y_attn = sliding_window_attention(jnp.pad(q, pad), jnp.pad(k, pad), jnp.pad(v, pad),
                                      attn_bias, sinks)[:, META_PAD:]
    y_ssm = s5_mixer(u, lam_re, lam_im, log_step, b_re, b_im, c_re, c_im, d_skip, w_glu)
    mixed = (g_attn * (y_attn @ w_attn_up) + g_ssm * (y_ssm @ w_ssm_up)) @ w_out
    h = layer_norm(DN_ALPHA * h + mixed, ln_mix_g, ln_mix_b)
    f = jnp.square(jax.nn.relu(h @ w_mlp_up)) @ w_mlp_down
    return layer_norm(DN_ALPHA * h + f, ln_mlp_g, ln_mlp_b)


def setup_inputs(seed: int = 0) -> dict:
    key = jax.random.key(seed)
    ks = jax.random.split(key, 32)
    f32 = jnp.float32

    def nrm(k, shape, scale):
        return jax.random.normal(k, shape, f32) * scale

    G, N, P = SSM_GROUPS, SSM_STATE, SSM_GROUP
    n_idx = jnp.arange(N, dtype=f32)
    return {
        'x': nrm(ks[0], (BATCH, SEQ, D_MODEL), 1.0),
        'meta_tokens': nrm(ks[1], (N_META, D_MODEL), 1.0),
        'ln_emb_g': 1.0 + nrm(ks[2], (D_MODEL,), 0.02),
        'ln_emb_b': nrm(ks[3], (D_MODEL,), 0.02),
        'rel_bias': nrm(ks[4], (N_BUCKETS, N_Q_HEADS), 0.5),
        'in_proj': nrm(ks[5], (DEPTH, D_MODEL, IN_WIDTH), D_MODEL ** -0.5),
        'gate_b': nrm(ks[6], (DEPTH, 2 * D_MODEL), 0.02),
        'attn_sinks': nrm(ks[7], (DEPTH, N_Q_HEADS), 0.5),
        'ssm_lambda_re': -0.5 + nrm(ks[8], (DEPTH, G, N), 0.01),
        'ssm_lambda_im': math.pi * n_idx + nrm(ks[9], (DEPTH, G, N), 0.01),
        'ssm_log_step': jax.random.uniform(ks[10], (DEPTH, G), f32, math.log(DT_MIN), math.log(DT_MAX)),
        'ssm_b_re': nrm(ks[11], (DEPTH, G, N, P), (2 * P) ** -0.5),
        'ssm_b_im': nrm(ks[12], (DEPTH, G, N, P), (2 * P) ** -0.5),
        'ssm_c_re': nrm(ks[13], (DEPTH, G, P, N), 0.5 ** 0.5),
        'ssm_c_im': nrm(ks[14], (DEPTH, G, P, N), 0.5 ** 0.5),
        'ssm_d': nrm(ks[15], (DEPTH, G, P), 0.5),
        'ssm_w_glu': nrm(ks[16], (DEPTH, SSM_WIDTH, SSM_WIDTH), SSM_WIDTH ** -0.5),
        'w_attn_up': nrm(ks[17], (DEPTH, ATTN_WIDTH, D_MODEL), ATTN_WIDTH ** -0.5),
        'w_ssm_up': nrm(ks[18], (DEPTH, SSM_WIDTH, D_MODEL), SSM_WIDTH ** -0.5),
        'w_out': nrm(ks[19], (DEPTH, D_MODEL, D_MODEL), DN_BETA * D_MODEL ** -0.5),
        'ln_mix_g': 1.0 + nrm(ks[20], (DEPTH, D_MODEL), 0.02),
        'ln_mix_b': nrm(ks[21], (DEPTH, D_MODEL), 0.02),
        'w_mlp_up': nrm(ks[22], (DEPTH, D_MODEL, D_FF), D_MODEL ** -0.5),
        'w_mlp_down': nrm(ks[23], (DEPTH, D_FF, D_MODEL), DN_BETA * D_FF ** -0.5),
        'ln_mlp_g': 1.0 + nrm(ks[24], (DEPTH, D_MODEL), 0.02),
        'ln_mlp_b': nrm(ks[25], (DEPTH, D_MODEL), 0.02),
    }


def reference(x, meta_tokens, ln_emb_g, ln_emb_b, rel_bias, in_proj, gate_b, attn_sinks,
              ssm_lambda_re, ssm_lambda_im, ssm_log_step, ssm_b_re, ssm_b_im, ssm_c_re, ssm_c_im,
              ssm_d, ssm_w_glu, w_attn_up, w_ssm_up, w_out, ln_mix_g, ln_mix_b,
              w_mlp_up, w_mlp_down, ln_mlp_g, ln_mlp_b):
    bsz, seq, _ = x.shape
    meta = jnp.broadcast_to(meta_tokens[None].astype(x.dtype), (bsz, N_META, D_MODEL))
    h = layer_norm(jnp.concatenate([meta, x], axis=1), ln_emb_g, ln_emb_b)

    n_blocks = (seq + BLOCK) // BLOCK
    dist, valid = band_layout(n_blocks)
    bias = rel_bias.astype(jnp.float32)[t5_bucket(dist)]
    bias = bias.transpose(0, 3, 1, 2).reshape(n_blocks, N_KV_HEADS, Q_PER_KV, BLOCK, 3 * BLOCK)
    attn_bias = jnp.where(valid[:, None, None], bias, NEG_INF)

    for l in range(DEPTH):
        h = hybrid_layer(h, attn_bias, in_proj[l], gate_b[l], attn_sinks[l],
                         ssm_lambda_re[l], ssm_lambda_im[l], ssm_log_step[l],
                         ssm_b_re[l], ssm_b_im[l], ssm_c_re[l], ssm_c_im[l], ssm_d[l],
                         ssm_w_glu[l], w_attn_up[l], w_ssm_up[l], w_out[l],
                         ln_mix_g[l], ln_mix_b[l], w_mlp_up[l], w_mlp_down[l],
                         ln_mlp_g[l], ln_mlp_b[l])
    return h[:, N_META:]
```

```python
import functools
import math

import jax
import jax.numpy as jnp
from jax import lax
from jax.experimental import pallas as pl
from jax.experimental.pallas import tpu as pltpu

F32 = jnp.float32
BF16 = jnp.bfloat16

N_META = 16
BLOCK = 128
META_PAD = BLOCK - N_META
HEAD_DIM = 64
N_Q_HEADS = 16
N_KV_HEADS = 4
Q_PER_KV = N_Q_HEADS // N_KV_HEADS
ATTN_WIDTH = N_Q_HEADS * HEAD_DIM
KV_WIDTH = N_KV_HEADS * HEAD_DIM
SSM_GROUP = 16
SSM_STATE = 64
N_BUCKETS = 32
MAX_DISTANCE = 128
LN_EPS = 1e-5
NEG_INF = -1e30

LANES = 128
SUBLANES = 8
VMEM_LIMIT_BYTES = 56 * 1024 * 1024

SSM_SEQS = SUBLANES
STATES_PER_CHUNK = LANES
GROUPS_PER_CHUNK = STATES_PER_CHUNK // SSM_STATE
CHUNKS_PER_SLAB = 4
SLAB_STATE_LANES = CHUNKS_PER_SLAB * 2 * STATES_PER_CHUNK


def _cparams(*sem):
    return pltpu.CompilerParams(dimension_semantics=sem, vmem_limit_bytes=VMEM_LIMIT_BYTES)


def _resident(shape):
    nd = len(shape)
    return pl.BlockSpec(shape, lambda *_: (0,) * nd, pipeline_mode=pl.Buffered(1))


def _layer_norm(x, g, b):
    mu = jnp.mean(x, axis=-1, keepdims=True)
    xc = x - mu
    var = jnp.mean(xc * xc, axis=-1, keepdims=True)
    return xc * lax.rsqrt(var + LN_EPS) * g + b


def _sigmoid(z):
    return 1.0 / (1.0 + jnp.exp(-z))


def _pad_row_mask(first_row, n_rows, lp, n_batch):
    r = first_row + lax.broadcasted_iota(jnp.int32, (n_rows, 1), 0)
    is_pad = jnp.zeros((n_rows, 1), jnp.bool_)
    for b in range(n_batch):
        is_pad = is_pad | ((r >= b * lp) & (r < b * lp + META_PAD))
    return jnp.logical_not(is_pad)


def _embed_ln_kernel(meta_ref, x_ref, g_ref, b_ref, hf_ref, hb_ref):
    is_meta = pl.program_id(1) == 0
    xin = jnp.where(is_meta, meta_ref[...], x_ref[...])
    y = _layer_norm(xin, g_ref[...], b_ref[...])
    rows = lax.broadcasted_iota(jnp.int32, (BLOCK, 1), 0)
    keep = jnp.logical_or(jnp.logical_not(is_meta), rows >= META_PAD)
    y = jnp.where(keep, y, 0.0)
    hf_ref[...] = y
    hb_ref[...] = y.astype(BF16)


def _embed_ln(x, meta_pad, g, b):
    bsz, seq, d = x.shape
    nb = seq // BLOCK + 1
    t = bsz * nb * BLOCK
    return pl.pallas_call(
        _embed_ln_kernel,
        grid=(bsz, nb),
        in_specs=[
            pl.BlockSpec((BLOCK, d), lambda bi, i: (0, 0)),
            pl.BlockSpec((None, BLOCK, d), lambda bi, i: (bi, jnp.maximum(i - 1, 0), 0)),
            pl.BlockSpec((1, d), lambda bi, i: (0, 0)),
            pl.BlockSpec((1, d), lambda bi, i: (0, 0)),
        ],
        out_specs=[
            pl.BlockSpec((BLOCK, d), lambda bi, i: (bi * nb + i, 0)),
            pl.BlockSpec((BLOCK, d), lambda bi, i: (bi * nb + i, 0)),
        ],
        out_shape=[jax.ShapeDtypeStruct((t, d), F32), jax.ShapeDtypeStruct((t, d), BF16)],
        compiler_params=_cparams("parallel", "parallel"),
        name="embed_ln",
    )(meta_pad, x, g, b)


def _matmul_kernel(x_ref, w_ref, o_ref):
    o_ref[...] = jnp.dot(x_ref[...], w_ref[...], preferred_element_type=F32).astype(o_ref.dtype)


def _gate_kernel(x_ref, w_ref, b_ref, o_ref):
    z = jnp.dot(x_ref[...], w_ref[...], preferred_element_type=F32) + b_ref[...]
    o_ref[...] = _sigmoid(z).astype(o_ref.dtype)


def _row_tile(t):
    for tm in (1024, 512, 384, 256, 128):
        if t % tm == 0:
            return tm
    raise ValueError(f"token count {t} is not a multiple of {BLOCK}")


def _col_tile(n):
    for tn in (512, 256, 128):
        if n % tn == 0:
            return tn
    raise ValueError(f"width {n} is not a multiple of {LANES}")


def _project(x, w, out_dtype, bias=None):
    t, k = x.shape
    n = w.shape[1]
    tm, tn = _row_tile(t), _col_tile(n)
    in_specs = [pl.BlockSpec((tm, k), lambda i, j: (i, 0)), pl.BlockSpec((k, tn), lambda i, j: (0, j))]
    args = [x, w]
    body = _matmul_kernel
    if bias is not None:
        in_specs.append(pl.BlockSpec((1, tn), lambda i, j: (0, j)))
        args.append(bias)
        body = _gate_kernel
    return pl.pallas_call(
        body,
        grid=(t // tm, n // tn),
        in_specs=in_specs,
        out_specs=pl.BlockSpec((tm, tn), lambda i, j: (i, j)),
        out_shape=jax.ShapeDtypeStruct((t, n), out_dtype),
        compiler_params=_cparams("parallel", "parallel"),
        name="in_proj",
    )(*args)


def _attn_kernel(sink_ref, q_ref, kvm_ref, kvp_ref, kvc_ref, bias_ref, o_ref):
    kv = jnp.concatenate([kvm_ref[...], kvp_ref[...], kvc_ref[...]], axis=0)
    for kh in range(N_KV_HEADS):
        k = kv[:, kh * HEAD_DIM:(kh + 1) * HEAD_DIM]
        v = kv[:, KV_WIDTH + kh * HEAD_DIM:KV_WIDTH + (kh + 1) * HEAD_DIM]
        for g in range(Q_PER_KV):
            h = kh * Q_PER_KV + g
            q = q_ref[:, h * HEAD_DIM:(h + 1) * HEAD_DIM]
            s = lax.dot_general(q, k, (((1,), (1,)), ((), ())), preferred_element_type=F32)
            s = s + bias_ref[h]
            sink = sink_ref[h]
            m = jnp.maximum(jnp.max(s, axis=-1, keepdims=True), sink)
            p = jnp.exp(s - m)
            denom = jnp.sum(p, axis=-1, keepdims=True) + jnp.exp(sink - m)
            o = jnp.dot(p.astype(BF16), v, preferred_element_type=F32) / denom
            o_ref[:, h * HEAD_DIM:(h + 1) * HEAD_DIM] = o.astype(o_ref.dtype)


def _attention(qkv, bias_tables, sinks, bsz, nb):
    t = qkv.shape[0]
    kv_col = ATTN_WIDTH // (2 * KV_WIDTH)
    kv_spec = lambda row_fn: pl.BlockSpec((BLOCK, 2 * KV_WIDTH), lambda b, n: (row_fn(b, n), kv_col))
    return pl.pallas_call(
        _attn_kernel,
        grid=(bsz, nb),
        in_specs=[
            pl.BlockSpec(memory_space=pltpu.SMEM),
            pl.BlockSpec((BLOCK, ATTN_WIDTH), lambda b, n: (b * nb + n, 0)),
            kv_spec(lambda b, n: b * nb),
            kv_spec(lambda b, n: b * nb + jnp.maximum(n - 1, 0)),
            kv_spec(lambda b, n: b * nb + n),
            pl.BlockSpec((None, N_Q_HEADS, BLOCK, 3 * BLOCK), lambda b, n: (jnp.minimum(n, 2), 0, 0, 0)),
        ],
        out_specs=pl.BlockSpec((BLOCK, ATTN_WIDTH), lambda b, n: (b * nb + n, 0)),
        out_shape=jax.ShapeDtypeStruct((t, ATTN_WIDTH), BF16),
        compiler_params=_cparams("parallel", "parallel"),
        name="swa",
    )(sinks, qkv, qkv, qkv, qkv, bias_tables)


def _t5_bucket(dist):
    n = jnp.maximum(dist, 0)
    max_exact = N_BUCKETS // 2
    nf = jnp.maximum(n, 1).astype(F32)
    large = max_exact + (jnp.log(nf / max_exact) / math.log(MAX_DISTANCE / max_exact)
                         * (N_BUCKETS - max_exact)).astype(jnp.int32)
    large = jnp.minimum(large, N_BUCKETS - 1)
    return jnp.where(n < max_exact, n, large)


def _bias_tables(rel_bias):
    blk = jnp.arange(3, dtype=jnp.int32)[:, None]
    j = jnp.arange(BLOCK, dtype=jnp.int32)[None, :]
    q_pos = blk * BLOCK + j
    k_pos = jnp.concatenate([jnp.broadcast_to(j, (3, BLOCK)), (blk - 1) * BLOCK + j, q_pos], axis=1)
    dist = q_pos[:, :, None] - k_pos[:, None, :]
    kp = k_pos[:, None, :]
    is_meta_seg = (jnp.arange(3 * BLOCK) < BLOCK)[None, None, :]
    valid = (dist >= 0) & jnp.where(is_meta_seg, kp >= META_PAD, (kp >= BLOCK) & (dist < BLOCK))
    bias = rel_bias.astype(F32)[_t5_bucket(dist)].transpose(0, 3, 1, 2)
    return jnp.where(valid[:, None], bias, NEG_INF)


def _s5_discretize_kernel(lr_ref, li_ref, ls_ref, bre_ref, bim_ref, ar_ref, ai_ref, bbr_ref, bbi_ref):
    lr, li = lr_ref[...], li_ref[...]
    dt = jnp.exp(ls_ref[...])
    decay = jnp.exp(lr * dt)
    ar, ai = decay * jnp.cos(li * dt), decay * jnp.sin(li * dt)
    den = lr * lr + li * li
    nr, ni = ar - 1.0, ai
    zr = (nr * lr + ni * li) / den
    zi = (ni * lr - nr * li) / den
    bre, bim = bre_ref[...], bim_ref[...]
    ar_ref[...] = ar
    ai_ref[...] = ai
    bbr_ref[...] = zr * bre - zi * bim
    bbi_ref[...] = zr * bim + zi * bre


def _s5_discretize(lam_re, lam_im, log_step, b_re, b_im):
    g, n, p = b_re.shape
    rep = lambda a: jnp.repeat(a.astype(F32), p, axis=-1)
    ls = jnp.broadcast_to(log_step.astype(F32)[:, None], (g, n * p))
    shp = jax.ShapeDtypeStruct((g, n * p), F32)
    ar, ai, bbr, bbi = pl.pallas_call(
        _s5_discretize_kernel, out_shape=[shp] * 4, name="s5_discretize",
    )(rep(lam_re), rep(lam_im), ls, b_re.astype(F32).reshape(g, n * p), b_im.astype(F32).reshape(g, n * p))
    return ar[:, ::p], ai[:, ::p], bbr.reshape(g, n, p), bbi.reshape(g, n, p)


def _s5_scan_weights(ar, ai, bbr, bbi, c_re, c_im, d_skip):
    g, n, p = bbr.shape
    halves, slabs, q_per, gl_per = 2, g // (2 * CHUNKS_PER_SLAB * GROUPS_PER_CHUNK), CHUNKS_PER_SLAB, GROUPS_PER_CHUNK
    gi_per = q_per * gl_per
    eye_q = jnp.eye(q_per, dtype=F32)
    eye_gl = jnp.eye(gl_per, dtype=F32)

    def split(a):
        return a.reshape((halves, slabs, q_per, gl_per) + a.shape[1:])

    b_ri = jnp.stack([split(bbr), split(bbi)], axis=0)
    wb = jnp.einsum("rhjqgnp,qa,gb->jhqgparbn", b_ri, eye_q, eye_gl)
    wb = wb.reshape(slabs, halves * gi_per * p, q_per * 2 * gl_per * n)
    c_ri = jnp.stack([split(c_re.astype(F32)), -split(c_im.astype(F32))], axis=0)
    wc = jnp.einsum("rhjqgpn,qa,gb->jarbnhqgp", c_ri, eye_q, eye_gl)
    wc = wc.reshape(slabs, q_per * 2 * gl_per * n, halves * gi_per * p)

    def lanes(a):
        a = a.reshape(halves, -1)
        return jnp.repeat(a, SSM_SEQS // halves, axis=0)

    d = jnp.repeat(d_skip.astype(F32).reshape(halves, -1), SSM_SEQS // halves, axis=0)
    return wb.astype(BF16), wc.astype(BF16), lanes(ar), lanes(ai), d


def _s5_scan_kernel(u_ref, wb_ref, wc_ref, ar_ref, ai_ref, d_ref, y_ref, s_ref, h_ref, *, steps, unroll):
    rows = steps * SSM_SEQS
    n_slabs = wb_ref.shape[0]
    ch = LANES

    @pl.when(pl.program_id(0) == 0)
    def _():
        h_ref[...] = jnp.zeros_like(h_ref)

    half0 = (lax.broadcasted_iota(jnp.int32, (rows, 1), 0) & (SSM_SEQS - 1)) < (SSM_SEQS // 2)

    for j in range(n_slabs):
        uj = u_ref[:, j * ch:(j + 1) * ch]
        lhs = jnp.concatenate([jnp.where(half0, uj, 0.0), jnp.where(half0, 0.0, uj)], axis=1).astype(BF16)
        s_ref[:, j * SLAB_STATE_LANES:(j + 1) * SLAB_STATE_LANES] = jnp.dot(
            lhs, wb_ref[j], preferred_element_type=F32)

    n_chunks = n_slabs * CHUNKS_PER_SLAB
    for c0 in range(0, n_chunks, CHUNKS_PER_SLAB):
        chunks = range(c0, c0 + CHUNKS_PER_SLAB)
        a_r = [ar_ref[:, c * LANES:(c + 1) * LANES] for c in chunks]
        a_i = [ai_ref[:, c * LANES:(c + 1) * LANES] for c in chunks]
        init = []
        for c in chunks:
            init += [h_ref[:, 2 * c * LANES:(2 * c + 1) * LANES], h_ref[:, (2 * c + 1) * LANES:(2 * c + 2) * LANES]]

        def body(t, carry, chunks=chunks, a_r=a_r, a_i=a_i):
            r0 = pl.multiple_of(t * SSM_SEQS, SSM_SEQS)
            new = []
            for k, c in enumerate(chunks):
                hr, hi = carry[2 * k], carry[2 * k + 1]
                re_sl = (pl.ds(r0, SSM_SEQS), pl.ds(2 * c * LANES, LANES))
                im_sl = (pl.ds(r0, SSM_SEQS), pl.ds((2 * c + 1) * LANES, LANES))
                nr = a_r[k] * hr - a_i[k] * hi + s_ref[re_sl]
                ni = a_r[k] * hi + a_i[k] * hr + s_ref[im_sl]
                s_ref[re_sl] = nr
                s_ref[im_sl] = ni
                new += [nr, ni]
            return tuple(new)

        final = lax.fori_loop(0, steps, body, tuple(init), unroll=unroll)
        for k, c in enumerate(chunks):
            h_ref[:, 2 * c * LANES:(2 * c + 1) * LANES] = final[2 * k]
            h_ref[:, (2 * c + 1) * LANES:(2 * c + 2) * LANES] = final[2 * k + 1]

    for j in range(n_slabs):
        hs = s_ref[:, j * SLAB_STATE_LANES:(j + 1) * SLAB_STATE_LANES].astype(BF16)
        both = jnp.dot(hs, wc_ref[j], preferred_element_type=F32)
        yj = jnp.where(half0, both[:, :ch], both[:, ch:])
        uj = u_ref[:, j * ch:(j + 1) * ch]
        dj = jnp.tile(d_ref[:, j * ch:(j + 1) * ch], (steps, 1))
        y_ref[:, j * ch:(j + 1) * ch] = jax.nn.gelu(yj + dj * uj).astype(y_ref.dtype)


def _s5_scan(u_seq, wb, wc, a_r, a_i, d_rows, steps):
    rows_total, half_w = u_seq.shape
    rows = steps * SSM_SEQS
    state_lanes = wb.shape[0] * SLAB_STATE_LANES
    return pl.pallas_call(
        functools.partial(_s5_scan_kernel, steps=steps, unroll=4),
        grid=(rows_total // rows,),
        in_specs=[
            pl.BlockSpec((rows, half_w), lambda i: (i, 0)),
            _resident(wb.shape), _resident(wc.shape),
            _resident(a_r.shape), _resident(a_i.shape), _resident(d_rows.shape),
        ],
        out_specs=pl.BlockSpec((rows, half_w), lambda i: (i, 0)),
        out_shape=jax.ShapeDtypeStruct((rows_total, half_w), BF16),
        scratch_shapes=[pltpu.VMEM((rows, state_lanes), F32), pltpu.VMEM((SSM_SEQS, state_lanes), F32)],
        compiler_params=_cparams("arbitrary"),
        name="s5_scan",
    )(u_seq, wb, wc, a_r, a_i, d_rows)


def _mix_kernel(ya_ref, ys_ref, gate_ref, h_ref, wglu_ref, wau_ref, wsu_ref, wout_ref, g_ref, b_ref,
                hf_ref, hb_ref, *, alpha, lp, n_batch):
    d = h_ref.shape[1]
    tm = h_ref.shape[0]
    y = ys_ref[...]
    glu = jnp.dot(y, wglu_ref[...], preferred_element_type=F32)
    y_ssm = (y.astype(F32) * _sigmoid(glu)).astype(BF16)
    up_a = jnp.dot(ya_ref[...], wau_ref[...], preferred_element_type=F32)
    up_s = jnp.dot(y_ssm, wsu_ref[...], preferred_element_type=F32)
    merged = gate_ref[:, :d].astype(F32) * up_a + gate_ref[:, d:].astype(F32) * up_s
    mixed = jnp.dot(merged.astype(BF16), wout_ref[...], preferred_element_type=F32)
    out = _layer_norm(alpha * h_ref[...] + mixed, g_ref[...], b_ref[...])
    out = jnp.where(_pad_row_mask(pl.program_id(0) * tm, tm, lp, n_batch), out, 0.0)
    hf_ref[...] = out
    hb_ref[...] = out.astype(BF16)


def _mix(y_attn, y_ssm, gates, h, wglu, wau, wsu, wout, g, b, alpha, lp, n_batch):
    t, d = h.shape
    tm = 256 if t % 256 == 0 else BLOCK
    row = lambda w: pl.BlockSpec((tm, w), lambda i: (i, 0))
    return pl.pallas_call(
        functools.partial(_mix_kernel, alpha=alpha, lp=lp, n_batch=n_batch),
        grid=(t // tm,),
        in_specs=[row(y_attn.shape[1]), row(y_ssm.shape[1]), row(gates.shape[1]), row(d),
                  _resident(wglu.shape), _resident(wau.shape), _resident(wsu.shape), _resident(wout.shape),
                  _resident(g.shape), _resident(b.shape)],
        out_specs=[row(d), row(d)],
        out_shape=[jax.ShapeDtypeStruct((t, d), F32), jax.ShapeDtypeStruct((t, d), BF16)],
        compiler_params=_cparams("parallel"),
        name="mix_ln",
    )(y_attn, y_ssm, gates, h, wglu, wau, wsu, wout, g, b)


def _mlp_kernel(xb_ref, xf_ref, wup_ref, wdn_ref, g_ref, b_ref, hf_ref, hb_ref, acc_ref, *, alpha, lp, n_batch):
    i, j = pl.program_id(0), pl.program_id(1)
    tm = xb_ref.shape[0]
    a = jnp.maximum(jnp.dot(xb_ref[...], wup_ref[...], preferred_element_type=F32), 0.0)
    part = jnp.dot((a * a).astype(BF16), wdn_ref[...], preferred_element_type=F32)

    @pl.when(j == 0)
    def _():
        acc_ref[...] = part

    @pl.when(j > 0)
    def _():
        acc_ref[...] += part

    @pl.when(j == pl.num_programs(1) - 1)
    def _():
        out = _layer_norm(alpha * xf_ref[...] + acc_ref[...], g_ref[...], b_ref[...])
        out = jnp.where(_pad_row_mask(i * tm, tm, lp, n_batch), out, 0.0)
        hf_ref[...] = out
        hb_ref[...] = out.astype(BF16)


def _mlp(hb, hf, wup, wdn, g, b, alpha, lp, n_batch):
    t, d = hf.shape
    dff = wup.shape[1]
    tm = 512 if t % 512 == 0 else BLOCK
    tf = 1024 if dff % 1024 == 0 else LANES
    row = lambda: pl.BlockSpec((tm, d), lambda i, j: (i, 0))
    return pl.pallas_call(
        functools.partial(_mlp_kernel, alpha=alpha, lp=lp, n_batch=n_batch),
        grid=(t // tm, dff // tf),
        in_specs=[row(), row(),
                  pl.BlockSpec((d, tf), lambda i, j: (0, j)),
                  pl.BlockSpec((tf, d), lambda i, j: (j, 0)),
                  pl.BlockSpec((1, d), lambda i, j: (0, 0)),
                  pl.BlockSpec((1, d), lambda i, j: (0, 0))],
        out_specs=[row(), row()],
        out_shape=[jax.ShapeDtypeStruct((t, d), F32), jax.ShapeDtypeStruct((t, d), BF16)],
        scratch_shapes=[pltpu.VMEM((tm, d), F32)],
        compiler_params=_cparams("parallel", "arbitrary"),
        name="mlp_ln",
    )(hb, hf, wup, wdn, g, b)


def kernel(x, meta_tokens, ln_emb_g, ln_emb_b, rel_bias, in_proj, gate_b, attn_sinks, ssm_lambda_re,
           ssm_lambda_im, ssm_log_step, ssm_b_re, ssm_b_im, ssm_c_re, ssm_c_im, ssm_d, ssm_w_glu, w_attn_up,
           w_ssm_up, w_out, ln_mix_g, ln_mix_b, w_mlp_up, w_mlp_down, ln_mlp_g, ln_mlp_b):
    bsz, seq, d = x.shape
    depth = in_proj.shape[0]
    assert seq % BLOCK == 0 and 2 * bsz == SSM_SEQS and meta_tokens.shape[0] == N_META
    nb = seq // BLOCK + 1
    lp = nb * BLOCK
    alpha = (2 * depth) ** 0.25
    ssm_w = ssm_w_glu.shape[1]
    half_w = ssm_w // 2
    row2 = lambda a: a.astype(F32).reshape(1, -1)

    meta_pad = jnp.concatenate([jnp.zeros((META_PAD, d), x.dtype), meta_tokens.astype(x.dtype)], axis=0)
    hf, hb = _embed_ln(x, meta_pad, row2(ln_emb_g), row2(ln_emb_b))
    bias_tables = _bias_tables(rel_bias)
    steps = 64 if lp % 64 == 0 else BLOCK // 2

    for l in range(depth):
        w_in = in_proj[l]
        qkv_w = ATTN_WIDTH + 2 * KV_WIDTH
        w_qkv = jnp.concatenate([w_in[:, :ATTN_WIDTH] * (HEAD_DIM ** -0.5), w_in[:, ATTN_WIDTH:qkv_w]], axis=1)
        qkv = _project(hb, w_qkv.astype(BF16), BF16)
        u = _project(hb, w_in[:, qkv_w:qkv_w + ssm_w].astype(BF16), F32)
        gates = _project(hb, w_in[:, qkv_w + ssm_w:].astype(BF16), BF16, bias=row2(gate_b[l]))

        y_attn = _attention(qkv, bias_tables, attn_sinks[l].astype(F32), bsz, nb)

        ar, ai, bbr, bbi = _s5_discretize(ssm_lambda_re[l], ssm_lambda_im[l], ssm_log_step[l],
                                          ssm_b_re[l], ssm_b_im[l])
        wb, wc, a_r, a_i, d_rows = _s5_scan_weights(ar, ai, bbr, bbi, ssm_c_re[l], ssm_c_im[l], ssm_d[l])
        u_seq = u.reshape(bsz, lp, 2, half_w).transpose(1, 2, 0, 3).reshape(lp * SSM_SEQS, half_w)
        y_seq = _s5_scan(u_seq, wb, wc, a_r, a_i, d_rows, steps)
        y_gelu = y_seq.reshape(lp, 2, bsz, half_w).transpose(2, 0, 1, 3).reshape(bsz * lp, ssm_w)

        hf, hb = _mix(y_attn, y_gelu, gates, hf, ssm_w_glu[l].astype(BF16), w_attn_up[l].astype(BF16),
                      w_ssm_up[l].astype(BF16), w_out[l].astype(BF16), row2(ln_mix_g[l]), row2(ln_mix_b[l]),
                      alpha, lp, bsz)
        hf, hb = _mlp(hb, hf, w_mlp_up[l].astype(BF16), w_mlp_down[l].astype(BF16),
                      row2(ln_mlp_g[l]), row2(ln_mlp_b[l]), alpha, lp, bsz)

    return hf.reshape(bsz, lp, d)[:, BLOCK:]
```

```python
import functools
import math

import jax
import jax.numpy as jnp
from jax import lax
from jax.experimental import pallas as pl
from jax.experimental.pallas import tpu as pltpu

F32 = jnp.float32
BF16 = jnp.bfloat16

N_META = 16
BLOCK = 128
META_PAD = BLOCK - N_META
HEAD_DIM = 64
N_Q_HEADS = 16
N_KV_HEADS = 4
Q_PER_KV = N_Q_HEADS // N_KV_HEADS
ATTN_WIDTH = N_Q_HEADS * HEAD_DIM
KV_WIDTH = N_KV_HEADS * HEAD_DIM
QKV_WIDTH = ATTN_WIDTH + 2 * KV_WIDTH
SSM_GROUP = 16
SSM_STATE = 64
N_BUCKETS = 32
MAX_DISTANCE = 128
LN_EPS = 1e-5
NEG_INF = -1e30

LANES = 128
SUBLANES = 8
VMEM_LIMIT_BYTES = 56 * 1024 * 1024

SSM_SEQS = SUBLANES
STATES_PER_CHUNK = LANES
GROUPS_PER_CHUNK = STATES_PER_CHUNK // SSM_STATE
CHUNKS_PER_SLAB = 4
SLAB_STATE_LANES = CHUNKS_PER_SLAB * 2 * STATES_PER_CHUNK


def _cparams(*sem):
    return pltpu.CompilerParams(dimension_semantics=sem, vmem_limit_bytes=VMEM_LIMIT_BYTES)


def _resident(shape):
    nd = len(shape)
    return pl.BlockSpec(shape, lambda *_: (0,) * nd, pipeline_mode=pl.Buffered(1))


def _layer_norm(x, g, b):
    mu = jnp.mean(x, axis=-1, keepdims=True)
    xc = x - mu
    var = jnp.mean(xc * xc, axis=-1, keepdims=True)
    return xc * lax.rsqrt(var + LN_EPS) * g + b


def _sigmoid(z):
    return 1.0 / (1.0 + jnp.exp(-z))


def _real_row_mask(first_row, n_rows, n_batch):
    r = first_row + lax.broadcasted_iota(jnp.int32, (n_rows, 1), 0)
    is_pad = (r < n_batch * BLOCK) & ((r & (BLOCK - 1)) < META_PAD)
    return jnp.logical_not(is_pad)


def _embed_ln_kernel(meta_ref, x_ref, g_ref, b_ref, hf_ref, hb_ref):
    is_meta = pl.program_id(0) == 0
    xin = jnp.where(is_meta, meta_ref[...], x_ref[...])
    y = _layer_norm(xin, g_ref[...], b_ref[...])
    rows = lax.broadcasted_iota(jnp.int32, (BLOCK, 1), 0)
    keep = jnp.logical_or(jnp.logical_not(is_meta), rows >= META_PAD)
    y = jnp.where(keep, y, 0.0)
    hf_ref[...] = y
    hb_ref[...] = y.astype(BF16)


def _embed_ln(x, meta_pad, g, b):
    bsz, seq, d = x.shape
    nb = seq // BLOCK + 1
    t = bsz * nb * BLOCK
    return pl.pallas_call(
        _embed_ln_kernel,
        grid=(nb, bsz),
        in_specs=[
            pl.BlockSpec((BLOCK, d), lambda n, bi: (0, 0)),
            pl.BlockSpec((None, BLOCK, d), lambda n, bi: (bi, jnp.maximum(n - 1, 0), 0)),
            pl.BlockSpec((1, d), lambda n, bi: (0, 0)),
            pl.BlockSpec((1, d), lambda n, bi: (0, 0)),
        ],
        out_specs=[
            pl.BlockSpec((BLOCK, d), lambda n, bi: (n * bsz + bi, 0)),
            pl.BlockSpec((BLOCK, d), lambda n, bi: (n * bsz + bi, 0)),
        ],
        out_shape=[jax.ShapeDtypeStruct((t, d), F32), jax.ShapeDtypeStruct((t, d), BF16)],
        compiler_params=_cparams("parallel", "parallel"),
        name="embed_ln",
    )(meta_pad, x, g, b)


GATE_COL_CHUNK = 1024


def _in_proj_kernel(x_ref, wqkv_ref, wg_ref, gb_ref, qkv_ref, gate_ref):
    x = x_ref[...]
    qkv_ref[...] = jnp.dot(x, wqkv_ref[...], preferred_element_type=F32).astype(qkv_ref.dtype)
    for c in range(0, wg_ref.shape[1], GATE_COL_CHUNK):
        z = jnp.dot(x, wg_ref[:, c:c + GATE_COL_CHUNK], preferred_element_type=F32) + gb_ref[:, c:c + GATE_COL_CHUNK]
        gate_ref[:, c:c + GATE_COL_CHUNK] = _sigmoid(z).astype(gate_ref.dtype)


def _in_proj(hb, w_qkv, w_gate, gate_bias, tm):
    t, d = hb.shape
    row = lambda w: pl.BlockSpec((tm, w), lambda i: (i, 0))
    return pl.pallas_call(
        _in_proj_kernel,
        grid=(t // tm,),
        in_specs=[row(d), _resident(w_qkv.shape), _resident(w_gate.shape), _resident(gate_bias.shape)],
        out_specs=[row(w_qkv.shape[1]), row(w_gate.shape[1])],
        out_shape=[jax.ShapeDtypeStruct((t, w_qkv.shape[1]), BF16), jax.ShapeDtypeStruct((t, w_gate.shape[1]), BF16)],
        compiler_params=_cparams("parallel"),
        name="in_proj",
    )(hb, w_qkv, w_gate, gate_bias)


def _attn_kernel(sink_ref, q_ref, kvm_ref, kvp_ref, kvc_ref, bias_ref, o_ref):
    kv = jnp.concatenate([kvm_ref[...], kvp_ref[...], kvc_ref[...]], axis=0)
    rows = Q_PER_KV * BLOCK
    low = lax.broadcasted_iota(jnp.int32, (1, LANES), 1) < HEAD_DIM
    head_of_row = lax.broadcasted_iota(jnp.int32, (rows, 1), 0) // BLOCK
    zero = jnp.zeros((), q_ref.dtype)
    for kh in range(N_KV_HEADS):
        k = kv[:, kh * HEAD_DIM:(kh + 1) * HEAD_DIM]
        v = kv[:, KV_WIDTH + kh * HEAD_DIM:KV_WIDTH + (kh + 1) * HEAD_DIM]
        k2 = jnp.concatenate([k, k], axis=1)
        v2 = jnp.concatenate([v, v], axis=1)
        cols = [q_ref[:, (kh * Q_PER_KV + 2 * c) * HEAD_DIM:(kh * Q_PER_KV + 2 * c + 2) * HEAD_DIM]
                for c in range(Q_PER_KV // 2)]
        qs = jnp.concatenate([jnp.where(low, cols[g // 2], zero) if g % 2 == 0 else jnp.where(low, zero, cols[g // 2])
                              for g in range(Q_PER_KV)], axis=0)
        s = lax.dot_general(qs, k2, (((1,), (1,)), ((), ())), preferred_element_type=F32) + bias_ref[kh]
        sink = jnp.zeros((rows, 1), F32)
        for g in range(Q_PER_KV):
            sink = jnp.where(head_of_row == g, sink_ref[kh * Q_PER_KV + g], sink)
        m = jnp.maximum(jnp.max(s, axis=-1, keepdims=True), sink)
        p = jnp.exp(s - m)
        denom = jnp.sum(p, axis=-1, keepdims=True) + jnp.exp(sink - m)
        o = jnp.dot(p.astype(BF16), v2, preferred_element_type=F32) * (1.0 / denom)
        for c in range(Q_PER_KV // 2):
            pair = jnp.where(low, o[2 * c * BLOCK:(2 * c + 1) * BLOCK], o[(2 * c + 1) * BLOCK:(2 * c + 2) * BLOCK])
            o_ref[:, (kh * Q_PER_KV + 2 * c) * HEAD_DIM:(kh * Q_PER_KV + 2 * c + 2) * HEAD_DIM] = pair.astype(o_ref.dtype)


def _attention(qkv, bias_tables, sinks, bsz, nb):
    t = qkv.shape[0]
    kv_col = ATTN_WIDTH // (2 * KV_WIDTH)
    kv_spec = lambda row_fn: pl.BlockSpec((BLOCK, 2 * KV_WIDTH), lambda n, b: (row_fn(n, b), kv_col))
    return pl.pallas_call(
        _attn_kernel,
        grid=(nb, bsz),
        in_specs=[
            pl.BlockSpec(memory_space=pltpu.SMEM),
            pl.BlockSpec((BLOCK, ATTN_WIDTH), lambda n, b: (n * bsz + b, 0)),
            kv_spec(lambda n, b: b),
            kv_spec(lambda n, b: jnp.maximum(n - 1, 0) * bsz + b),
            kv_spec(lambda n, b: n * bsz + b),
            pl.BlockSpec((None, N_KV_HEADS, Q_PER_KV * BLOCK, 3 * BLOCK),
                         lambda n, b: (jnp.minimum(n, 2), 0, 0, 0)),
        ],
        out_specs=pl.BlockSpec((BLOCK, ATTN_WIDTH), lambda n, b: (n * bsz + b, 0)),
        out_shape=jax.ShapeDtypeStruct((t, ATTN_WIDTH), BF16),
        compiler_params=_cparams("parallel", "parallel"),
        name="swa",
    )(sinks, qkv, qkv, qkv, qkv, bias_tables)


def _t5_bucket(dist):
    n = jnp.maximum(dist, 0)
    max_exact = N_BUCKETS // 2
    nf = jnp.maximum(n, 1).astype(F32)
    large = max_exact + (jnp.log(nf / max_exact) / math.log(MAX_DISTANCE / max_exact)
                         * (N_BUCKETS - max_exact)).astype(jnp.int32)
    large = jnp.minimum(large, N_BUCKETS - 1)
    return jnp.where(n < max_exact, n, large)


def _bias_tables(rel_bias):
    blk = jnp.arange(3, dtype=jnp.int32)[:, None]
    j = jnp.arange(BLOCK, dtype=jnp.int32)[None, :]
    q_pos = blk * BLOCK + j
    k_pos = jnp.concatenate([jnp.broadcast_to(j, (3, BLOCK)), (blk - 1) * BLOCK + j, q_pos], axis=1)
    dist = q_pos[:, :, None] - k_pos[:, None, :]
    kp = k_pos[:, None, :]
    is_meta_seg = (jnp.arange(3 * BLOCK) < BLOCK)[None, None, :]
    valid = (dist >= 0) & jnp.where(is_meta_seg, kp >= META_PAD, (kp >= BLOCK) & (dist < BLOCK))
    onehot = jax.nn.one_hot(_t5_bucket(dist), N_BUCKETS, dtype=F32)
    bias = jnp.einsum("tqkb,bh->thqk", onehot, rel_bias.astype(F32), precision=lax.Precision.HIGHEST)
    bias = jnp.where(valid[:, None], bias, NEG_INF)
    return bias.reshape(3, N_KV_HEADS, Q_PER_KV * BLOCK, 3 * BLOCK)


def _s5_discretize_kernel(lr_ref, li_ref, ls_ref, bre_ref, bim_ref, ar_ref, ai_ref, bbr_ref, bbi_ref):
    lr, li = lr_ref[...], li_ref[...]
    dt = jnp.exp(ls_ref[...])
    decay = jnp.exp(lr * dt)
    ar, ai = decay * jnp.cos(li * dt), decay * jnp.sin(li * dt)
    den = lr * lr + li * li
    nr, ni = ar - 1.0, ai
    zr = (nr * lr + ni * li) / den
    zi = (ni * lr - nr * li) / den
    bre, bim = bre_ref[...], bim_ref[...]
    ar_ref[...] = ar
    ai_ref[...] = ai
    bbr_ref[...] = zr * bre - zi * bim
    bbi_ref[...] = zr * bim + zi * bre


def _s5_discretize(lam_re, lam_im, log_step, b_re, b_im):
    g, n, p = b_re.shape
    rep = lambda a: jnp.repeat(a.astype(F32), p, axis=-1)
    ls = jnp.broadcast_to(log_step.astype(F32)[:, None], (g, n * p))
    shp = jax.ShapeDtypeStruct((g, n * p), F32)
    ar, ai, bbr, bbi = pl.pallas_call(
        _s5_discretize_kernel, out_shape=[shp] * 4, name="s5_discretize",
    )(rep(lam_re), rep(lam_im), ls, b_re.astype(F32).reshape(g, n * p), b_im.astype(F32).reshape(g, n * p))
    return ar[:, ::p], ai[:, ::p], bbr.reshape(g, n, p), bbi.reshape(g, n, p)


def _s5_scan_weights(ar, ai, bbr, bbi, c_re, c_im, d_skip):
    g, n, p = bbr.shape
    halves, slabs, q_per, gl_per = 2, g // (2 * CHUNKS_PER_SLAB * GROUPS_PER_CHUNK), CHUNKS_PER_SLAB, GROUPS_PER_CHUNK
    gi_per = q_per * gl_per
    eye_q = jnp.eye(q_per, dtype=F32)
    eye_gl = jnp.eye(gl_per, dtype=F32)

    def split(a):
        return a.reshape((halves, slabs, q_per, gl_per) + a.shape[1:])

    b_ri = jnp.stack([split(bbr), split(bbi)], axis=0)
    wb = jnp.einsum("rhjqgnp,qa,gb->jhqgparbn", b_ri, eye_q, eye_gl)
    wb = wb.reshape(slabs, halves * gi_per * p, q_per * 2 * gl_per * n)
    c_ri = jnp.stack([split(c_re.astype(F32)), -split(c_im.astype(F32))], axis=0)
    wc = jnp.einsum("rhjqgpn,qa,gb->jarbnhqgp", c_ri, eye_q, eye_gl)
    wc = wc.reshape(slabs, q_per * 2 * gl_per * n, halves * gi_per * p)

    def lanes(a):
        return jnp.repeat(a.reshape(halves, -1), SSM_SEQS // halves, axis=0)

    d = jnp.repeat(d_skip.astype(F32).reshape(halves, -1), SSM_SEQS // halves, axis=0)
    return wb.astype(BF16), wc.astype(BF16), lanes(ar), lanes(ai), d


def _s5_scan_kernel(x_ref, wu_ref, wb_ref, wc_ref, ar_ref, ai_ref, d_ref, y_ref, u_scr, s_ref, y_scr, h_ref,
                    *, n_batch, unroll):
    steps = BLOCK
    rows = steps * SSM_SEQS
    n_slabs = wb_ref.shape[0]
    half_w = n_slabs * LANES

    @pl.when(pl.program_id(0) == 0)
    def _():
        h_ref[...] = jnp.zeros_like(h_ref)

    def seq_rows(b, half):
        return pl.ds(half * n_batch + b, steps, stride=SSM_SEQS)

    def channels(half, j):
        return slice(half * half_w + j * LANES, half * half_w + (j + 1) * LANES)

    u_all = jnp.dot(x_ref[...], wu_ref[...], preferred_element_type=F32)
    for b in range(n_batch):
        for half in range(2):
            for j in range(n_slabs):
                u_scr[j, seq_rows(b, half), :] = u_all[b * steps:(b + 1) * steps, channels(half, j)]

    half0 = (lax.broadcasted_iota(jnp.int32, (rows, 1), 0) & (SSM_SEQS - 1)) < (SSM_SEQS // 2)

    for j in range(n_slabs):
        uj = u_scr[j]
        lhs = jnp.concatenate([jnp.where(half0, uj, 0.0), jnp.where(half0, 0.0, uj)], axis=1).astype(BF16)
        s_ref[:, j * SLAB_STATE_LANES:(j + 1) * SLAB_STATE_LANES] = jnp.dot(
            lhs, wb_ref[j], preferred_element_type=F32)

    n_chunks = n_slabs * CHUNKS_PER_SLAB
    for c0 in range(0, n_chunks, CHUNKS_PER_SLAB):
        chunks = range(c0, c0 + CHUNKS_PER_SLAB)
        a_r = [ar_ref[:, c * LANES:(c + 1) * LANES] for c in chunks]
        a_i = [ai_ref[:, c * LANES:(c + 1) * LANES] for c in chunks]
        init = []
        for c in chunks:
            init += [h_ref[:, 2 * c * LANES:(2 * c + 1) * LANES], h_ref[:, (2 * c + 1) * LANES:(2 * c + 2) * LANES]]

        def body(t, carry, chunks=chunks, a_r=a_r, a_i=a_i):
            r0 = pl.multiple_of(t * SSM_SEQS, SSM_SEQS)
            new = []
            for k, c in enumerate(chunks):
                hr, hi = carry[2 * k], carry[2 * k + 1]
                re_sl = (pl.ds(r0, SSM_SEQS), pl.ds(2 * c * LANES, LANES))
                im_sl = (pl.ds(r0, SSM_SEQS), pl.ds((2 * c + 1) * LANES, LANES))
                nr = a_r[k] * hr - a_i[k] * hi + s_ref[re_sl]
                ni = a_r[k] * hi + a_i[k] * hr + s_ref[im_sl]
                s_ref[re_sl] = nr
                s_ref[im_sl] = ni
                new += [nr, ni]
            return tuple(new)

        final = lax.fori_loop(0, steps, body, tuple(init), unroll=unroll)
        for k, c in enumerate(chunks):
            h_ref[:, 2 * c * LANES:(2 * c + 1) * LANES] = final[2 * k]
            h_ref[:, (2 * c + 1) * LANES:(2 * c + 2) * LANES] = final[2 * k + 1]

    for j in range(n_slabs):
        hs = s_ref[:, j * SLAB_STATE_LANES:(j + 1) * SLAB_STATE_LANES].astype(BF16)
        both = jnp.dot(hs, wc_ref[j], preferred_element_type=F32)
        yj = jnp.where(half0, both[:, :LANES], both[:, LANES:])
        dj = jnp.tile(d_ref[:, j * LANES:(j + 1) * LANES], (steps, 1))
        y_scr[j] = jax.nn.gelu(yj + dj * u_scr[j])

    for b in range(n_batch):
        for half in range(2):
            for j in range(n_slabs):
                y_ref[b * steps:(b + 1) * steps, channels(half, j)] = y_scr[j, seq_rows(b, half), :].astype(y_ref.dtype)


def _s5_scan(hb, w_u, wb, wc, a_r, a_i, d_rows, n_batch):
    t, d = hb.shape
    tile = n_batch * BLOCK
    rows = BLOCK * SSM_SEQS
    n_slabs = wb.shape[0]
    state_lanes = n_slabs * SLAB_STATE_LANES
    ssm_w = w_u.shape[1]
    return pl.pallas_call(
        functools.partial(_s5_scan_kernel, n_batch=n_batch, unroll=4),
        grid=(t // tile,),
        in_specs=[
            pl.BlockSpec((tile, d), lambda i: (i, 0)),
            _resident(w_u.shape), _resident(wb.shape), _resident(wc.shape),
            _resident(a_r.shape), _resident(a_i.shape), _resident(d_rows.shape),
        ],
        out_specs=pl.BlockSpec((tile, ssm_w), lambda i: (i, 0)),
        out_shape=jax.ShapeDtypeStruct((t, ssm_w), BF16),
        scratch_shapes=[pltpu.VMEM((n_slabs, rows, LANES), F32), pltpu.VMEM((rows, state_lanes), F32),
                        pltpu.VMEM((n_slabs, rows, LANES), F32), pltpu.VMEM((SSM_SEQS, state_lanes), F32)],
        compiler_params=_cparams("arbitrary"),
        name="s5_scan",
    )(hb, w_u, wb, wc, a_r, a_i, d_rows)


def _mix_kernel(ya_ref, ys_ref, gate_ref, h_ref, wglu_ref, wau_ref, wsu_ref, wout_ref, g_ref, b_ref,
                hf_ref, hb_ref, *, alpha, n_batch):
    tm, d = h_ref.shape
    y = ys_ref[...]
    glu = jnp.dot(y, wglu_ref[...], preferred_element_type=F32)
    y_ssm = (y.astype(F32) * _sigmoid(glu)).astype(BF16)
    up_a = jnp.dot(ya_ref[...], wau_ref[...], preferred_element_type=F32)
    up_s = jnp.dot(y_ssm, wsu_ref[...], preferred_element_type=F32)
    merged = gate_ref[:, :d].astype(F32) * up_a + gate_ref[:, d:].astype(F32) * up_s
    mixed = jnp.dot(merged.astype(BF16), wout_ref[...], preferred_element_type=F32)
    out = _layer_norm(alpha * h_ref[...] + mixed, g_ref[...], b_ref[...])
    out = jnp.where(_real_row_mask(pl.program_id(0) * tm, tm, n_batch), out, 0.0)
    hf_ref[...] = out
    hb_ref[...] = out.astype(BF16)


def _mix(y_attn, y_ssm, gates, h, wglu, wau, wsu, wout, g, b, alpha, n_batch):
    t, d = h.shape
    tm = 2 * BLOCK
    row = lambda w: pl.BlockSpec((tm, w), lambda i: (i, 0))
    return pl.pallas_call(
        functools.partial(_mix_kernel, alpha=alpha, n_batch=n_batch),
        grid=(t // tm,),
        in_specs=[row(y_attn.shape[1]), row(y_ssm.shape[1]), row(gates.shape[1]), row(d),
                  _resident(wglu.shape), _resident(wau.shape), _resident(wsu.shape), _resident(wout.shape),
                  _resident(g.shape), _resident(b.shape)],
        out_specs=[row(d), row(d)],
        out_shape=[jax.ShapeDtypeStruct((t, d), F32), jax.ShapeDtypeStruct((t, d), BF16)],
        compiler_params=_cparams("parallel"),
        name="mix_ln",
    )(y_attn, y_ssm, gates, h, wglu, wau, wsu, wout, g, b)


def _mlp_kernel(xb_ref, xf_ref, wup_ref, wdn_ref, g_ref, b_ref, *refs, alpha, n_batch, final):
    out_refs, acc_ref = refs[:-1], refs[-1]
    i, j = pl.program_id(0), pl.program_id(1)
    tm, d = xb_ref.shape
    a = jnp.maximum(jnp.dot(xb_ref[...], wup_ref[...], preferred_element_type=F32), 0.0)
    part = jnp.dot((a * a).astype(BF16), wdn_ref[...], preferred_element_type=F32)

    @pl.when(j == 0)
    def _():
        acc_ref[...] = part

    @pl.when(j > 0)
    def _():
        acc_ref[...] += part

    @pl.when(j == pl.num_programs(1) - 1)
    def _():
        out = _layer_norm(alpha * xf_ref[...] + acc_ref[...], g_ref[...], b_ref[...])
        if final:
            out_refs[0][...] = out.reshape(n_batch, BLOCK, d)
        else:
            out = jnp.where(_real_row_mask(i * tm, tm, n_batch), out, 0.0)
            out_refs[0][...] = out
            out_refs[1][...] = out.astype(BF16)


def _mlp(hb, hf, wup, wdn, g, b, alpha, n_batch, final):
    t, d = hf.shape
    dff = wup.shape[1]
    tm = n_batch * BLOCK
    tf = 1024
    first = 1 if final else 0
    row = lambda: pl.BlockSpec((tm, d), lambda i, j: (i + first, 0))
    if final:
        out_specs = [pl.BlockSpec((n_batch, BLOCK, d), lambda i, j: (0, i, 0))]
        out_shape = [jax.ShapeDtypeStruct((n_batch, t // n_batch - BLOCK, d), F32)]
    else:
        out_specs = [row(), row()]
        out_shape = [jax.ShapeDtypeStruct((t, d), F32), jax.ShapeDtypeStruct((t, d), BF16)]
    return pl.pallas_call(
        functools.partial(_mlp_kernel, alpha=alpha, n_batch=n_batch, final=final),
        grid=(t // tm - first, dff // tf),
        in_specs=[row(), row(),
                  pl.BlockSpec((d, tf), lambda i, j: (0, j)),
                  pl.BlockSpec((tf, d), lambda i, j: (j, 0)),
                  pl.BlockSpec((1, d), lambda i, j: (0, 0)),
                  pl.BlockSpec((1, d), lambda i, j: (0, 0))],
        out_specs=out_specs,
        out_shape=out_shape,
        scratch_shapes=[pltpu.VMEM((tm, d), F32)],
        compiler_params=_cparams("parallel", "arbitrary"),
        name="mlp_ln",
    )(hb, hf, wup, wdn, g, b)


def kernel(x, meta_tokens, ln_emb_g, ln_emb_b, rel_bias, in_proj, gate_b, attn_sinks, ssm_lambda_re,
           ssm_lambda_im, ssm_log_step, ssm_b_re, ssm_b_im, ssm_c_re, ssm_c_im, ssm_d, ssm_w_glu, w_attn_up,
           w_ssm_up, w_out, ln_mix_g, ln_mix_b, w_mlp_up, w_mlp_down, ln_mlp_g, ln_mlp_b):
    bsz, seq, d = x.shape
    depth = in_proj.shape[0]
    assert seq % BLOCK == 0 and 2 * bsz == SSM_SEQS and meta_tokens.shape[0] == N_META
    nb = seq // BLOCK + 1
    alpha = (2 * depth) ** 0.25
    ssm_w = ssm_w_glu.shape[1]
    row2 = lambda a: a.astype(F32).reshape(1, -1)

    meta_pad = jnp.concatenate([jnp.zeros((META_PAD, d), x.dtype), meta_tokens.astype(x.dtype)], axis=0)
    hf, hb = _embed_ln(x, meta_pad, row2(ln_emb_g), row2(ln_emb_b))
    bias_tables = _bias_tables(rel_bias)

    for l in range(depth):
        w_in = in_proj[l]
        w_qkv = jnp.concatenate([w_in[:, :ATTN_WIDTH] * (HEAD_DIM ** -0.5), w_in[:, ATTN_WIDTH:QKV_WIDTH]], axis=1)
        qkv, gates = _in_proj(hb, w_qkv.astype(BF16), w_in[:, QKV_WIDTH + ssm_w:].astype(BF16), row2(gate_b[l]),
                              bsz * BLOCK)
        y_attn = _attention(qkv, bias_tables, attn_sinks[l].astype(F32), bsz, nb)

        ar, ai, bbr, bbi = _s5_discretize(ssm_lambda_re[l], ssm_lambda_im[l], ssm_log_step[l],
                                          ssm_b_re[l], ssm_b_im[l])
        wb, wc, a_r, a_i, d_rows = _s5_scan_weights(ar, ai, bbr, bbi, ssm_c_re[l], ssm_c_im[l], ssm_d[l])
        y_gelu = _s5_scan(hb, w_in[:, QKV_WIDTH:QKV_WIDTH + ssm_w].astype(BF16), wb, wc, a_r, a_i, d_rows, bsz)

        hf, hb = _mix(y_attn, y_gelu, gates, hf, ssm_w_glu[l].astype(BF16), w_attn_up[l].astype(BF16),
                      w_ssm_up[l].astype(BF16), w_out[l].astype(BF16), row2(ln_mix_g[l]), row2(ln_mix_b[l]),
                      alpha, bsz)
        out = _mlp(hb, hf, w_mlp_up[l].astype(BF16), w_mlp_down[l].astype(BF16),
                   row2(ln_mlp_g[l]), row2(ln_mlp_b[l]), alpha, bsz, final=(l == depth - 1))
        if l < depth - 1:
            hf, hb = out

    return out[0]
```

```python
import functools
import math

import jax
import jax.numpy as jnp
from jax import lax
from jax.experimental import pallas as pl
from jax.experimental.pallas import tpu as pltpu

F32 = jnp.float32
BF16 = jnp.bfloat16

N_META = 16
BLOCK = 128
META_PAD = BLOCK - N_META
HEAD_DIM = 64
N_Q_HEADS = 16
N_KV_HEADS = 4
Q_PER_KV = N_Q_HEADS // N_KV_HEADS
ATTN_WIDTH = N_Q_HEADS * HEAD_DIM
KV_WIDTH = N_KV_HEADS * HEAD_DIM
QKV_WIDTH = ATTN_WIDTH + 2 * KV_WIDTH
SSM_GROUP = 16
SSM_STATE = 64
N_BUCKETS = 32
MAX_DISTANCE = 128
LN_EPS = 1e-5
NEG_INF = -1e30

LANES = 128
SUBLANES = 8
VMEM_LIMIT_BYTES = 56 * 1024 * 1024

SSM_SEQS = SUBLANES
STATES_PER_CHUNK = LANES
GROUPS_PER_CHUNK = STATES_PER_CHUNK // SSM_STATE
CHUNKS_PER_SLAB = 4
SLAB_STATE_LANES = CHUNKS_PER_SLAB * 2 * STATES_PER_CHUNK


def _cparams(*sem):
    return pltpu.CompilerParams(dimension_semantics=sem, vmem_limit_bytes=VMEM_LIMIT_BYTES)


def _resident(shape):
    nd = len(shape)
    return pl.BlockSpec(shape, lambda *_: (0,) * nd, pipeline_mode=pl.Buffered(1))


def _layer_norm(x, g, b):
    mu = jnp.mean(x, axis=-1, keepdims=True)
    xc = x - mu
    var = jnp.mean(xc * xc, axis=-1, keepdims=True)
    return xc * lax.rsqrt(var + LN_EPS) * g + b


def _sigmoid(z):
    return 1.0 / (1.0 + jnp.exp(-z))


def _real_row_mask(first_row, n_rows, n_batch):
    r = first_row + lax.broadcasted_iota(jnp.int32, (n_rows, 1), 0)
    is_pad = (r < n_batch * BLOCK) & ((r & (BLOCK - 1)) < META_PAD)
    return jnp.logical_not(is_pad)


def _embed_ln_kernel(meta_ref, x_ref, g_ref, b_ref, hf_ref, hb_ref):
    is_meta = pl.program_id(0) == 0
    rows = lax.broadcasted_iota(jnp.int32, (BLOCK, 1), 0)
    keep = jnp.logical_or(jnp.logical_not(is_meta), rows >= META_PAD)
    for bi in range(x_ref.shape[0]):
        xin = jnp.where(is_meta, meta_ref[...], x_ref[bi])
        y = jnp.where(keep, _layer_norm(xin, g_ref[...], b_ref[...]), 0.0)
        hf_ref[bi * BLOCK:(bi + 1) * BLOCK, :] = y
        hb_ref[bi * BLOCK:(bi + 1) * BLOCK, :] = y.astype(BF16)


def _embed_ln(x, meta_pad, g, b):
    bsz, seq, d = x.shape
    nb = seq // BLOCK + 1
    t = bsz * nb * BLOCK
    tile = pl.BlockSpec((bsz * BLOCK, d), lambda n: (n, 0))
    return pl.pallas_call(
        _embed_ln_kernel,
        grid=(nb,),
        in_specs=[
            _resident(meta_pad.shape),
            pl.BlockSpec((bsz, BLOCK, d), lambda n: (0, jnp.maximum(n - 1, 0), 0)),
            _resident(g.shape), _resident(b.shape),
        ],
        out_specs=[tile, tile],
        out_shape=[jax.ShapeDtypeStruct((t, d), F32), jax.ShapeDtypeStruct((t, d), BF16)],
        compiler_params=_cparams("parallel"),
        name="embed_ln",
    )(meta_pad, x, g, b)


GATE_COL_CHUNK = 1024


def _in_proj_kernel(x_ref, wqkv_ref, wg_ref, gb_ref, qkv_ref, gate_ref):
    x = x_ref[...]
    qkv_ref[...] = jnp.dot(x, wqkv_ref[...], preferred_element_type=F32).astype(qkv_ref.dtype)
    for c in range(0, wg_ref.shape[1], GATE_COL_CHUNK):
        z = jnp.dot(x, wg_ref[:, c:c + GATE_COL_CHUNK], preferred_element_type=F32) + gb_ref[:, c:c + GATE_COL_CHUNK]
        gate_ref[:, c:c + GATE_COL_CHUNK] = _sigmoid(z).astype(gate_ref.dtype)


def _in_proj(hb, w_qkv, w_gate, gate_bias, tm):
    t, d = hb.shape
    row = lambda w: pl.BlockSpec((tm, w), lambda i: (i, 0))
    return pl.pallas_call(
        _in_proj_kernel,
        grid=(t // tm,),
        in_specs=[row(d), _resident(w_qkv.shape), _resident(w_gate.shape), _resident(gate_bias.shape)],
        out_specs=[row(w_qkv.shape[1]), row(w_gate.shape[1])],
        out_shape=[jax.ShapeDtypeStruct((t, w_qkv.shape[1]), BF16), jax.ShapeDtypeStruct((t, w_gate.shape[1]), BF16)],
        compiler_params=_cparams("parallel"),
        name="in_proj",
    )(hb, w_qkv, w_gate, gate_bias)


def _attn_kernel(sink_ref, q_ref, kvm_ref, kvp_ref, kvc_ref, bias_ref, o_ref):
    kv = jnp.concatenate([kvm_ref[...], kvp_ref[...], kvc_ref[...]], axis=0)
    rows = Q_PER_KV * BLOCK
    low = lax.broadcasted_iota(jnp.int32, (1, LANES), 1) < HEAD_DIM
    head_of_row = lax.broadcasted_iota(jnp.int32, (rows, 1), 0) // BLOCK
    zero = jnp.zeros((), q_ref.dtype)
    for kh in range(N_KV_HEADS):
        k = kv[:, kh * HEAD_DIM:(kh + 1) * HEAD_DIM]
        v = kv[:, KV_WIDTH + kh * HEAD_DIM:KV_WIDTH + (kh + 1) * HEAD_DIM]
        k2 = jnp.concatenate([k, k], axis=1)
        v2 = jnp.concatenate([v, v], axis=1)
        cols = [q_ref[:, (kh * Q_PER_KV + 2 * c) * HEAD_DIM:(kh * Q_PER_KV + 2 * c + 2) * HEAD_DIM]
                for c in range(Q_PER_KV // 2)]
        qs = jnp.concatenate([jnp.where(low, cols[g // 2], zero) if g % 2 == 0 else jnp.where(low, zero, cols[g // 2])
                              for g in range(Q_PER_KV)], axis=0)
        s = lax.dot_general(qs, k2, (((1,), (1,)), ((), ())), preferred_element_type=F32) + bias_ref[kh]
        sink = jnp.zeros((rows, 1), F32)
        for g in range(Q_PER_KV):
            sink = jnp.where(head_of_row == g, sink_ref[kh * Q_PER_KV + g], sink)
        m = jnp.maximum(jnp.max(s, axis=-1, keepdims=True), sink)
        p = jnp.exp(s - m)
        denom = jnp.sum(p, axis=-1, keepdims=True) + jnp.exp(sink - m)
        o = jnp.dot(p.astype(BF16), v2, preferred_element_type=F32) * (1.0 / denom)
        for c in range(Q_PER_KV // 2):
            pair = jnp.where(low, o[2 * c * BLOCK:(2 * c + 1) * BLOCK], o[(2 * c + 1) * BLOCK:(2 * c + 2) * BLOCK])
            o_ref[:, (kh * Q_PER_KV + 2 * c) * HEAD_DIM:(kh * Q_PER_KV + 2 * c + 2) * HEAD_DIM] = pair.astype(o_ref.dtype)


def _attention(qkv, bias_tables, sinks, bsz, nb):
    t = qkv.shape[0]
    kv_col = ATTN_WIDTH // (2 * KV_WIDTH)
    kv_spec = lambda row_fn: pl.BlockSpec((BLOCK, 2 * KV_WIDTH), lambda n, b: (row_fn(n, b), kv_col))
    return pl.pallas_call(
        _attn_kernel,
        grid=(nb, bsz),
        in_specs=[
            pl.BlockSpec(memory_space=pltpu.SMEM),
            pl.BlockSpec((BLOCK, ATTN_WIDTH), lambda n, b: (n * bsz + b, 0)),
            kv_spec(lambda n, b: b),
            kv_spec(lambda n, b: jnp.maximum(n - 1, 0) * bsz + b),
            kv_spec(lambda n, b: n * bsz + b),
            pl.BlockSpec((None, N_KV_HEADS, Q_PER_KV * BLOCK, 3 * BLOCK),
                         lambda n, b: (jnp.minimum(n, 2), 0, 0, 0)),
        ],
        out_specs=pl.BlockSpec((BLOCK, ATTN_WIDTH), lambda n, b: (n * bsz + b, 0)),
        out_shape=jax.ShapeDtypeStruct((t, ATTN_WIDTH), BF16),
        compiler_params=_cparams("parallel", "parallel"),
        name="swa",
    )(sinks, qkv, qkv, qkv, qkv, bias_tables)


def _t5_bucket(dist):
    n = jnp.maximum(dist, 0)
    max_exact = N_BUCKETS // 2
    nf = jnp.maximum(n, 1).astype(F32)
    large = max_exact + (jnp.log(nf / max_exact) / math.log(MAX_DISTANCE / max_exact)
                         * (N_BUCKETS - max_exact)).astype(jnp.int32)
    large = jnp.minimum(large, N_BUCKETS - 1)
    return jnp.where(n < max_exact, n, large)


def _bias_tables(rel_bias):
    blk = jnp.arange(3, dtype=jnp.int32)[:, None]
    j = jnp.arange(BLOCK, dtype=jnp.int32)[None, :]
    q_pos = blk * BLOCK + j
    k_pos = jnp.concatenate([jnp.broadcast_to(j, (3, BLOCK)), (blk - 1) * BLOCK + j, q_pos], axis=1)
    dist = q_pos[:, :, None] - k_pos[:, None, :]
    kp = k_pos[:, None, :]
    is_meta_seg = (jnp.arange(3 * BLOCK) < BLOCK)[None, None, :]
    valid = (dist >= 0) & jnp.where(is_meta_seg, kp >= META_PAD, (kp >= BLOCK) & (dist < BLOCK))
    onehot = jax.nn.one_hot(_t5_bucket(dist), N_BUCKETS, dtype=F32)
    bias = jnp.einsum("tqkb,bh->thqk", onehot, rel_bias.astype(F32), precision=lax.Precision.HIGHEST)
    bias = jnp.where(valid[:, None], bias, NEG_INF)
    return bias.reshape(3, N_KV_HEADS, Q_PER_KV * BLOCK, 3 * BLOCK)


def _s5_discretize_kernel(lr_ref, li_ref, ls_ref, bre_ref, bim_ref, ar_ref, ai_ref, bbr_ref, bbi_ref):
    lr, li = lr_ref[...], li_ref[...]
    dt = jnp.exp(ls_ref[...])
    decay = jnp.exp(lr * dt)
    ar, ai = decay * jnp.cos(li * dt), decay * jnp.sin(li * dt)
    den = lr * lr + li * li
    nr, ni = ar - 1.0, ai
    zr = (nr * lr + ni * li) / den
    zi = (ni * lr - nr * li) / den
    bre, bim = bre_ref[...], bim_ref[...]
    ar_ref[...] = ar
    ai_ref[...] = ai
    bbr_ref[...] = zr * bre - zi * bim
    bbi_ref[...] = zr * bim + zi * bre


def _s5_discretize(lam_re, lam_im, log_step, b_re, b_im):
    g, n, p = b_re.shape
    rep = lambda a: jnp.repeat(a.astype(F32), p, axis=-1)
    ls = jnp.broadcast_to(log_step.astype(F32)[:, None], (g, n * p))
    shp = jax.ShapeDtypeStruct((g, n * p), F32)
    ar, ai, bbr, bbi = pl.pallas_call(
        _s5_discretize_kernel, out_shape=[shp] * 4, name="s5_discretize",
    )(rep(lam_re), rep(lam_im), ls, b_re.astype(F32).reshape(g, n * p), b_im.astype(F32).reshape(g, n * p))
    return ar[:, ::p], ai[:, ::p], bbr.reshape(g, n, p), bbi.reshape(g, n, p)


def _s5_scan_weights(ar, ai, bbr, bbi, c_re, c_im, d_skip):
    g, n, p = bbr.shape
    halves, slabs, q_per, gl_per = 2, g // (2 * CHUNKS_PER_SLAB * GROUPS_PER_CHUNK), CHUNKS_PER_SLAB, GROUPS_PER_CHUNK
    gi_per = q_per * gl_per
    eye_q = jnp.eye(q_per, dtype=F32)
    eye_gl = jnp.eye(gl_per, dtype=F32)

    def split(a):
        return a.reshape((halves, slabs, q_per, gl_per) + a.shape[1:])

    b_ri = jnp.stack([split(bbr), split(bbi)], axis=0)
    wb = jnp.einsum("rhjqgnp,qa,gb->jhqgparbn", b_ri, eye_q, eye_gl)
    wb = wb.reshape(slabs, halves * gi_per * p, q_per * 2 * gl_per * n)
    c_ri = jnp.stack([split(c_re.astype(F32)), -split(c_im.astype(F32))], axis=0)
    wc = jnp.einsum("rhjqgpn,qa,gb->jarbnhqgp", c_ri, eye_q, eye_gl)
    wc = wc.reshape(slabs, q_per * 2 * gl_per * n, halves * gi_per * p)

    def lanes(a):
        return jnp.repeat(a.reshape(halves, -1), SSM_SEQS // halves, axis=0)

    d = jnp.repeat(d_skip.astype(F32).reshape(halves, -1), SSM_SEQS // halves, axis=0)
    return wb.astype(BF16), wc.astype(BF16), lanes(ar), lanes(ai), d


def _s5_scan_kernel(x_ref, wu_ref, wb_ref, wc_ref, ar_ref, ai_ref, d_ref, y_ref, u_scr, s_ref, y_scr, h_ref,
                    *, n_batch, unroll):
    steps = BLOCK
    rows = steps * SSM_SEQS
    n_slabs = wb_ref.shape[0]
    half_w = n_slabs * LANES

    @pl.when(pl.program_id(0) == 0)
    def _():
        h_ref[...] = jnp.zeros_like(h_ref)

    def seq_rows(b, half):
        return pl.ds(half * n_batch + b, steps, stride=SSM_SEQS)

    def channels(half, j):
        return slice(half * half_w + j * LANES, half * half_w + (j + 1) * LANES)

    u_all = jnp.dot(x_ref[...], wu_ref[...], preferred_element_type=F32)
    for b in range(n_batch):
        for half in range(2):
            for j in range(n_slabs):
                u_scr[j, seq_rows(b, half), :] = u_all[b * steps:(b + 1) * steps, channels(half, j)]

    half0 = (lax.broadcasted_iota(jnp.int32, (rows, 1), 0) & (SSM_SEQS - 1)) < (SSM_SEQS // 2)

    for j in range(n_slabs):
        uj = u_scr[j]
        lhs = jnp.concatenate([jnp.where(half0, uj, 0.0), jnp.where(half0, 0.0, uj)], axis=1).astype(BF16)
        s_ref[:, j * SLAB_STATE_LANES:(j + 1) * SLAB_STATE_LANES] = jnp.dot(
            lhs, wb_ref[j], preferred_element_type=F32)

    n_chunks = n_slabs * CHUNKS_PER_SLAB
    for c0 in range(0, n_chunks, CHUNKS_PER_SLAB):
        chunks = range(c0, c0 + CHUNKS_PER_SLAB)
        a_r = [ar_ref[:, c * LANES:(c + 1) * LANES] for c in chunks]
        a_i = [ai_ref[:, c * LANES:(c + 1) * LANES] for c in chunks]
        init = []
        for c in chunks:
            init += [h_ref[:, 2 * c * LANES:(2 * c + 1) * LANES], h_ref[:, (2 * c + 1) * LANES:(2 * c + 2) * LANES]]

        def body(t, carry, chunks=chunks, a_r=a_r, a_i=a_i):
            r0 = pl.multiple_of(t * SSM_SEQS, SSM_SEQS)
            new = []
            for k, c in enumerate(chunks):
                hr, hi = carry[2 * k], carry[2 * k + 1]
                re_sl = (pl.ds(r0, SSM_SEQS), pl.ds(2 * c * LANES, LANES))
                im_sl = (pl.ds(r0, SSM_SEQS), pl.ds((2 * c + 1) * LANES, LANES))
                nr = a_r[k] * hr - a_i[k] * hi + s_ref[re_sl]
                ni = a_r[k] * hi + a_i[k] * hr + s_ref[im_sl]
                s_ref[re_sl] = nr
                s_ref[im_sl] = ni
                new += [nr, ni]
            return tuple(new)

        final = lax.fori_loop(0, steps, body, tuple(init), unroll=unroll)
        for k, c in enumerate(chunks):
            h_ref[:, 2 * c * LANES:(2 * c + 1) * LANES] = final[2 * k]
            h_ref[:, (2 * c + 1) * LANES:(2 * c + 2) * LANES] = final[2 * k + 1]

    for j in range(n_slabs):
        hs = s_ref[:, j * SLAB_STATE_LANES:(j + 1) * SLAB_STATE_LANES].astype(BF16)
        both = jnp.dot(hs, wc_ref[j], preferred_element_type=F32)
        yj = jnp.where(half0, both[:, :LANES], both[:, LANES:])
        dj = jnp.tile(d_ref[:, j * LANES:(j + 1) * LANES], (steps, 1))
        y_scr[j] = jax.nn.gelu(yj + dj * u_scr[j])

    for b in range(n_batch):
        for half in range(2):
            for j in range(n_slabs):
                y_ref[b * steps:(b + 1) * steps, channels(half, j)] = y_scr[j, seq_rows(b, half), :].astype(y_ref.dtype)


def _s5_scan(hb, w_u, wb, wc, a_r, a_i, d_rows, n_batch):
    t, d = hb.shape
    tile = n_batch * BLOCK
    rows = BLOCK * SSM_SEQS
    n_slabs = wb.shape[0]
    state_lanes = n_slabs * SLAB_STATE_LANES
    ssm_w = w_u.shape[1]
    return pl.pallas_call(
        functools.partial(_s5_scan_kernel, n_batch=n_batch, unroll=True),
        grid=(t // tile,),
        in_specs=[
            pl.BlockSpec((tile, d), lambda i: (i, 0)),
            _resident(w_u.shape), _resident(wb.shape), _resident(wc.shape),
            _resident(a_r.shape), _resident(a_i.shape), _resident(d_rows.shape),
        ],
        out_specs=pl.BlockSpec((tile, ssm_w), lambda i: (i, 0)),
        out_shape=jax.ShapeDtypeStruct((t, ssm_w), BF16),
        scratch_shapes=[pltpu.VMEM((n_slabs, rows, LANES), F32), pltpu.VMEM((rows, state_lanes), F32),
                        pltpu.VMEM((n_slabs, rows, LANES), F32), pltpu.VMEM((SSM_SEQS, state_lanes), F32)],
        compiler_params=_cparams("arbitrary"),
        name="s5_scan",
    )(hb, w_u, wb, wc, a_r, a_i, d_rows)


def _mix_kernel(ya_ref, ys_ref, gate_ref, h_ref, wglu_ref, wau_ref, wsu_ref, wout_ref, g_ref, b_ref,
                hf_ref, hb_ref, *, alpha, n_batch):
    tm, d = h_ref.shape
    y = ys_ref[...]
    glu = jnp.dot(y, wglu_ref[...], preferred_element_type=F32)
    y_ssm = (y.astype(F32) * _sigmoid(glu)).astype(BF16)
    up_a = jnp.dot(ya_ref[...], wau_ref[...], preferred_element_type=F32)
    up_s = jnp.dot(y_ssm, wsu_ref[...], preferred_element_type=F32)
    merged = gate_ref[:, :d].astype(F32) * up_a + gate_ref[:, d:].astype(F32) * up_s
    mixed = jnp.dot(merged.astype(BF16), wout_ref[...], preferred_element_type=F32)
    out = _layer_norm(alpha * h_ref[...] + mixed, g_ref[...], b_ref[...])
    out = jnp.where(_real_row_mask(pl.program_id(0) * tm, tm, n_batch), out, 0.0)
    hf_ref[...] = out
    hb_ref[...] = out.astype(BF16)


def _mix(y_attn, y_ssm, gates, h, wglu, wau, wsu, wout, g, b, alpha, n_batch):
    t, d = h.shape
    tm = 2 * BLOCK
    row = lambda w: pl.BlockSpec((tm, w), lambda i: (i, 0))
    return pl.pallas_call(
        functools.partial(_mix_kernel, alpha=alpha, n_batch=n_batch),
        grid=(t // tm,),
        in_specs=[row(y_attn.shape[1]), row(y_ssm.shape[1]), row(gates.shape[1]), row(d),
                  _resident(wglu.shape), _resident(wau.shape), _resident(wsu.shape), _resident(wout.shape),
                  _resident(g.shape), _resident(b.shape)],
        out_specs=[row(d), row(d)],
        out_shape=[jax.ShapeDtypeStruct((t, d), F32), jax.ShapeDtypeStruct((t, d), BF16)],
        compiler_params=_cparams("parallel"),
        name="mix_ln",
    )(y_attn, y_ssm, gates, h, wglu, wau, wsu, wout, g, b)


def _mlp_kernel(xb_ref, xf_ref, wup_ref, wdn_ref, g_ref, b_ref, *refs, alpha, n_batch, final):
    out_refs, acc_ref = refs[:-1], refs[-1]
    i, j = pl.program_id(0), pl.program_id(1)
    tm, d = xb_ref.shape

    @pl.when(j == 0)
    def _():
        acc_ref[...] = jnp.zeros_like(acc_ref)

    a = jnp.maximum(jnp.dot(xb_ref[...], wup_ref[...], preferred_element_type=F32), 0.0)
    acc_ref[...] += jnp.dot((a * a).astype(BF16), wdn_ref[...], preferred_element_type=F32)

    @pl.when(j == pl.num_programs(1) - 1)
    def _():
        out = _layer_norm(alpha * xf_ref[...] + acc_ref[...], g_ref[...], b_ref[...])
        if final:
            out_refs[0][...] = out.reshape(n_batch, BLOCK, d)
        else:
            out = jnp.where(_real_row_mask(i * tm, tm, n_batch), out, 0.0)
            out_refs[0][...] = out
            out_refs[1][...] = out.astype(BF16)


def _mlp(hb, hf, wup, wdn, g, b, alpha, n_batch, final):
    t, d = hf.shape
    dff = wup.shape[1]
    tm = n_batch * BLOCK
    tf = 1024
    first = 1 if final else 0
    row = lambda: pl.BlockSpec((tm, d), lambda i, j: (i + first, 0))
    if final:
        out_specs = [pl.BlockSpec((n_batch, BLOCK, d), lambda i, j: (0, i, 0))]
        out_shape = [jax.ShapeDtypeStruct((n_batch, t // n_batch - BLOCK, d), F32)]
    else:
        out_specs = [row(), row()]
        out_shape = [jax.ShapeDtypeStruct((t, d), F32), jax.ShapeDtypeStruct((t, d), BF16)]
    return pl.pallas_call(
        functools.partial(_mlp_kernel, alpha=alpha, n_batch=n_batch, final=final),
        grid=(t // tm - first, dff // tf),
        in_specs=[row(), row(),
                  pl.BlockSpec((d, tf), lambda i, j: (0, j)),
                  pl.BlockSpec((tf, d), lambda i, j: (j, 0)),
                  pl.BlockSpec((1, d), lambda i, j: (0, 0)),
                  pl.BlockSpec((1, d), lambda i, j: (0, 0))],
        out_specs=out_specs,
        out_shape=out_shape,
        scratch_shapes=[pltpu.VMEM((tm, d), F32)],
        compiler_params=_cparams("parallel", "arbitrary"),
        name="mlp_ln",
    )(hb, hf, wup, wdn, g, b)


def kernel(x, meta_tokens, ln_emb_g, ln_emb_b, rel_bias, in_proj, gate_b, attn_sinks, ssm_lambda_re,
           ssm_lambda_im, ssm_log_step, ssm_b_re, ssm_b_im, ssm_c_re, ssm_c_im, ssm_d, ssm_w_glu, w_attn_up,
           w_ssm_up, w_out, ln_mix_g, ln_mix_b, w_mlp_up, w_mlp_down, ln_mlp_g, ln_mlp_b):
    bsz, seq, d = x.shape
    depth = in_proj.shape[0]
    assert seq % BLOCK == 0 and 2 * bsz == SSM_SEQS and meta_tokens.shape[0] == N_META
    nb = seq // BLOCK + 1
    alpha = (2 * depth) ** 0.25
    ssm_w = ssm_w_glu.shape[1]
    row2 = lambda a: a.astype(F32).reshape(1, -1)

    meta_pad = jnp.concatenate([jnp.zeros((META_PAD, d), x.dtype), meta_tokens.astype(x.dtype)], axis=0)
    hf, hb = _embed_ln(x, meta_pad, row2(ln_emb_g), row2(ln_emb_b))
    bias_tables = _bias_tables(rel_bias)

    for l in range(depth):
        w_in = in_proj[l]
        w_qkv = jnp.concatenate([w_in[:, :ATTN_WIDTH] * (HEAD_DIM ** -0.5), w_in[:, ATTN_WIDTH:QKV_WIDTH]], axis=1)
        qkv, gates = _in_proj(hb, w_qkv.astype(BF16), w_in[:, QKV_WIDTH + ssm_w:].astype(BF16), row2(gate_b[l]),
                              bsz * BLOCK)
        y_attn = _attention(qkv, bias_tables, attn_sinks[l].astype(F32), bsz, nb)

        ar, ai, bbr, bbi = _s5_discretize(ssm_lambda_re[l], ssm_lambda_im[l], ssm_log_step[l],
                                          ssm_b_re[l], ssm_b_im[l])
        wb, wc, a_r, a_i, d_rows = _s5_scan_weights(ar, ai, bbr, bbi, ssm_c_re[l], ssm_c_im[l], ssm_d[l])
        y_gelu = _s5_scan(hb, w_in[:, QKV_WIDTH:QKV_WIDTH + ssm_w].astype(BF16), wb, wc, a_r, a_i, d_rows, bsz)

        hf, hb = _mix(y_attn, y_gelu, gates, hf, ssm_w_glu[l].astype(BF16), w_attn_up[l].astype(BF16),
                      w_ssm_up[l].astype(BF16), w_out[l].astype(BF16), row2(ln_mix_g[l]), row2(ln_mix_b[l]),
                      alpha, bsz)
        out = _mlp(hb, hf, w_mlp_up[l].astype(BF16), w_mlp_down[l].astype(BF16),
                   row2(ln_mlp_g[l]), row2(ln_mlp_b[l]), alpha, bsz, final=(l == depth - 1))
        if l < depth - 1:
            hf, hb = out

    return out[0]
```

```python
import functools
import math

import jax
import jax.numpy as jnp
from jax import lax
from jax.experimental import pallas as pl
from jax.experimental.pallas import tpu as pltpu

F32 = jnp.float32
BF16 = jnp.bfloat16

N_META = 16
BLOCK = 128
META_PAD = BLOCK - N_META
HEAD_DIM = 64
N_Q_HEADS = 16
N_KV_HEADS = 4
Q_PER_KV = N_Q_HEADS // N_KV_HEADS
ATTN_WIDTH = N_Q_HEADS * HEAD_DIM
KV_WIDTH = N_KV_HEADS * HEAD_DIM
QKV_WIDTH = ATTN_WIDTH + 2 * KV_WIDTH
SSM_GROUP = 16
SSM_STATE = 64
N_BUCKETS = 32
MAX_DISTANCE = 128
LN_EPS = 1e-5
NEG_INF = -1e30

LANES = 128
SUBLANES = 8
VMEM_LIMIT_BYTES = 56 * 1024 * 1024

SSM_SEQS = SUBLANES
STATES_PER_CHUNK = LANES
GROUPS_PER_CHUNK = STATES_PER_CHUNK // SSM_STATE
CHUNKS_PER_SLAB = 4
SLAB_STATE_LANES = CHUNKS_PER_SLAB * 2 * STATES_PER_CHUNK


def _cparams(*sem):
    return pltpu.CompilerParams(dimension_semantics=sem, vmem_limit_bytes=VMEM_LIMIT_BYTES)


def _resident(shape):
    nd = len(shape)
    return pl.BlockSpec(shape, lambda *_: (0,) * nd, pipeline_mode=pl.Buffered(1))


def _layer_spec(shape, layer):
    nd = len(shape)
    return pl.BlockSpec((None,) + tuple(shape[1:]), lambda *_: (layer,) + (0,) * (nd - 1),
                        pipeline_mode=pl.Buffered(1))


def _layer_norm(x, g, b):
    mu = jnp.mean(x, axis=-1, keepdims=True)
    xc = x - mu
    var = jnp.mean(xc * xc, axis=-1, keepdims=True)
    return xc * lax.rsqrt(var + LN_EPS) * g + b


def _sigmoid(z):
    return 1.0 / (1.0 + jnp.exp(-z))


def _real_row_mask(first_row, n_rows, n_batch):
    r = first_row + lax.broadcasted_iota(jnp.int32, (n_rows, 1), 0)
    is_pad = (r < n_batch * BLOCK) & ((r & (BLOCK - 1)) < META_PAD)
    return jnp.logical_not(is_pad)


def _embed_ln_kernel(meta_ref, x_ref, g_ref, b_ref, hf_ref, hb_ref):
    is_meta = pl.program_id(0) == 0
    rows = lax.broadcasted_iota(jnp.int32, (BLOCK, 1), 0)
    keep = jnp.logical_or(jnp.logical_not(is_meta), rows >= META_PAD)
    for bi in range(x_ref.shape[0]):
        xin = jnp.where(is_meta, meta_ref[...], x_ref[bi])
        y = jnp.where(keep, _layer_norm(xin, g_ref[...], b_ref[...]), 0.0)
        hf_ref[bi * BLOCK:(bi + 1) * BLOCK, :] = y
        hb_ref[bi * BLOCK:(bi + 1) * BLOCK, :] = y.astype(BF16)


def _embed_ln(x, meta_pad, g, b):
    bsz, seq, d = x.shape
    nb = seq // BLOCK + 1
    t = bsz * nb * BLOCK
    tile = pl.BlockSpec((bsz * BLOCK, d), lambda n: (n, 0))
    return pl.pallas_call(
        _embed_ln_kernel,
        grid=(nb,),
        in_specs=[
            _resident(meta_pad.shape),
            pl.BlockSpec((bsz, BLOCK, d), lambda n: (0, jnp.maximum(n - 1, 0), 0)),
            _resident(g.shape), _resident(b.shape),
        ],
        out_specs=[tile, tile],
        out_shape=[jax.ShapeDtypeStruct((t, d), F32), jax.ShapeDtypeStruct((t, d), BF16)],
        compiler_params=_cparams("parallel"),
        name="embed_ln",
    )(meta_pad, x, g, b)


GATE_COL_CHUNK = 1024


def _in_proj_kernel(x_ref, wqkv_ref, wg_ref, gb_ref, qkv_ref, gate_ref):
    x = x_ref[...]
    qkv_ref[...] = jnp.dot(x, wqkv_ref[...], preferred_element_type=F32).astype(qkv_ref.dtype)
    for c in range(0, wg_ref.shape[1], GATE_COL_CHUNK):
        z = jnp.dot(x, wg_ref[:, c:c + GATE_COL_CHUNK], preferred_element_type=F32) + gb_ref[:, c:c + GATE_COL_CHUNK]
        gate_ref[:, c:c + GATE_COL_CHUNK] = _sigmoid(z).astype(gate_ref.dtype)


def _in_proj(hb, w_qkv, w_gate, gate_bias, layer, tm):
    t, d = hb.shape
    row = lambda w: pl.BlockSpec((tm, w), lambda i: (i, 0))
    return pl.pallas_call(
        _in_proj_kernel,
        grid=(t // tm,),
        in_specs=[row(d), _layer_spec(w_qkv.shape, layer), _layer_spec(w_gate.shape, layer),
                  _layer_spec(gate_bias.shape, layer)],
        out_specs=[row(w_qkv.shape[2]), row(w_gate.shape[2])],
        out_shape=[jax.ShapeDtypeStruct((t, w_qkv.shape[2]), BF16), jax.ShapeDtypeStruct((t, w_gate.shape[2]), BF16)],
        compiler_params=_cparams("parallel"),
        name="in_proj",
    )(hb, w_qkv, w_gate, gate_bias)


def _attn_kernel(sink_ref, q_ref, kvm_ref, kvp_ref, kvc_ref, bias_ref, o_ref, *, layer):
    kv = jnp.concatenate([kvm_ref[...], kvp_ref[...], kvc_ref[...]], axis=0)
    rows = Q_PER_KV * BLOCK
    low = lax.broadcasted_iota(jnp.int32, (1, LANES), 1) < HEAD_DIM
    head_of_row = lax.broadcasted_iota(jnp.int32, (rows, 1), 0) // BLOCK
    zero = jnp.zeros((), q_ref.dtype)
    for kh in range(N_KV_HEADS):
        k = kv[:, kh * HEAD_DIM:(kh + 1) * HEAD_DIM]
        v = kv[:, KV_WIDTH + kh * HEAD_DIM:KV_WIDTH + (kh + 1) * HEAD_DIM]
        k2 = jnp.concatenate([k, k], axis=1)
        v2 = jnp.concatenate([v, v], axis=1)
        cols = [q_ref[:, (kh * Q_PER_KV + 2 * c) * HEAD_DIM:(kh * Q_PER_KV + 2 * c + 2) * HEAD_DIM]
                for c in range(Q_PER_KV // 2)]
        qs = jnp.concatenate([jnp.where(low, cols[g // 2], zero) if g % 2 == 0 else jnp.where(low, zero, cols[g // 2])
                              for g in range(Q_PER_KV)], axis=0)
        s = lax.dot_general(qs, k2, (((1,), (1,)), ((), ())), preferred_element_type=F32) + bias_ref[kh]
        sink = jnp.zeros((rows, 1), F32)
        for g in range(Q_PER_KV):
            sink = jnp.where(head_of_row == g, sink_ref[layer, kh * Q_PER_KV + g], sink)
        m = jnp.maximum(jnp.max(s, axis=-1, keepdims=True), sink)
        p = jnp.exp(s - m)
        denom = jnp.sum(p, axis=-1, keepdims=True) + jnp.exp(sink - m)
        o = jnp.dot(p.astype(BF16), v2, preferred_element_type=F32) * (1.0 / denom)
        for c in range(Q_PER_KV // 2):
            pair = jnp.where(low, o[2 * c * BLOCK:(2 * c + 1) * BLOCK], o[(2 * c + 1) * BLOCK:(2 * c + 2) * BLOCK])
            o_ref[:, (kh * Q_PER_KV + 2 * c) * HEAD_DIM:(kh * Q_PER_KV + 2 * c + 2) * HEAD_DIM] = pair.astype(o_ref.dtype)


def _attention(qkv, bias_tables, sinks, layer, bsz, nb):
    t = qkv.shape[0]
    kv_col = ATTN_WIDTH // (2 * KV_WIDTH)
    kv_spec = lambda row_fn: pl.BlockSpec((BLOCK, 2 * KV_WIDTH), lambda n, b: (row_fn(n, b), kv_col))
    return pl.pallas_call(
        functools.partial(_attn_kernel, layer=layer),
        grid=(nb, bsz),
        in_specs=[
            pl.BlockSpec(memory_space=pltpu.SMEM),
            pl.BlockSpec((BLOCK, ATTN_WIDTH), lambda n, b: (n * bsz + b, 0)),
            kv_spec(lambda n, b: b),
            kv_spec(lambda n, b: jnp.maximum(n - 1, 0) * bsz + b),
            kv_spec(lambda n, b: n * bsz + b),
            pl.BlockSpec((None, N_KV_HEADS, Q_PER_KV * BLOCK, 3 * BLOCK),
                         lambda n, b: (jnp.minimum(n, 2), 0, 0, 0)),
        ],
        out_specs=pl.BlockSpec((BLOCK, ATTN_WIDTH), lambda n, b: (n * bsz + b, 0)),
        out_shape=jax.ShapeDtypeStruct((t, ATTN_WIDTH), BF16),
        compiler_params=_cparams("parallel", "parallel"),
        name="swa",
    )(sinks, qkv, qkv, qkv, qkv, bias_tables)


def _t5_bucket(dist):
    n = jnp.maximum(dist, 0)
    max_exact = N_BUCKETS // 2
    nf = jnp.maximum(n, 1).astype(F32)
    large = max_exact + (jnp.log(nf / max_exact) / math.log(MAX_DISTANCE / max_exact)
                         * (N_BUCKETS - max_exact)).astype(jnp.int32)
    large = jnp.minimum(large, N_BUCKETS - 1)
    return jnp.where(n < max_exact, n, large)


def _bias_tables(rel_bias):
    blk = jnp.arange(3, dtype=jnp.int32)[:, None]
    j = jnp.arange(BLOCK, dtype=jnp.int32)[None, :]
    q_pos = blk * BLOCK + j
    k_pos = jnp.concatenate([jnp.broadcast_to(j, (3, BLOCK)), (blk - 1) * BLOCK + j, q_pos], axis=1)
    dist = q_pos[:, :, None] - k_pos[:, None, :]
    kp = k_pos[:, None, :]
    is_meta_seg = (jnp.arange(3 * BLOCK) < BLOCK)[None, None, :]
    valid = (dist >= 0) & jnp.where(is_meta_seg, kp >= META_PAD, (kp >= BLOCK) & (dist < BLOCK))
    onehot = jax.nn.one_hot(_t5_bucket(dist), N_BUCKETS, dtype=F32)
    bias = jnp.einsum("tqkb,bh->thqk", onehot, rel_bias.astype(F32), precision=lax.Precision.HIGHEST)
    bias = jnp.where(valid[:, None], bias, NEG_INF)
    return bias.reshape(3, N_KV_HEADS, Q_PER_KV * BLOCK, 3 * BLOCK)


def _s5_discretize_kernel(lr_ref, li_ref, ls_ref, bre_ref, bim_ref, ar_ref, ai_ref, bbr_ref, bbi_ref):
    lr, li = lr_ref[...], li_ref[...]
    dt = jnp.exp(ls_ref[...])
    decay = jnp.exp(lr * dt)
    ar, ai = decay * jnp.cos(li * dt), decay * jnp.sin(li * dt)
    den = lr * lr + li * li
    nr, ni = ar - 1.0, ai
    zr = (nr * lr + ni * li) / den
    zi = (ni * lr - nr * li) / den
    bre, bim = bre_ref[...], bim_ref[...]
    ar_ref[...] = ar
    ai_ref[...] = ai
    bbr_ref[...] = zr * bre - zi * bim
    bbi_ref[...] = zr * bim + zi * bre


def _s5_discretize(lam_re, lam_im, log_step, b_re, b_im):
    depth, g, n, p = b_re.shape
    flat = lambda a: a.astype(F32).reshape(depth * g, n * p)
    rep = lambda a: flat(jnp.broadcast_to(a.astype(F32)[..., None], (depth, g, n, p)))
    ls = flat(jnp.broadcast_to(log_step.astype(F32)[:, :, None, None], (depth, g, n, p)))
    shp = jax.ShapeDtypeStruct((depth * g, n * p), F32)
    ar, ai, bbr, bbi = pl.pallas_call(
        _s5_discretize_kernel, out_shape=[shp] * 4, name="s5_discretize",
    )(rep(lam_re), rep(lam_im), ls, flat(b_re), flat(b_im))
    state = lambda a: a.reshape(depth, g, n, p)[..., 0]
    return state(ar), state(ai), bbr.reshape(depth, g, n, p), bbi.reshape(depth, g, n, p)


def _s5_scan_weights(ar, ai, bbr, bbi, c_re, c_im, d_skip):
    depth, g, n, p = bbr.shape
    halves, q_per, gl_per = 2, CHUNKS_PER_SLAB, GROUPS_PER_CHUNK
    gi_per = q_per * gl_per
    slabs = g // (halves * gi_per)
    gi = jnp.arange(gi_per)
    own = gi[:, None, None] == (gl_per * jnp.arange(q_per)[None, :, None] + jnp.arange(gl_per)[None, None, :])

    b_ri = jnp.stack([bbr, bbi], axis=1).reshape(depth, 2, halves, slabs, gi_per, n, p)
    b_ri = b_ri.transpose(0, 3, 2, 4, 6, 1, 5)
    wb = jnp.where(own[None, None, None, :, None, :, None, :, None],
                   b_ri[:, :, :, :, :, None, :, None, :], 0.0)
    wb = wb.reshape(depth, slabs, halves * gi_per * p, q_per * 2 * gl_per * n)

    c_ri = jnp.stack([c_re.astype(F32), -c_im.astype(F32)], axis=1).reshape(depth, 2, halves, slabs, gi_per, p, n)
    c_ri = c_ri.transpose(0, 3, 1, 6, 2, 4, 5)
    own_t = own.transpose(1, 2, 0)
    wc = jnp.where(own_t[None, None, :, None, :, None, None, :, None],
                   c_ri[:, :, None, :, None, :, :, :, :], 0.0)
    wc = wc.reshape(depth, slabs, q_per * 2 * gl_per * n, halves * gi_per * p)

    def per_seq(a):
        return jnp.repeat(a.astype(F32).reshape(depth, halves, -1), SSM_SEQS // halves, axis=1)

    return wb.astype(BF16), wc.astype(BF16), per_seq(ar), per_seq(ai), per_seq(d_skip)


def _s5_scan_kernel(x_ref, wu_ref, wb_ref, wc_ref, ar_ref, ai_ref, d_ref, y_ref, u_scr, s_ref, y_scr, h_ref,
                    *, n_batch, unroll):
    steps = BLOCK
    rows = steps * SSM_SEQS
    n_slabs = wb_ref.shape[0]
    half_w = n_slabs * LANES

    @pl.when(pl.program_id(0) == 0)
    def _():
        h_ref[...] = jnp.zeros_like(h_ref)

    def seq_rows(b, half):
        return pl.ds(half * n_batch + b, steps, stride=SSM_SEQS)

    def channels(half, j):
        return slice(half * half_w + j * LANES, half * half_w + (j + 1) * LANES)

    u_all = jnp.dot(x_ref[...], wu_ref[...], preferred_element_type=F32)
    for b in range(n_batch):
        for half in range(2):
            for j in range(n_slabs):
                u_scr[j, seq_rows(b, half), :] = u_all[b * steps:(b + 1) * steps, channels(half, j)]

    half0 = (lax.broadcasted_iota(jnp.int32, (rows, 1), 0) & (SSM_SEQS - 1)) < (SSM_SEQS // 2)

    for j in range(n_slabs):
        uj = u_scr[j]
        lhs = jnp.concatenate([jnp.where(half0, uj, 0.0), jnp.where(half0, 0.0, uj)], axis=1).astype(BF16)
        s_ref[:, j * SLAB_STATE_LANES:(j + 1) * SLAB_STATE_LANES] = jnp.dot(
            lhs, wb_ref[j], preferred_element_type=F32)

    n_chunks = n_slabs * CHUNKS_PER_SLAB
    for c0 in range(0, n_chunks, CHUNKS_PER_SLAB):
        chunks = range(c0, c0 + CHUNKS_PER_SLAB)
        a_r = [ar_ref[:, c * LANES:(c + 1) * LANES] for c in chunks]
        a_i = [ai_ref[:, c * LANES:(c + 1) * LANES] for c in chunks]
        init = []
        for c in chunks:
            init += [h_ref[:, 2 * c * LANES:(2 * c + 1) * LANES], h_ref[:, (2 * c + 1) * LANES:(2 * c + 2) * LANES]]

        def body(t, carry, chunks=chunks, a_r=a_r, a_i=a_i):
            r0 = pl.multiple_of(t * SSM_SEQS, SSM_SEQS)
            new = []
            for k, c in enumerate(chunks):
                hr, hi = carry[2 * k], carry[2 * k + 1]
                re_sl = (pl.ds(r0, SSM_SEQS), pl.ds(2 * c * LANES, LANES))
                im_sl = (pl.ds(r0, SSM_SEQS), pl.ds((2 * c + 1) * LANES, LANES))
                nr = a_r[k] * hr - a_i[k] * hi + s_ref[re_sl]
                ni = a_r[k] * hi + a_i[k] * hr + s_ref[im_sl]
                s_ref[re_sl] = nr
                s_ref[im_sl] = ni
                new += [nr, ni]
            return tuple(new)

        final = lax.fori_loop(0, steps, body, tuple(init), unroll=unroll)
        for k, c in enumerate(chunks):
            h_ref[:, 2 * c * LANES:(2 * c + 1) * LANES] = final[2 * k]
            h_ref[:, (2 * c + 1) * LANES:(2 * c + 2) * LANES] = final[2 * k + 1]

    for j in range(n_slabs):
        hs = s_ref[:, j * SLAB_STATE_LANES:(j + 1) * SLAB_STATE_LANES].astype(BF16)
        both = jnp.dot(hs, wc_ref[j], preferred_element_type=F32)
        yj = jnp.where(half0, both[:, :LANES], both[:, LANES:])
        dj = jnp.tile(d_ref[:, j * LANES:(j + 1) * LANES], (steps, 1))
        y_scr[j] = jax.nn.gelu(yj + dj * u_scr[j])

    for b in range(n_batch):
        for half in range(2):
            for j in range(n_slabs):
                y_ref[b * steps:(b + 1) * steps, channels(half, j)] = y_scr[j, seq_rows(b, half), :].astype(y_ref.dtype)


def _s5_scan(hb, w_u, wb, wc, a_r, a_i, d_rows, layer, n_batch):
    t, d = hb.shape
    tile = n_batch * BLOCK
    rows = BLOCK * SSM_SEQS
    n_slabs = wb.shape[1]
    state_lanes = n_slabs * SLAB_STATE_LANES
    ssm_w = w_u.shape[2]
    return pl.pallas_call(
        functools.partial(_s5_scan_kernel, n_batch=n_batch, unroll=True),
        grid=(t // tile,),
        in_specs=[
            pl.BlockSpec((tile, d), lambda i: (i, 0)),
            _layer_spec(w_u.shape, layer), _layer_spec(wb.shape, layer), _layer_spec(wc.shape, layer),
            _layer_spec(a_r.shape, layer), _layer_spec(a_i.shape, layer), _layer_spec(d_rows.shape, layer),
        ],
        out_specs=pl.BlockSpec((tile, ssm_w), lambda i: (i, 0)),
        out_shape=jax.ShapeDtypeStruct((t, ssm_w), BF16),
        scratch_shapes=[pltpu.VMEM((n_slabs, rows, LANES), F32), pltpu.VMEM((rows, state_lanes), F32),
                        pltpu.VMEM((n_slabs, rows, LANES), F32), pltpu.VMEM((SSM_SEQS, state_lanes), F32)],
        compiler_params=_cparams("arbitrary"),
        name="s5_scan",
    )(hb, w_u, wb, wc, a_r, a_i, d_rows)


def _mix_kernel(ya_ref, ys_ref, gate_ref, h_ref, wglu_ref, wau_ref, wsu_ref, wout_ref, g_ref, b_ref,
                hf_ref, hb_ref, *, alpha, n_batch):
    tm, d = h_ref.shape
    y = ys_ref[...]
    glu = jnp.dot(y, wglu_ref[...], preferred_element_type=F32)
    y_ssm = (y.astype(F32) * _sigmoid(glu)).astype(BF16)
    up_a = jnp.dot(ya_ref[...], wau_ref[...], preferred_element_type=F32)
    up_s = jnp.dot(y_ssm, wsu_ref[...], preferred_element_type=F32)
    merged = gate_ref[:, :d].astype(F32) * up_a + gate_ref[:, d:].astype(F32) * up_s
    mixed = jnp.dot(merged.astype(BF16), wout_ref[...], preferred_element_type=F32)
    out = _layer_norm(alpha * h_ref[...] + mixed, g_ref[...], b_ref[...])
    out = jnp.where(_real_row_mask(pl.program_id(0) * tm, tm, n_batch), out, 0.0)
    hf_ref[...] = out
    hb_ref[...] = out.astype(BF16)


def _mix(y_attn, y_ssm, gates, h, wglu, wau, wsu, wout, g, b, layer, alpha, n_batch):
    t, d = h.shape
    tm = 2 * BLOCK
    row = lambda w: pl.BlockSpec((tm, w), lambda i: (i, 0))
    return pl.pallas_call(
        functools.partial(_mix_kernel, alpha=alpha, n_batch=n_batch),
        grid=(t // tm,),
        in_specs=[row(y_attn.shape[1]), row(y_ssm.shape[1]), row(gates.shape[1]), row(d),
                  _layer_spec(wglu.shape, layer), _layer_spec(wau.shape, layer), _layer_spec(wsu.shape, layer),
                  _layer_spec(wout.shape, layer), _layer_spec(g.shape, layer), _layer_spec(b.shape, layer)],
        out_specs=[row(d), row(d)],
        out_shape=[jax.ShapeDtypeStruct((t, d), F32), jax.ShapeDtypeStruct((t, d), BF16)],
        compiler_params=_cparams("parallel"),
        name="mix_ln",
    )(y_attn, y_ssm, gates, h, wglu, wau, wsu, wout, g, b)


def _mlp_kernel(xb_ref, xf_ref, wup_ref, wdn_ref, g_ref, b_ref, *refs, alpha, n_batch, final):
    out_refs, acc_ref = refs[:-1], refs[-1]
    i, j = pl.program_id(0), pl.program_id(1)
    tm, d = xb_ref.shape

    @pl.when(j == 0)
    def _():
        acc_ref[...] = jnp.zeros_like(acc_ref)

    a = jnp.maximum(jnp.dot(xb_ref[...], wup_ref[...], preferred_element_type=F32), 0.0)
    acc_ref[...] += jnp.dot((a * a).astype(BF16), wdn_ref[...], preferred_element_type=F32)

    @pl.when(j == pl.num_programs(1) - 1)
    def _():
        out = _layer_norm(alpha * xf_ref[...] + acc_ref[...], g_ref[...], b_ref[...])
        if final:
            out_refs[0][...] = out.reshape(n_batch, BLOCK, d)
        else:
            out = jnp.where(_real_row_mask(i * tm, tm, n_batch), out, 0.0)
            out_refs[0][...] = out
            out_refs[1][...] = out.astype(BF16)


def _mlp(hb, hf, wup, wdn, g, b, layer, alpha, n_batch, final):
    t, d = hf.shape
    dff = wup.shape[2]
    tm = n_batch * BLOCK
    tf = 1024
    first = 1 if final else 0
    row = lambda: pl.BlockSpec((tm, d), lambda i, j: (i + first, 0))
    if final:
        out_specs = [pl.BlockSpec((n_batch, BLOCK, d), lambda i, j: (0, i, 0))]
        out_shape = [jax.ShapeDtypeStruct((n_batch, t // n_batch - BLOCK, d), F32)]
    else:
        out_specs = [row(), row()]
        out_shape = [jax.ShapeDtypeStruct((t, d), F32), jax.ShapeDtypeStruct((t, d), BF16)]
    return pl.pallas_call(
        functools.partial(_mlp_kernel, alpha=alpha, n_batch=n_batch, final=final),
        grid=(t // tm - first, dff // tf),
        in_specs=[row(), row(),
                  pl.BlockSpec((None, d, tf), lambda i, j: (layer, 0, j)),
                  pl.BlockSpec((None, tf, d), lambda i, j: (layer, j, 0)),
                  _layer_spec(g.shape, layer), _layer_spec(b.shape, layer)],
        out_specs=out_specs,
        out_shape=out_shape,
        scratch_shapes=[pltpu.VMEM((tm, d), F32)],
        compiler_params=_cparams("parallel", "arbitrary"),
        name="mlp_ln",
    )(hb, hf, wup, wdn, g, b)


def kernel(x, meta_tokens, ln_emb_g, ln_emb_b, rel_bias, in_proj, gate_b, attn_sinks, ssm_lambda_re,
           ssm_lambda_im, ssm_log_step, ssm_b_re, ssm_b_im, ssm_c_re, ssm_c_im, ssm_d, ssm_w_glu, w_attn_up,
           w_ssm_up, w_out, ln_mix_g, ln_mix_b, w_mlp_up, w_mlp_down, ln_mlp_g, ln_mlp_b):
    bsz, seq, d = x.shape
    depth = in_proj.shape[0]
    assert seq % BLOCK == 0 and 2 * bsz == SSM_SEQS and meta_tokens.shape[0] == N_META
    nb = seq // BLOCK + 1
    alpha = (2 * depth) ** 0.25
    ssm_w = ssm_w_glu.shape[2]
    row2 = lambda a: a.astype(F32).reshape(1, -1)
    rows = lambda a: a.astype(F32)[:, None, :]
    bf = lambda a: a.astype(BF16)

    meta_pad = jnp.concatenate([jnp.zeros((META_PAD, d), x.dtype), meta_tokens.astype(x.dtype)], axis=0)
    hf, hb = _embed_ln(x, meta_pad, row2(ln_emb_g), row2(ln_emb_b))
    bias_tables = _bias_tables(rel_bias)

    q_scale = jnp.where(jnp.arange(QKV_WIDTH) < ATTN_WIDTH, HEAD_DIM ** -0.5, 1.0).astype(in_proj.dtype)
    w_qkv = bf(in_proj[:, :, :QKV_WIDTH] * q_scale)
    w_u = bf(in_proj[:, :, QKV_WIDTH:QKV_WIDTH + ssm_w])
    w_gate = bf(in_proj[:, :, QKV_WIDTH + ssm_w:])
    w_glu, w_au, w_su, w_o = bf(ssm_w_glu), bf(w_attn_up), bf(w_ssm_up), bf(w_out)
    w_up, w_dn = bf(w_mlp_up), bf(w_mlp_down)
    sinks = attn_sinks.astype(F32)
    ar, ai, bbr, bbi = _s5_discretize(ssm_lambda_re, ssm_lambda_im, ssm_log_step, ssm_b_re, ssm_b_im)
    wb, wc, a_r, a_i, d_rows = _s5_scan_weights(ar, ai, bbr, bbi, ssm_c_re, ssm_c_im, ssm_d)

    for l in range(depth):
        qkv, gates = _in_proj(hb, w_qkv, w_gate, rows(gate_b), l, bsz * BLOCK)
        y_attn = _attention(qkv, bias_tables, sinks, l, bsz, nb)
        y_gelu = _s5_scan(hb, w_u, wb, wc, a_r, a_i, d_rows, l, bsz)
        hf, hb = _mix(y_attn, y_gelu, gates, hf, w_glu, w_au, w_su, w_o, rows(ln_mix_g), rows(ln_mix_b),
                      l, alpha, bsz)
        out = _mlp(hb, hf, w_up, w_dn, rows(ln_mlp_g), rows(ln_mlp_b), l, alpha, bsz, final=(l == depth - 1))
        if l < depth - 1:
            hf, hb = out

    return out[0]
```

```python
import functools
import math

import jax
import jax.numpy as jnp
from jax import lax
from jax.experimental import pallas as pl
from jax.experimental.pallas import tpu as pltpu

F32 = jnp.float32
BF16 = jnp.bfloat16

N_META = 16
BLOCK = 128
META_PAD = BLOCK - N_META
HEAD_DIM = 64
N_Q_HEADS = 16
N_KV_HEADS = 4
Q_PER_KV = N_Q_HEADS // N_KV_HEADS
ATTN_WIDTH = N_Q_HEADS * HEAD_DIM
KV_WIDTH = N_KV_HEADS * HEAD_DIM
QKV_WIDTH = ATTN_WIDTH + 2 * KV_WIDTH
SSM_GROUP = 16
SSM_STATE = 64
N_BUCKETS = 32
MAX_DISTANCE = 128
LN_EPS = 1e-5
NEG_INF = -1e30
LOG2_E = math.log2(math.e)

LANES = 128
SUBLANES = 8
VMEM_LIMIT_BYTES = 56 * 1024 * 1024

SSM_SEQS = SUBLANES
STATES_PER_CHUNK = LANES
GROUPS_PER_CHUNK = STATES_PER_CHUNK // SSM_STATE
CHUNKS_PER_SLAB = 4
SLAB_STATE_LANES = CHUNKS_PER_SLAB * 2 * STATES_PER_CHUNK


def _cparams(*sem):
    return pltpu.CompilerParams(dimension_semantics=sem, vmem_limit_bytes=VMEM_LIMIT_BYTES)


def _resident(shape):
    nd = len(shape)
    return pl.BlockSpec(shape, lambda *_: (0,) * nd, pipeline_mode=pl.Buffered(1))


def _layer_spec(shape, layer):
    nd = len(shape)
    return pl.BlockSpec((None,) + tuple(shape[1:]), lambda *_: (layer,) + (0,) * (nd - 1),
                        pipeline_mode=pl.Buffered(1))


def _layer_norm(x, g, b):
    mu = jnp.mean(x, axis=-1, keepdims=True)
    xc = x - mu
    var = jnp.mean(xc * xc, axis=-1, keepdims=True)
    return xc * lax.rsqrt(var + LN_EPS) * g + b


def _sigmoid(z):
    return 1.0 / (1.0 + jnp.exp(-z))


def _real_row_mask(first_row, n_rows, n_batch):
    r = first_row + lax.broadcasted_iota(jnp.int32, (n_rows, 1), 0)
    is_pad = (r < n_batch * BLOCK) & ((r & (BLOCK - 1)) < META_PAD)
    return jnp.logical_not(is_pad)


def _embed_ln_kernel(meta_ref, x_ref, g_ref, b_ref, hf_ref, hb_ref):
    is_meta = pl.program_id(0) == 0
    rows = lax.broadcasted_iota(jnp.int32, (BLOCK, 1), 0)
    keep = jnp.logical_or(jnp.logical_not(is_meta), rows >= META_PAD)
    for bi in range(x_ref.shape[0]):
        xin = jnp.where(is_meta, meta_ref[...], x_ref[bi])
        y = jnp.where(keep, _layer_norm(xin, g_ref[...], b_ref[...]), 0.0)
        hf_ref[bi * BLOCK:(bi + 1) * BLOCK, :] = y
        hb_ref[bi * BLOCK:(bi + 1) * BLOCK, :] = y.astype(BF16)


def _embed_ln(x, meta_pad, g, b):
    bsz, seq, d = x.shape
    nb = seq // BLOCK + 1
    t = bsz * nb * BLOCK
    tile = pl.BlockSpec((bsz * BLOCK, d), lambda n: (n, 0))
    return pl.pallas_call(
        _embed_ln_kernel,
        grid=(nb,),
        in_specs=[
            _resident(meta_pad.shape),
            pl.BlockSpec((bsz, BLOCK, d), lambda n: (0, jnp.maximum(n - 1, 0), 0)),
            _resident(g.shape), _resident(b.shape),
        ],
        out_specs=[tile, tile],
        out_shape=[jax.ShapeDtypeStruct((t, d), F32), jax.ShapeDtypeStruct((t, d), BF16)],
        compiler_params=_cparams("parallel"),
        name="embed_ln",
    )(meta_pad, x, g, b)


GATE_COL_CHUNK = 1024


def _in_proj_kernel(x_ref, wqkv_ref, wg_ref, gb_ref, qkv_ref, gate_ref):
    x = x_ref[...]
    qkv_ref[...] = jnp.dot(x, wqkv_ref[...], preferred_element_type=F32).astype(qkv_ref.dtype)
    for c in range(0, wg_ref.shape[1], GATE_COL_CHUNK):
        z = jnp.dot(x, wg_ref[:, c:c + GATE_COL_CHUNK], preferred_element_type=F32) + gb_ref[:, c:c + GATE_COL_CHUNK]
        gate_ref[:, c:c + GATE_COL_CHUNK] = _sigmoid(z).astype(gate_ref.dtype)


def _in_proj(hb, w_qkv, w_gate, gate_bias, layer, tm):
    t, d = hb.shape
    row = lambda w: pl.BlockSpec((tm, w), lambda i: (i, 0))
    return pl.pallas_call(
        _in_proj_kernel,
        grid=(t // tm,),
        in_specs=[row(d), _layer_spec(w_qkv.shape, layer), _layer_spec(w_gate.shape, layer),
                  _layer_spec(gate_bias.shape, layer)],
        out_specs=[row(w_qkv.shape[2]), row(w_gate.shape[2])],
        out_shape=[jax.ShapeDtypeStruct((t, w_qkv.shape[2]), BF16), jax.ShapeDtypeStruct((t, w_gate.shape[2]), BF16)],
        compiler_params=_cparams("parallel"),
        name="in_proj",
    )(hb, w_qkv, w_gate, gate_bias)


def _attn_kernel(sink_ref, q_ref, kvm_ref, kvp_ref, kvc_ref, bias_ref, o_ref, *, layer):
    kv = jnp.concatenate([kvm_ref[...], kvp_ref[...], kvc_ref[...]], axis=0)
    rows = Q_PER_KV * BLOCK
    low = lax.broadcasted_iota(jnp.int32, (1, LANES), 1) < HEAD_DIM
    head_of_row = lax.broadcasted_iota(jnp.int32, (rows, 1), 0) // BLOCK
    zero = jnp.zeros((), q_ref.dtype)
    ones = jnp.ones((kv.shape[0], LANES), kv.dtype)
    for kh in range(N_KV_HEADS):
        k = kv[:, kh * HEAD_DIM:(kh + 1) * HEAD_DIM]
        v = kv[:, KV_WIDTH + kh * HEAD_DIM:KV_WIDTH + (kh + 1) * HEAD_DIM]
        k2 = jnp.concatenate([k, k], axis=1)
        v3 = jnp.concatenate([v, v, ones], axis=1)
        cols = [q_ref[:, (kh * Q_PER_KV + 2 * c) * HEAD_DIM:(kh * Q_PER_KV + 2 * c + 2) * HEAD_DIM]
                for c in range(Q_PER_KV // 2)]
        qs = jnp.concatenate([jnp.where(low, cols[g // 2], zero) if g % 2 == 0 else jnp.where(low, zero, cols[g // 2])
                              for g in range(Q_PER_KV)], axis=0)
        s = lax.dot_general(qs, k2, (((1,), (1,)), ((), ())), preferred_element_type=F32) + bias_ref[kh]
        sink = jnp.zeros((rows, 1), F32)
        for g in range(Q_PER_KV):
            sink = jnp.where(head_of_row == g, sink_ref[layer, kh * Q_PER_KV + g], sink)
        m = jnp.maximum(jnp.max(s, axis=-1, keepdims=True), sink)
        p = jnp.exp2(s - m)
        pv = jnp.dot(p.astype(BF16), v3, preferred_element_type=F32)
        denom = pv[:, 2 * HEAD_DIM:] + jnp.exp2(sink - m)
        o = pv[:, :2 * HEAD_DIM] * (1.0 / denom)
        for c in range(Q_PER_KV // 2):
            pair = jnp.where(low, o[2 * c * BLOCK:(2 * c + 1) * BLOCK], o[(2 * c + 1) * BLOCK:(2 * c + 2) * BLOCK])
            o_ref[:, (kh * Q_PER_KV + 2 * c) * HEAD_DIM:(kh * Q_PER_KV + 2 * c + 2) * HEAD_DIM] = pair.astype(o_ref.dtype)


def _attention(qkv, bias_tables, sinks, layer, bsz, nb):
    t = qkv.shape[0]
    kv_col = ATTN_WIDTH // (2 * KV_WIDTH)
    kv_spec = lambda row_fn: pl.BlockSpec((BLOCK, 2 * KV_WIDTH), lambda n, b: (row_fn(n, b), kv_col))
    return pl.pallas_call(
        functools.partial(_attn_kernel, layer=layer),
        grid=(nb, bsz),
        in_specs=[
            pl.BlockSpec(memory_space=pltpu.SMEM),
            pl.BlockSpec((BLOCK, ATTN_WIDTH), lambda n, b: (n * bsz + b, 0)),
            kv_spec(lambda n, b: b),
            kv_spec(lambda n, b: jnp.maximum(n - 1, 0) * bsz + b),
            kv_spec(lambda n, b: n * bsz + b),
            pl.BlockSpec((None, N_KV_HEADS, Q_PER_KV * BLOCK, 3 * BLOCK),
                         lambda n, b: (jnp.minimum(n, 2), 0, 0, 0)),
        ],
        out_specs=pl.BlockSpec((BLOCK, ATTN_WIDTH), lambda n, b: (n * bsz + b, 0)),
        out_shape=jax.ShapeDtypeStruct((t, ATTN_WIDTH), BF16),
        compiler_params=_cparams("parallel", "parallel"),
        name="swa",
    )(sinks, qkv, qkv, qkv, qkv, bias_tables)


def _t5_bucket(dist):
    n = jnp.maximum(dist, 0)
    max_exact = N_BUCKETS // 2
    nf = jnp.maximum(n, 1).astype(F32)
    large = max_exact + (jnp.log(nf / max_exact) / math.log(MAX_DISTANCE / max_exact)
                         * (N_BUCKETS - max_exact)).astype(jnp.int32)
    large = jnp.minimum(large, N_BUCKETS - 1)
    return jnp.where(n < max_exact, n, large)


def _bias_tables(rel_bias):
    blk = jnp.arange(3, dtype=jnp.int32)[:, None]
    j = jnp.arange(BLOCK, dtype=jnp.int32)[None, :]
    q_pos = blk * BLOCK + j
    k_pos = jnp.concatenate([jnp.broadcast_to(j, (3, BLOCK)), (blk - 1) * BLOCK + j, q_pos], axis=1)
    dist = q_pos[:, :, None] - k_pos[:, None, :]
    kp = k_pos[:, None, :]
    is_meta_seg = (jnp.arange(3 * BLOCK) < BLOCK)[None, None, :]
    valid = (dist >= 0) & jnp.where(is_meta_seg, kp >= META_PAD, (kp >= BLOCK) & (dist < BLOCK))
    onehot = jax.nn.one_hot(_t5_bucket(dist), N_BUCKETS, dtype=F32)
    bias = jnp.einsum("tqkb,bh->thqk", onehot, rel_bias.astype(F32), precision=lax.Precision.HIGHEST)
    bias = jnp.where(valid[:, None], bias * LOG2_E, NEG_INF)
    return bias.reshape(3, N_KV_HEADS, Q_PER_KV * BLOCK, 3 * BLOCK)


def _s5_discretize_kernel(lr_ref, li_ref, ls_ref, bre_ref, bim_ref, ar_ref, ai_ref, bbr_ref, bbi_ref):
    lr, li = lr_ref[...], li_ref[...]
    dt = jnp.exp(ls_ref[...])
    decay = jnp.exp(lr * dt)
    ar, ai = decay * jnp.cos(li * dt), decay * jnp.sin(li * dt)
    den = lr * lr + li * li
    nr, ni = ar - 1.0, ai
    zr = (nr * lr + ni * li) / den
    zi = (ni * lr - nr * li) / den
    bre, bim = bre_ref[...], bim_ref[...]
    ar_ref[...] = ar
    ai_ref[...] = ai
    bbr_ref[...] = zr * bre - zi * bim
    bbi_ref[...] = zr * bim + zi * bre


def _s5_discretize(lam_re, lam_im, log_step, b_re, b_im):
    depth, g, n, p = b_re.shape
    flat = lambda a: a.astype(F32).reshape(depth * g, n * p)
    rep = lambda a: flat(jnp.broadcast_to(a.astype(F32)[..., None], (depth, g, n, p)))
    ls = flat(jnp.broadcast_to(log_step.astype(F32)[:, :, None, None], (depth, g, n, p)))
    shp = jax.ShapeDtypeStruct((depth * g, n * p), F32)
    ar, ai, bbr, bbi = pl.pallas_call(
        _s5_discretize_kernel, out_shape=[shp] * 4, name="s5_discretize",
    )(rep(lam_re), rep(lam_im), ls, flat(b_re), flat(b_im))
    state = lambda a: a.reshape(depth, g, n, p)[..., 0]
    return state(ar), state(ai), bbr.reshape(depth, g, n, p), bbi.reshape(depth, g, n, p)


def _s5_scan_weights(ar, ai, bbr, bbi, c_re, c_im, d_skip):
    depth, g, n, p = bbr.shape
    halves, q_per, gl_per = 2, CHUNKS_PER_SLAB, GROUPS_PER_CHUNK
    gi_per = q_per * gl_per
    slabs = g // (halves * gi_per)
    n_ch = halves * gi_per * p
    gi_of_ch = (jnp.arange(n_ch) // p) % gi_per
    lane_blocks = [(q, ri, gl) for q in range(q_per) for ri in range(2) for gl in range(gl_per)]

    b_ri = jnp.stack([bbr, bbi], axis=1).reshape(depth, 2, halves, slabs, gi_per, n, p)
    b_rows = b_ri.transpose(0, 3, 1, 2, 4, 6, 5).reshape(depth, slabs, 2, n_ch, n)
    wb = jnp.concatenate([jnp.where((gi_of_ch == gl_per * q + gl)[:, None], b_rows[:, :, ri], 0.0)
                          for q, ri, gl in lane_blocks], axis=-1)

    c_ri = jnp.stack([c_re.astype(F32), -c_im.astype(F32)], axis=1).reshape(depth, 2, halves, slabs, gi_per, p, n)
    c_cols = c_ri.transpose(0, 3, 1, 6, 2, 4, 5).reshape(depth, slabs, 2, n, n_ch)
    wc = jnp.concatenate([jnp.where((gi_of_ch == gl_per * q + gl)[None, :], c_cols[:, :, ri], 0.0)
                          for q, ri, gl in lane_blocks], axis=-2)

    def per_seq(a):
        return jnp.repeat(a.astype(F32).reshape(depth, halves, -1), SSM_SEQS // halves, axis=1)

    return wb.astype(BF16), wc.astype(BF16), per_seq(ar), per_seq(ai), per_seq(d_skip)


def _s5_scan_kernel(x_ref, wu_ref, wb_ref, wc_ref, ar_ref, ai_ref, d_ref, y_ref, u_scr, s_ref, y_scr, h_ref,
                    *, n_batch, unroll):
    steps = BLOCK
    rows = steps * SSM_SEQS
    n_slabs = wb_ref.shape[0]
    half_w = n_slabs * LANES

    @pl.when(pl.program_id(0) == 0)
    def _():
        h_ref[...] = jnp.zeros_like(h_ref)

    def seq_rows(b, half):
        return pl.ds(half * n_batch + b, steps, stride=SSM_SEQS)

    def channels(half, j):
        return slice(half * half_w + j * LANES, half * half_w + (j + 1) * LANES)

    u_all = jnp.dot(x_ref[...], wu_ref[...], preferred_element_type=F32)
    for b in range(n_batch):
        for half in range(2):
            for j in range(n_slabs):
                u_scr[j, seq_rows(b, half), :] = u_all[b * steps:(b + 1) * steps, channels(half, j)]

    half0 = (lax.broadcasted_iota(jnp.int32, (rows, 1), 0) & (SSM_SEQS - 1)) < (SSM_SEQS // 2)

    for j in range(n_slabs):
        uj = u_scr[j]
        lhs = jnp.concatenate([jnp.where(half0, uj, 0.0), jnp.where(half0, 0.0, uj)], axis=1).astype(BF16)
        s_ref[:, j * SLAB_STATE_LANES:(j + 1) * SLAB_STATE_LANES] = jnp.dot(
            lhs, wb_ref[j], preferred_element_type=F32)

    n_chunks = n_slabs * CHUNKS_PER_SLAB
    for c0 in range(0, n_chunks, CHUNKS_PER_SLAB):
        chunks = range(c0, c0 + CHUNKS_PER_SLAB)
        a_r = [ar_ref[:, c * LANES:(c + 1) * LANES] for c in chunks]
        a_i = [ai_ref[:, c * LANES:(c + 1) * LANES] for c in chunks]
        init = []
        for c in chunks:
            init += [h_ref[:, 2 * c * LANES:(2 * c + 1) * LANES], h_ref[:, (2 * c + 1) * LANES:(2 * c + 2) * LANES]]

        def body(t, carry, chunks=chunks, a_r=a_r, a_i=a_i):
            r0 = pl.multiple_of(t * SSM_SEQS, SSM_SEQS)
            new = []
            for k, c in enumerate(chunks):
                hr, hi = carry[2 * k], carry[2 * k + 1]
                re_sl = (pl.ds(r0, SSM_SEQS), pl.ds(2 * c * LANES, LANES))
                im_sl = (pl.ds(r0, SSM_SEQS), pl.ds((2 * c + 1) * LANES, LANES))
                nr = a_r[k] * hr - a_i[k] * hi + s_ref[re_sl]
                ni = a_r[k] * hi + a_i[k] * hr + s_ref[im_sl]
                s_ref[re_sl] = nr
                s_ref[im_sl] = ni
                new += [nr, ni]
            return tuple(new)

        final = lax.fori_loop(0, steps, body, tuple(init), unroll=unroll)
        for k, c in enumerate(chunks):
            h_ref[:, 2 * c * LANES:(2 * c + 1) * LANES] = final[2 * k]
            h_ref[:, (2 * c + 1) * LANES:(2 * c + 2) * LANES] = final[2 * k + 1]

    for j in range(n_slabs):
        hs = s_ref[:, j * SLAB_STATE_LANES:(j + 1) * SLAB_STATE_LANES].astype(BF16)
        both = jnp.dot(hs, wc_ref[j], preferred_element_type=F32)
        yj = jnp.where(half0, both[:, :LANES], both[:, LANES:])
        dj = jnp.tile(d_ref[:, j * LANES:(j + 1) * LANES], (steps, 1))
        y_scr[j] = jax.nn.gelu(yj + dj * u_scr[j])

    for b in range(n_batch):
        for half in range(2):
            for j in range(n_slabs):
                y_ref[b * steps:(b + 1) * steps, channels(half, j)] = y_scr[j, seq_rows(b, half), :].astype(y_ref.dtype)


def _s5_scan(hb, w_u, wb, wc, a_r, a_i, d_rows, layer, n_batch):
    t, d = hb.shape
    tile = n_batch * BLOCK
    rows = BLOCK * SSM_SEQS
    n_slabs = wb.shape[1]
    state_lanes = n_slabs * SLAB_STATE_LANES
    ssm_w = w_u.shape[2]
    return pl.pallas_call(
        functools.partial(_s5_scan_kernel, n_batch=n_batch, unroll=True),
        grid=(t // tile,),
        in_specs=[
            pl.BlockSpec((tile, d), lambda i: (i, 0)),
            _layer_spec(w_u.shape, layer), _layer_spec(wb.shape, layer), _layer_spec(wc.shape, layer),
            _layer_spec(a_r.shape, layer), _layer_spec(a_i.shape, layer), _layer_spec(d_rows.shape, layer),
        ],
        out_specs=pl.BlockSpec((tile, ssm_w), lambda i: (i, 0)),
        out_shape=jax.ShapeDtypeStruct((t, ssm_w), BF16),
        scratch_shapes=[pltpu.VMEM((n_slabs, rows, LANES), F32), pltpu.VMEM((rows, state_lanes), F32),
                        pltpu.VMEM((n_slabs, rows, LANES), F32), pltpu.VMEM((SSM_SEQS, state_lanes), F32)],
        compiler_params=_cparams("arbitrary"),
        name="s5_scan",
    )(hb, w_u, wb, wc, a_r, a_i, d_rows)


def _mix_kernel(ya_ref, ys_ref, gate_ref, h_ref, wglu_ref, wau_ref, wsu_ref, wout_ref, g_ref, b_ref,
                hf_ref, hb_ref, sum_scr, *, alpha, n_batch):
    i = pl.program_id(0)
    tm, d = h_ref.shape

    @pl.when(i == 0)
    def _():
        sum_scr[...] = jnp.zeros_like(sum_scr)

    out = _layer_norm(sum_scr[...], g_ref[...], b_ref[...])
    out = jnp.where(_real_row_mask((i - 1) * tm, tm, n_batch), out, 0.0)
    hf_ref[...] = out
    hb_ref[...] = out.astype(BF16)

    y = ys_ref[...]
    glu = jnp.dot(y, wglu_ref[...], preferred_element_type=F32)
    y_ssm = (y.astype(F32) * _sigmoid(glu)).astype(BF16)
    up_a = jnp.dot(ya_ref[...], wau_ref[...], preferred_element_type=F32)
    up_s = jnp.dot(y_ssm, wsu_ref[...], preferred_element_type=F32)
    merged = gate_ref[:, :d].astype(F32) * up_a + gate_ref[:, d:].astype(F32) * up_s
    mixed = jnp.dot(merged.astype(BF16), wout_ref[...], preferred_element_type=F32)
    sum_scr[...] = alpha * h_ref[...] + mixed


MIX_ROWS = 2 * BLOCK


def _mix(y_attn, y_ssm, gates, h, wglu, wau, wsu, wout, g, b, layer, alpha, n_batch):
    t, d = h.shape
    tm = MIX_ROWS
    n_tiles = t // tm
    row_in = lambda w: pl.BlockSpec((tm, w), lambda i: (jnp.minimum(i, n_tiles - 1), 0))
    row_out = lambda: pl.BlockSpec((tm, d), lambda i: (jnp.maximum(i - 1, 0), 0))
    return pl.pallas_call(
        functools.partial(_mix_kernel, alpha=alpha, n_batch=n_batch),
        grid=(n_tiles + 1,),
        in_specs=[row_in(y_attn.shape[1]), row_in(y_ssm.shape[1]), row_in(gates.shape[1]), row_in(d),
                  _layer_spec(wglu.shape, layer), _layer_spec(wau.shape, layer), _layer_spec(wsu.shape, layer),
                  _layer_spec(wout.shape, layer), _layer_spec(g.shape, layer), _layer_spec(b.shape, layer)],
        out_specs=[row_out(), row_out()],
        out_shape=[jax.ShapeDtypeStruct((t, d), F32), jax.ShapeDtypeStruct((t, d), BF16)],
        scratch_shapes=[pltpu.VMEM((tm, d), F32)],
        compiler_params=_cparams("arbitrary"),
        name="mix_ln",
    )(y_attn, y_ssm, gates, h, wglu, wau, wsu, wout, g, b)


def _mlp_kernel(xb_ref, xf_ref, wup_ref, wdn_ref, g_ref, b_ref, *refs, alpha, n_batch, final):
    out_refs, acc_ref = refs[:-1], refs[-1]
    i, j = pl.program_id(0), pl.program_id(1)
    tm, d = xb_ref.shape

    @pl.when(j == 0)
    def _():
        acc_ref[...] = jnp.zeros_like(acc_ref)

    a = jnp.maximum(jnp.dot(xb_ref[...], wup_ref[...], preferred_element_type=F32), 0.0)
    acc_ref[...] += jnp.dot((a * a).astype(BF16), wdn_ref[...], preferred_element_type=F32)

    @pl.when(j == pl.num_programs(1) - 1)
    def _():
        out = _layer_norm(alpha * xf_ref[...] + acc_ref[...], g_ref[...], b_ref[...])
        if final:
            out_refs[0][...] = out.reshape(n_batch, BLOCK, d)
        else:
            out = jnp.where(_real_row_mask(i * tm, tm, n_batch), out, 0.0)
            out_refs[0][...] = out
            out_refs[1][...] = out.astype(BF16)


def _mlp(hb, hf, wup, wdn, g, b, layer, alpha, n_batch, final):
    t, d = hf.shape
    dff = wup.shape[2]
    tm = n_batch * BLOCK
    tf = 1024
    first = 1 if final else 0
    row = lambda: pl.BlockSpec((tm, d), lambda i, j: (i + first, 0))
    if final:
        out_specs = [pl.BlockSpec((n_batch, BLOCK, d), lambda i, j: (0, i, 0))]
        out_shape = [jax.ShapeDtypeStruct((n_batch, t // n_batch - BLOCK, d), F32)]
    else:
        out_specs = [row(), row()]
        out_shape = [jax.ShapeDtypeStruct((t, d), F32), jax.ShapeDtypeStruct((t, d), BF16)]
    return pl.pallas_call(
        functools.partial(_mlp_kernel, alpha=alpha, n_batch=n_batch, final=final),
        grid=(t // tm - first, dff // tf),
        in_specs=[row(), row(),
                  pl.BlockSpec((None, d, tf), lambda i, j: (layer, 0, j)),
                  pl.BlockSpec((None, tf, d), lambda i, j: (layer, j, 0)),
                  _layer_spec(g.shape, layer), _layer_spec(b.shape, layer)],
        out_specs=out_specs,
        out_shape=out_shape,
        scratch_shapes=[pltpu.VMEM((tm, d), F32)],
        compiler_params=_cparams("parallel", "arbitrary"),
        name="mlp_ln",
    )(hb, hf, wup, wdn, g, b)


def kernel(x, meta_tokens, ln_emb_g, ln_emb_b, rel_bias, in_proj, gate_b, attn_sinks, ssm_lambda_re,
           ssm_lambda_im, ssm_log_step, ssm_b_re, ssm_b_im, ssm_c_re, ssm_c_im, ssm_d, ssm_w_glu, w_attn_up,
           w_ssm_up, w_out, ln_mix_g, ln_mix_b, w_mlp_up, w_mlp_down, ln_mlp_g, ln_mlp_b):
    bsz, seq, d = x.shape
    depth = in_proj.shape[0]
    assert seq % BLOCK == 0 and 2 * bsz == SSM_SEQS and meta_tokens.shape[0] == N_META
    nb = seq // BLOCK + 1
    alpha = (2 * depth) ** 0.25
    ssm_w = ssm_w_glu.shape[2]
    row2 = lambda a: a.astype(F32).reshape(1, -1)
    rows = lambda a: a.astype(F32)[:, None, :]
    bf = lambda a: a.astype(BF16)

    meta_pad = jnp.concatenate([jnp.zeros((META_PAD, d), x.dtype), meta_tokens.astype(x.dtype)], axis=0)
    hf, hb = _embed_ln(x, meta_pad, row2(ln_emb_g), row2(ln_emb_b))
    bias_tables = _bias_tables(rel_bias)

    q_scale = jnp.where(jnp.arange(QKV_WIDTH) < ATTN_WIDTH, HEAD_DIM ** -0.5 * LOG2_E, 1.0).astype(in_proj.dtype)
    w_qkv = bf(in_proj[:, :, :QKV_WIDTH] * q_scale)
    w_u = bf(in_proj[:, :, QKV_WIDTH:QKV_WIDTH + ssm_w])
    w_gate = bf(in_proj[:, :, QKV_WIDTH + ssm_w:])
    w_glu, w_au, w_su, w_o = bf(ssm_w_glu), bf(w_attn_up), bf(w_ssm_up), bf(w_out)
    w_up, w_dn = bf(w_mlp_up), bf(w_mlp_down)
    sinks = attn_sinks.astype(F32) * LOG2_E
    ar, ai, bbr, bbi = _s5_discretize(ssm_lambda_re, ssm_lambda_im, ssm_log_step, ssm_b_re, ssm_b_im)
    wb, wc, a_r, a_i, d_rows = _s5_scan_weights(ar, ai, bbr, bbi, ssm_c_re, ssm_c_im, ssm_d)

    for l in range(depth):
        qkv, gates = _in_proj(hb, w_qkv, w_gate, rows(gate_b), l, bsz * BLOCK)
        y_attn = _attention(qkv, bias_tables, sinks, l, bsz, nb)
        y_gelu = _s5_scan(hb, w_u, wb, wc, a_r, a_i, d_rows, l, bsz)
        hf, hb = _mix(y_attn, y_gelu, gates, hf, w_glu, w_au, w_su, w_o, rows(ln_mix_g), rows(ln_mix_b),
                      l, alpha, bsz)
        out = _mlp(hb, hf, w_up, w_dn, rows(ln_mlp_g), rows(ln_mlp_b), l, alpha, bsz, final=(l == depth - 1))
        if l < depth - 1:
            hf, hb = out

    return out[0]
```

```python
import functools
import math

import jax
import jax.numpy as jnp
from jax import lax
from jax.experimental import pallas as pl
from jax.experimental.pallas import tpu as pltpu

F32 = jnp.float32
BF16 = jnp.bfloat16

N_META = 16
BLOCK = 128
META_PAD = BLOCK - N_META
HEAD_DIM = 64
N_Q_HEADS = 16
N_KV_HEADS = 4
Q_PER_KV = N_Q_HEADS // N_KV_HEADS
ATTN_WIDTH = N_Q_HEADS * HEAD_DIM
KV_WIDTH = N_KV_HEADS * HEAD_DIM
QKV_WIDTH = ATTN_WIDTH + 2 * KV_WIDTH
SSM_GROUP = 16
SSM_STATE = 64
N_BUCKETS = 32
MAX_DISTANCE = 128
LN_EPS = 1e-5
NEG_INF = -1e30
LOG2_E = math.log2(math.e)

LANES = 128
SUBLANES = 8
VMEM_LIMIT_BYTES = 56 * 1024 * 1024

SSM_SEQS = SUBLANES
STATES_PER_CHUNK = LANES
GROUPS_PER_CHUNK = STATES_PER_CHUNK // SSM_STATE
CHUNKS_PER_SLAB = 4
SLAB_STATE_LANES = CHUNKS_PER_SLAB * 2 * STATES_PER_CHUNK


def _cparams(*sem):
    return pltpu.CompilerParams(dimension_semantics=sem, vmem_limit_bytes=VMEM_LIMIT_BYTES)


def _resident(shape):
    nd = len(shape)
    return pl.BlockSpec(shape, lambda *_: (0,) * nd, pipeline_mode=pl.Buffered(1))


def _layer_spec(shape, layer):
    nd = len(shape)
    return pl.BlockSpec((None,) + tuple(shape[1:]), lambda *_: (layer,) + (0,) * (nd - 1),
                        pipeline_mode=pl.Buffered(1))


def _layer_norm(x, g, b):
    mu = jnp.mean(x, axis=-1, keepdims=True)
    xc = x - mu
    var = jnp.mean(xc * xc, axis=-1, keepdims=True)
    return xc * lax.rsqrt(var + LN_EPS) * g + b


def _sigmoid(z):
    return 1.0 / (1.0 + jnp.exp(-z))


def _real_row_mask(first_row, n_rows, n_batch):
    r = first_row + lax.broadcasted_iota(jnp.int32, (n_rows, 1), 0)
    is_pad = (r < n_batch * BLOCK) & ((r & (BLOCK - 1)) < META_PAD)
    return jnp.logical_not(is_pad)


def _embed_ln_kernel(meta_ref, x_ref, g_ref, b_ref, hf_ref, hb_ref):
    is_meta = pl.program_id(0) == 0
    rows = lax.broadcasted_iota(jnp.int32, (BLOCK, 1), 0)
    keep = jnp.logical_or(jnp.logical_not(is_meta), rows >= META_PAD)
    for bi in range(x_ref.shape[0]):
        xin = jnp.where(is_meta, meta_ref[...], x_ref[bi])
        y = jnp.where(keep, _layer_norm(xin, g_ref[...], b_ref[...]), 0.0)
        hf_ref[bi * BLOCK:(bi + 1) * BLOCK, :] = y
        hb_ref[bi * BLOCK:(bi + 1) * BLOCK, :] = y.astype(BF16)


def _embed_ln(x, meta_pad, g, b):
    bsz, seq, d = x.shape
    nb = seq // BLOCK + 1
    t = bsz * nb * BLOCK
    tile = pl.BlockSpec((bsz * BLOCK, d), lambda n: (n, 0))
    return pl.pallas_call(
        _embed_ln_kernel,
        grid=(nb,),
        in_specs=[
            _resident(meta_pad.shape),
            pl.BlockSpec((bsz, BLOCK, d), lambda n: (0, jnp.maximum(n - 1, 0), 0)),
            _resident(g.shape), _resident(b.shape),
        ],
        out_specs=[tile, tile],
        out_shape=[jax.ShapeDtypeStruct((t, d), F32), jax.ShapeDtypeStruct((t, d), BF16)],
        compiler_params=_cparams("parallel"),
        name="embed_ln",
    )(meta_pad, x, g, b)


GATE_COL_CHUNK = 1024


def _in_proj_kernel(x_ref, wqkv_ref, wg_ref, gb_ref, qkv_ref, gate_ref):
    x = x_ref[...]
    qkv_ref[...] = jnp.dot(x, wqkv_ref[...], preferred_element_type=F32).astype(qkv_ref.dtype)
    for c in range(0, wg_ref.shape[1], GATE_COL_CHUNK):
        z = jnp.dot(x, wg_ref[:, c:c + GATE_COL_CHUNK], preferred_element_type=F32) + gb_ref[:, c:c + GATE_COL_CHUNK]
        gate_ref[:, c:c + GATE_COL_CHUNK] = _sigmoid(z).astype(gate_ref.dtype)


def _in_proj(hb, w_qkv, w_gate, gate_bias, layer, tm):
    t, d = hb.shape
    row = lambda w: pl.BlockSpec((tm, w), lambda i: (i, 0))
    return pl.pallas_call(
        _in_proj_kernel,
        grid=(t // tm,),
        in_specs=[row(d), _layer_spec(w_qkv.shape, layer), _layer_spec(w_gate.shape, layer),
                  _layer_spec(gate_bias.shape, layer)],
        out_specs=[row(w_qkv.shape[2]), row(w_gate.shape[2])],
        out_shape=[jax.ShapeDtypeStruct((t, w_qkv.shape[2]), BF16), jax.ShapeDtypeStruct((t, w_gate.shape[2]), BF16)],
        compiler_params=_cparams("parallel"),
        name="in_proj",
    )(hb, w_qkv, w_gate, gate_bias)


def _attn_kernel(sink_ref, q_ref, kvm_ref, kvp_ref, kvc_ref, bias_ref, o_ref, *, layer):
    kv_m = kvm_ref[...]
    kv_w = jnp.concatenate([kvp_ref[...], kvc_ref[...]], axis=0)
    rows = Q_PER_KV * BLOCK
    low = lax.broadcasted_iota(jnp.int32, (1, LANES), 1) < HEAD_DIM
    head_of_row = lax.broadcasted_iota(jnp.int32, (rows, 1), 0) // BLOCK
    zero = jnp.zeros((), q_ref.dtype)
    contract_last = (((1,), (1,)), ((), ()))

    def dup_keys(kv, kh):
        k = kv[:, kh * HEAD_DIM:(kh + 1) * HEAD_DIM]
        return jnp.concatenate([k, k], axis=1)

    def dup_values(kv, kh):
        v = kv[:, KV_WIDTH + kh * HEAD_DIM:KV_WIDTH + (kh + 1) * HEAD_DIM]
        return jnp.concatenate([v, v, jnp.ones((kv.shape[0], LANES), kv.dtype)], axis=1)

    def scores(kh):
        cols = [q_ref[:, (kh * Q_PER_KV + 2 * c) * HEAD_DIM:(kh * Q_PER_KV + 2 * c + 2) * HEAD_DIM]
                for c in range(Q_PER_KV // 2)]
        qs = jnp.concatenate([jnp.where(low, cols[g // 2], zero) if g % 2 == 0 else jnp.where(low, zero, cols[g // 2])
                              for g in range(Q_PER_KV)], axis=0)
        s_m = lax.dot_general(qs, dup_keys(kv_m, kh), contract_last, preferred_element_type=F32)
        s_w = lax.dot_general(qs, dup_keys(kv_w, kh), contract_last, preferred_element_type=F32)
        return s_m + bias_ref[kh, :, :BLOCK], s_w + bias_ref[kh, :, BLOCK:]

    def finish(kh, s_m, s_w):
        sink = jnp.zeros((rows, 1), F32)
        for g in range(Q_PER_KV):
            sink = jnp.where(head_of_row == g, sink_ref[layer, kh * Q_PER_KV + g], sink)
        m = jnp.maximum(jnp.maximum(jnp.max(s_m, axis=-1, keepdims=True), jnp.max(s_w, axis=-1, keepdims=True)), sink)
        p_m = jnp.exp2(s_m - m).astype(BF16)
        p_w = jnp.exp2(s_w - m).astype(BF16)
        pv = (jnp.dot(p_m, dup_values(kv_m, kh), preferred_element_type=F32)
              + jnp.dot(p_w, dup_values(kv_w, kh), preferred_element_type=F32))
        denom = pv[:, 2 * HEAD_DIM:] + jnp.exp2(sink - m)
        o = pv[:, :2 * HEAD_DIM] * (1.0 / denom)
        for c in range(Q_PER_KV // 2):
            pair = jnp.where(low, o[2 * c * BLOCK:(2 * c + 1) * BLOCK], o[(2 * c + 1) * BLOCK:(2 * c + 2) * BLOCK])
            o_ref[:, (kh * Q_PER_KV + 2 * c) * HEAD_DIM:(kh * Q_PER_KV + 2 * c + 2) * HEAD_DIM] = pair.astype(o_ref.dtype)

    pending = scores(0)
    for kh in range(N_KV_HEADS):
        ahead = scores(kh + 1) if kh + 1 < N_KV_HEADS else None
        finish(kh, *pending)
        pending = ahead


def _attention(qkv, bias_tables, sinks, layer, bsz, nb):
    t = qkv.shape[0]
    kv_col = ATTN_WIDTH // (2 * KV_WIDTH)
    kv_spec = lambda row_fn: pl.BlockSpec((BLOCK, 2 * KV_WIDTH), lambda n, b: (row_fn(n, b), kv_col))
    return pl.pallas_call(
        functools.partial(_attn_kernel, layer=layer),
        grid=(nb, bsz),
        in_specs=[
            pl.BlockSpec(memory_space=pltpu.SMEM),
            pl.BlockSpec((BLOCK, ATTN_WIDTH), lambda n, b: (n * bsz + b, 0)),
            kv_spec(lambda n, b: b),
            kv_spec(lambda n, b: jnp.maximum(n - 1, 0) * bsz + b),
            kv_spec(lambda n, b: n * bsz + b),
            pl.BlockSpec((None, N_KV_HEADS, Q_PER_KV * BLOCK, 3 * BLOCK),
                         lambda n, b: (jnp.minimum(n, 2), 0, 0, 0)),
        ],
        out_specs=pl.BlockSpec((BLOCK, ATTN_WIDTH), lambda n, b: (n * bsz + b, 0)),
        out_shape=jax.ShapeDtypeStruct((t, ATTN_WIDTH), BF16),
        compiler_params=_cparams("parallel", "parallel"),
        name="swa",
    )(sinks, qkv, qkv, qkv, qkv, bias_tables)


def _t5_bucket(dist):
    n = jnp.maximum(dist, 0)
    max_exact = N_BUCKETS // 2
    nf = jnp.maximum(n, 1).astype(F32)
    large = max_exact + (jnp.log(nf / max_exact) / math.log(MAX_DISTANCE / max_exact)
                         * (N_BUCKETS - max_exact)).astype(jnp.int32)
    large = jnp.minimum(large, N_BUCKETS - 1)
    return jnp.where(n < max_exact, n, large)


def _bias_tables(rel_bias):
    blk = jnp.arange(3, dtype=jnp.int32)[:, None]
    j = jnp.arange(BLOCK, dtype=jnp.int32)[None, :]
    q_pos = blk * BLOCK + j
    k_pos = jnp.concatenate([jnp.broadcast_to(j, (3, BLOCK)), (blk - 1) * BLOCK + j, q_pos], axis=1)
    dist = q_pos[:, :, None] - k_pos[:, None, :]
    kp = k_pos[:, None, :]
    is_meta_seg = (jnp.arange(3 * BLOCK) < BLOCK)[None, None, :]
    valid = (dist >= 0) & jnp.where(is_meta_seg, kp >= META_PAD, (kp >= BLOCK) & (dist < BLOCK))
    onehot = jax.nn.one_hot(_t5_bucket(dist), N_BUCKETS, dtype=F32)
    bias = jnp.einsum("tqkb,bh->thqk", onehot, rel_bias.astype(F32), precision=lax.Precision.HIGHEST)
    bias = jnp.where(valid[:, None], bias * LOG2_E, NEG_INF)
    return bias.reshape(3, N_KV_HEADS, Q_PER_KV * BLOCK, 3 * BLOCK)


def _s5_discretize_kernel(lr_ref, li_ref, ls_ref, bre_ref, bim_ref, ar_ref, ai_ref, bbr_ref, bbi_ref):
    lr, li = lr_ref[...], li_ref[...]
    dt = jnp.exp(ls_ref[...])
    decay = jnp.exp(lr * dt)
    ar, ai = decay * jnp.cos(li * dt), decay * jnp.sin(li * dt)
    den = lr * lr + li * li
    nr, ni = ar - 1.0, ai
    zr = (nr * lr + ni * li) / den
    zi = (ni * lr - nr * li) / den
    bre, bim = bre_ref[...], bim_ref[...]
    ar_ref[...] = ar
    ai_ref[...] = ai
    bbr_ref[...] = zr * bre - zi * bim
    bbi_ref[...] = zr * bim + zi * bre


def _s5_discretize(lam_re, lam_im, log_step, b_re, b_im):
    depth, g, n, p = b_re.shape
    flat = lambda a: a.astype(F32).reshape(depth * g, n * p)
    rep = lambda a: flat(jnp.broadcast_to(a.astype(F32)[..., None], (depth, g, n, p)))
    ls = flat(jnp.broadcast_to(log_step.astype(F32)[:, :, None, None], (depth, g, n, p)))
    shp = jax.ShapeDtypeStruct((depth * g, n * p), F32)
    ar, ai, bbr, bbi = pl.pallas_call(
        _s5_discretize_kernel, out_shape=[shp] * 4, name="s5_discretize",
    )(rep(lam_re), rep(lam_im), ls, flat(b_re), flat(b_im))
    state = lambda a: a.reshape(depth, g, n, p)[..., 0]
    return state(ar), state(ai), bbr.reshape(depth, g, n, p), bbi.reshape(depth, g, n, p)


def _s5_scan_weights(ar, ai, bbr, bbi, c_re, c_im, d_skip):
    depth, g, n, p = bbr.shape
    halves, q_per, gl_per = 2, CHUNKS_PER_SLAB, GROUPS_PER_CHUNK
    gi_per = q_per * gl_per
    slabs = g // (halves * gi_per)
    n_ch = halves * gi_per * p
    gi_of_ch = (jnp.arange(n_ch) // p) % gi_per
    lane_blocks = [(q, ri, gl) for q in range(q_per) for ri in range(2) for gl in range(gl_per)]

    b_ri = jnp.stack([bbr, bbi], axis=1).reshape(depth, 2, halves, slabs, gi_per, n, p)
    b_rows = b_ri.transpose(0, 3, 1, 2, 4, 6, 5).reshape(depth, slabs, 2, n_ch, n)
    wb = jnp.concatenate([jnp.where((gi_of_ch == gl_per * q + gl)[:, None], b_rows[:, :, ri], 0.0)
                          for q, ri, gl in lane_blocks], axis=-1)

    c_ri = jnp.stack([c_re.astype(F32), -c_im.astype(F32)], axis=1).reshape(depth, 2, halves, slabs, gi_per, p, n)
    c_cols = c_ri.transpose(0, 3, 1, 6, 2, 4, 5).reshape(depth, slabs, 2, n, n_ch)
    wc = jnp.concatenate([jnp.where((gi_of_ch == gl_per * q + gl)[None, :], c_cols[:, :, ri], 0.0)
                          for q, ri, gl in lane_blocks], axis=-2)

    def per_seq(a):
        return jnp.repeat(a.astype(F32).reshape(depth, halves, -1), SSM_SEQS // halves, axis=1)

    return wb.astype(BF16), wc.astype(BF16), per_seq(ar), per_seq(ai), per_seq(d_skip)


def _s5_scan_kernel(x_ref, wu_ref, wb_ref, wc_ref, ar_ref, ai_ref, d_ref, y_ref, u_scr, s_ref, y_scr, h_ref,
                    *, n_batch, unroll):
    steps = BLOCK
    rows = steps * SSM_SEQS
    n_slabs = wb_ref.shape[0]
    half_w = n_slabs * LANES

    @pl.when(pl.program_id(0) == 0)
    def _():
        h_ref[...] = jnp.zeros_like(h_ref)

    def seq_rows(b, half):
        return pl.ds(half * n_batch + b, steps, stride=SSM_SEQS)

    def channels(half, j):
        return slice(half * half_w + j * LANES, half * half_w + (j + 1) * LANES)

    u_all = jnp.dot(x_ref[...], wu_ref[...], preferred_element_type=F32)
    for b in range(n_batch):
        for half in range(2):
            for j in range(n_slabs):
                u_scr[j, seq_rows(b, half), :] = u_all[b * steps:(b + 1) * steps, channels(half, j)]

    half0 = (lax.broadcasted_iota(jnp.int32, (rows, 1), 0) & (SSM_SEQS - 1)) < (SSM_SEQS // 2)

    for j in range(n_slabs):
        uj = u_scr[j]
        lhs = jnp.concatenate([jnp.where(half0, uj, 0.0), jnp.where(half0, 0.0, uj)], axis=1).astype(BF16)
        s_ref[:, j * SLAB_STATE_LANES:(j + 1) * SLAB_STATE_LANES] = jnp.dot(
            lhs, wb_ref[j], preferred_element_type=F32)

    n_chunks = n_slabs * CHUNKS_PER_SLAB
    for c0 in range(0, n_chunks, CHUNKS_PER_SLAB):
        chunks = range(c0, c0 + CHUNKS_PER_SLAB)
        a_r = [ar_ref[:, c * LANES:(c + 1) * LANES] for c in chunks]
        a_i = [ai_ref[:, c * LANES:(c + 1) * LANES] for c in chunks]
        init = []
        for c in chunks:
            init += [h_ref[:, 2 * c * LANES:(2 * c + 1) * LANES], h_ref[:, (2 * c + 1) * LANES:(2 * c + 2) * LANES]]

        def body(t, carry, chunks=chunks, a_r=a_r, a_i=a_i):
            r0 = pl.multiple_of(t * SSM_SEQS, SSM_SEQS)
            new = []
            for k, c in enumerate(chunks):
                hr, hi = carry[2 * k], carry[2 * k + 1]
                re_sl = (pl.ds(r0, SSM_SEQS), pl.ds(2 * c * LANES, LANES))
                im_sl = (pl.ds(r0, SSM_SEQS), pl.ds((2 * c + 1) * LANES, LANES))
                nr = a_r[k] * hr - a_i[k] * hi + s_ref[re_sl]
                ni = a_r[k] * hi + a_i[k] * hr + s_ref[im_sl]
                s_ref[re_sl] = nr
                s_ref[im_sl] = ni
                new += [nr, ni]
            return tuple(new)

        final = lax.fori_loop(0, steps, body, tuple(init), unroll=unroll)
        for k, c in enumerate(chunks):
            h_ref[:, 2 * c * LANES:(2 * c + 1) * LANES] = final[2 * k]
            h_ref[:, (2 * c + 1) * LANES:(2 * c + 2) * LANES] = final[2 * k + 1]

    for j in range(n_slabs):
        hs = s_ref[:, j * SLAB_STATE_LANES:(j + 1) * SLAB_STATE_LANES].astype(BF16)
        both = jnp.dot(hs, wc_ref[j], preferred_element_type=F32)
        yj = jnp.where(half0, both[:, :LANES], both[:, LANES:])
        dj = jnp.tile(d_ref[:, j * LANES:(j + 1) * LANES], (steps, 1))
        y_scr[j] = jax.nn.gelu(yj + dj * u_scr[j])

    for b in range(n_batch):
        for half in range(2):
            for j in range(n_slabs):
                y_ref[b * steps:(b + 1) * steps, channels(half, j)] = y_scr[j, seq_rows(b, half), :].astype(y_ref.dtype)


def _s5_scan(hb, w_u, wb, wc, a_r, a_i, d_rows, layer, n_batch):
    t, d = hb.shape
    tile = n_batch * BLOCK
    rows = BLOCK * SSM_SEQS
    n_slabs = wb.shape[1]
    state_lanes = n_slabs * SLAB_STATE_LANES
    ssm_w = w_u.shape[2]
    return pl.pallas_call(
        functools.partial(_s5_scan_kernel, n_batch=n_batch, unroll=True),
        grid=(t // tile,),
        in_specs=[
            pl.BlockSpec((tile, d), lambda i: (i, 0)),
            _layer_spec(w_u.shape, layer), _layer_spec(wb.shape, layer), _layer_spec(wc.shape, layer),
            _layer_spec(a_r.shape, layer), _layer_spec(a_i.shape, layer), _layer_spec(d_rows.shape, layer),
        ],
        out_specs=pl.BlockSpec((tile, ssm_w), lambda i: (i, 0)),
        out_shape=jax.ShapeDtypeStruct((t, ssm_w), BF16),
        scratch_shapes=[pltpu.VMEM((n_slabs, rows, LANES), F32), pltpu.VMEM((rows, state_lanes), F32),
                        pltpu.VMEM((n_slabs, rows, LANES), F32), pltpu.VMEM((SSM_SEQS, state_lanes), F32)],
        compiler_params=_cparams("arbitrary"),
        name="s5_scan",
    )(hb, w_u, wb, wc, a_r, a_i, d_rows)


def _mix_kernel(ya_ref, ys_ref, gate_ref, h_ref, wglu_ref, wau_ref, wsu_ref, wout_ref, g_ref, b_ref,
                hf_ref, hb_ref, *, alpha, n_batch):
    tm, d = h_ref.shape
    y = ys_ref[...]
    glu = jnp.dot(y, wglu_ref[...], preferred_element_type=F32)
    y_ssm = (y.astype(F32) * _sigmoid(glu)).astype(BF16)
    up_a = jnp.dot(ya_ref[...], wau_ref[...], preferred_element_type=F32)
    up_s = jnp.dot(y_ssm, wsu_ref[...], preferred_element_type=F32)
    merged = gate_ref[:, :d].astype(F32) * up_a + gate_ref[:, d:].astype(F32) * up_s
    mixed = jnp.dot(merged.astype(BF16), wout_ref[...], preferred_element_type=F32)
    out = _layer_norm(alpha * h_ref[...] + mixed, g_ref[...], b_ref[...])
    out = jnp.where(_real_row_mask(pl.program_id(0) * tm, tm, n_batch), out, 0.0)
    hf_ref[...] = out
    hb_ref[...] = out.astype(BF16)


MIX_ROWS = 2 * BLOCK


def _mix(y_attn, y_ssm, gates, h, wglu, wau, wsu, wout, g, b, layer, alpha, n_batch):
    t, d = h.shape
    tm = MIX_ROWS
    row = lambda w: pl.BlockSpec((tm, w), lambda i: (i, 0))
    return pl.pallas_call(
        functools.partial(_mix_kernel, alpha=alpha, n_batch=n_batch),
        grid=(t // tm,),
        in_specs=[row(y_attn.shape[1]), row(y_ssm.shape[1]), row(gates.shape[1]), row(d),
                  _layer_spec(wglu.shape, layer), _layer_spec(wau.shape, layer), _layer_spec(wsu.shape, layer),
                  _layer_spec(wout.shape, layer), _layer_spec(g.shape, layer), _layer_spec(b.shape, layer)],
        out_specs=[row(d), row(d)],
        out_shape=[jax.ShapeDtypeStruct((t, d), F32), jax.ShapeDtypeStruct((t, d), BF16)],
        compiler_params=_cparams("parallel"),
        name="mix_ln",
    )(y_attn, y_ssm, gates, h, wglu, wau, wsu, wout, g, b)


def _mlp_kernel(xb_ref, xf_ref, wup_ref, wdn_ref, g_ref, b_ref, *refs, alpha, n_batch, final):
    out_refs, acc_ref = refs[:-1], refs[-1]
    i, j = pl.program_id(0), pl.program_id(1)
    tm, d = xb_ref.shape

    @pl.when(j == 0)
    def _():
        acc_ref[...] = jnp.zeros_like(acc_ref)

    a = jnp.maximum(jnp.dot(xb_ref[...], wup_ref[...], preferred_element_type=F32), 0.0)
    acc_ref[...] += jnp.dot((a * a).astype(BF16), wdn_ref[...], preferred_element_type=F32)

    @pl.when(j == pl.num_programs(1) - 1)
    def _():
        out = _layer_norm(alpha * xf_ref[...] + acc_ref[...], g_ref[...], b_ref[...])
        if final:
            out_refs[0][...] = out.reshape(n_batch, BLOCK, d)
        else:
            out = jnp.where(_real_row_mask(i * tm, tm, n_batch), out, 0.0)
            out_refs[0][...] = out
            out_refs[1][...] = out.astype(BF16)


def _mlp(hb, hf, wup, wdn, g, b, layer, alpha, n_batch, final):
    t, d = hf.shape
    dff = wup.shape[2]
    tm = n_batch * BLOCK
    tf = 1024
    first = 1 if final else 0
    row = lambda: pl.BlockSpec((tm, d), lambda i, j: (i + first, 0))
    if final:
        out_specs = [pl.BlockSpec((n_batch, BLOCK, d), lambda i, j: (0, i, 0))]
        out_shape = [jax.ShapeDtypeStruct((n_batch, t // n_batch - BLOCK, d), F32)]
    else:
        out_specs = [row(), row()]
        out_shape = [jax.ShapeDtypeStruct((t, d), F32), jax.ShapeDtypeStruct((t, d), BF16)]
    return pl.pallas_call(
        functools.partial(_mlp_kernel, alpha=alpha, n_batch=n_batch, final=final),
        grid=(t // tm - first, dff // tf),
        in_specs=[row(), row(),
                  pl.BlockSpec((None, d, tf), lambda i, j: (layer, 0, j)),
                  pl.BlockSpec((None, tf, d), lambda i, j: (layer, j, 0)),
                  _layer_spec(g.shape, layer), _layer_spec(b.shape, layer)],
        out_specs=out_specs,
        out_shape=out_shape,
        scratch_shapes=[pltpu.VMEM((tm, d), F32)],
        compiler_params=_cparams("parallel", "arbitrary"),
        name="mlp_ln",
    )(hb, hf, wup, wdn, g, b)


def kernel(x, meta_tokens, ln_emb_g, ln_emb_b, rel_bias, in_proj, gate_b, attn_sinks, ssm_lambda_re,
           ssm_lambda_im, ssm_log_step, ssm_b_re, ssm_b_im, ssm_c_re, ssm_c_im, ssm_d, ssm_w_glu, w_attn_up,
           w_ssm_up, w_out, ln_mix_g, ln_mix_b, w_mlp_up, w_mlp_down, ln_mlp_g, ln_mlp_b):
    bsz, seq, d = x.shape
    depth = in_proj.shape[0]
    assert seq % BLOCK == 0 and 2 * bsz == SSM_SEQS and meta_tokens.shape[0] == N_META
    nb = seq // BLOCK + 1
    alpha = (2 * depth) ** 0.25
    ssm_w = ssm_w_glu.shape[2]
    row2 = lambda a: a.astype(F32).reshape(1, -1)
    rows = lambda a: a.astype(F32)[:, None, :]
    bf = lambda a: a.astype(BF16)

    meta_pad = jnp.concatenate([jnp.zeros((META_PAD, d), x.dtype), meta_tokens.astype(x.dtype)], axis=0)
    hf, hb = _embed_ln(x, meta_pad, row2(ln_emb_g), row2(ln_emb_b))
    bias_tables = _bias_tables(rel_bias)

    q_scale = jnp.where(jnp.arange(QKV_WIDTH) < ATTN_WIDTH, HEAD_DIM ** -0.5 * LOG2_E, 1.0).astype(in_proj.dtype)
    w_qkv = bf(in_proj[:, :, :QKV_WIDTH] * q_scale)
    w_u = bf(in_proj[:, :, QKV_WIDTH:QKV_WIDTH + ssm_w])
    w_gate = bf(in_proj[:, :, QKV_WIDTH + ssm_w:])
    w_glu, w_au, w_su, w_o = bf(ssm_w_glu), bf(w_attn_up), bf(w_ssm_up), bf(w_out)
    w_up, w_dn = bf(w_mlp_up), bf(w_mlp_down)
    sinks = attn_sinks.astype(F32) * LOG2_E
    ar, ai, bbr, bbi = _s5_discretize(ssm_lambda_re, ssm_lambda_im, ssm_log_step, ssm_b_re, ssm_b_im)
    wb, wc, a_r, a_i, d_rows = _s5_scan_weights(ar, ai, bbr, bbi, ssm_c_re, ssm_c_im, ssm_d)

    for l in range(depth):
        qkv, gates = _in_proj(hb, w_qkv, w_gate, rows(gate_b), l, bsz * BLOCK)
        y_attn = _attention(qkv, bias_tables, sinks, l, bsz, nb)
        y_gelu = _s5_scan(hb, w_u, wb, wc, a_r, a_i, d_rows, l, bsz)
        hf, hb = _mix(y_attn, y_gelu, gates, hf, w_glu, w_au, w_su, w_o, rows(ln_mix_g), rows(ln_mix_b),
                      l, alpha, bsz)
        out = _mlp(hb, hf, w_up, w_dn, rows(ln_mlp_g), rows(ln_mlp_b), l, alpha, bsz, final=(l == depth - 1))
        if l < depth - 1:
            hf, hb = out

    return out[0]
```

```python
import functools
import math

import jax
import jax.numpy as jnp
from jax import lax
from jax.experimental import pallas as pl
from jax.experimental.pallas import tpu as pltpu

F32 = jnp.float32
BF16 = jnp.bfloat16

N_META = 16
BLOCK = 128
META_PAD = BLOCK - N_META
HEAD_DIM = 64
N_Q_HEADS = 16
N_KV_HEADS = 4
Q_PER_KV = N_Q_HEADS // N_KV_HEADS
ATTN_WIDTH = N_Q_HEADS * HEAD_DIM
KV_WIDTH = N_KV_HEADS * HEAD_DIM
QKV_WIDTH = ATTN_WIDTH + 2 * KV_WIDTH
SSM_GROUP = 16
SSM_STATE = 64
N_BUCKETS = 32
MAX_DISTANCE = 128
LN_EPS = 1e-5
NEG_INF = -1e30
LOG2_E = math.log2(math.e)

LANES = 128
SUBLANES = 8
VMEM_LIMIT_BYTES = 56 * 1024 * 1024

SSM_SEQS = SUBLANES
STATES_PER_CHUNK = LANES
GROUPS_PER_CHUNK = STATES_PER_CHUNK // SSM_STATE
CHUNKS_PER_SLAB = 4
SLAB_STATE_LANES = CHUNKS_PER_SLAB * 2 * STATES_PER_CHUNK


def _cparams(*sem):
    return pltpu.CompilerParams(dimension_semantics=sem, vmem_limit_bytes=VMEM_LIMIT_BYTES)


def _resident(shape):
    nd = len(shape)
    return pl.BlockSpec(shape, lambda *_: (0,) * nd, pipeline_mode=pl.Buffered(1))


def _layer_spec(shape, layer):
    nd = len(shape)
    return pl.BlockSpec((None,) + tuple(shape[1:]), lambda *_: (layer,) + (0,) * (nd - 1),
                        pipeline_mode=pl.Buffered(1))


def _layer_norm(x, g, b):
    mu = jnp.mean(x, axis=-1, keepdims=True)
    xc = x - mu
    var = jnp.mean(xc * xc, axis=-1, keepdims=True)
    return xc * lax.rsqrt(var + LN_EPS) * g + b


def _sigmoid(z):
    return 1.0 / (1.0 + jnp.exp(-z))


def _real_row_mask(first_row, n_rows, n_batch):
    r = first_row + lax.broadcasted_iota(jnp.int32, (n_rows, 1), 0)
    is_pad = (r < n_batch * BLOCK) & ((r & (BLOCK - 1)) < META_PAD)
    return jnp.logical_not(is_pad)


def _embed_ln_kernel(meta_ref, x_ref, g_ref, b_ref, hf_ref, hb_ref):
    is_meta = pl.program_id(0) == 0
    rows = lax.broadcasted_iota(jnp.int32, (BLOCK, 1), 0)
    keep = jnp.logical_or(jnp.logical_not(is_meta), rows >= META_PAD)
    for bi in range(x_ref.shape[0]):
        xin = jnp.where(is_meta, meta_ref[...], x_ref[bi])
        y = jnp.where(keep, _layer_norm(xin, g_ref[...], b_ref[...]), 0.0)
        hf_ref[bi * BLOCK:(bi + 1) * BLOCK, :] = y
        hb_ref[bi * BLOCK:(bi + 1) * BLOCK, :] = y.astype(BF16)


def _embed_ln(x, meta_pad, g, b):
    bsz, seq, d = x.shape
    nb = seq // BLOCK + 1
    t = bsz * nb * BLOCK
    tile = pl.BlockSpec((bsz * BLOCK, d), lambda n: (n, 0))
    return pl.pallas_call(
        _embed_ln_kernel,
        grid=(nb,),
        in_specs=[
            _resident(meta_pad.shape),
            pl.BlockSpec((bsz, BLOCK, d), lambda n: (0, jnp.maximum(n - 1, 0), 0)),
            _resident(g.shape), _resident(b.shape),
        ],
        out_specs=[tile, tile],
        out_shape=[jax.ShapeDtypeStruct((t, d), F32), jax.ShapeDtypeStruct((t, d), BF16)],
        compiler_params=_cparams("parallel"),
        name="embed_ln",
    )(meta_pad, x, g, b)


GATE_COL_CHUNK = 1024


def _in_proj_kernel(x_ref, wqkv_ref, wg_ref, gb_ref, qkv_ref, gate_ref):
    x = x_ref[...]
    qkv_ref[...] = jnp.dot(x, wqkv_ref[...], preferred_element_type=F32).astype(qkv_ref.dtype)
    for c in range(0, wg_ref.shape[1], GATE_COL_CHUNK):
        z = jnp.dot(x, wg_ref[:, c:c + GATE_COL_CHUNK], preferred_element_type=F32) + gb_ref[:, c:c + GATE_COL_CHUNK]
        gate_ref[:, c:c + GATE_COL_CHUNK] = _sigmoid(z).astype(gate_ref.dtype)


def _in_proj(hb, w_qkv, w_gate, gate_bias, layer, tm):
    t, d = hb.shape
    row = lambda w: pl.BlockSpec((tm, w), lambda i: (i, 0))
    return pl.pallas_call(
        _in_proj_kernel,
        grid=(t // tm,),
        in_specs=[row(d), _layer_spec(w_qkv.shape, layer), _layer_spec(w_gate.shape, layer),
                  _layer_spec(gate_bias.shape, layer)],
        out_specs=[row(w_qkv.shape[2]), row(w_gate.shape[2])],
        out_shape=[jax.ShapeDtypeStruct((t, w_qkv.shape[2]), BF16), jax.ShapeDtypeStruct((t, w_gate.shape[2]), BF16)],
        compiler_params=_cparams("parallel"),
        name="in_proj",
    )(hb, w_qkv, w_gate, gate_bias)


def _attn_kernel(sink_ref, q_ref, kvm_ref, kvp_ref, kvc_ref, bias_ref, o_ref, *, layer):
    kv_m = kvm_ref[...]
    kv_w = jnp.concatenate([kvp_ref[...], kvc_ref[...]], axis=0)
    rows = Q_PER_KV * BLOCK
    low = lax.broadcasted_iota(jnp.int32, (1, LANES), 1) < HEAD_DIM
    head_of_row = lax.broadcasted_iota(jnp.int32, (rows, 1), 0) // BLOCK
    zero = jnp.zeros((), q_ref.dtype)
    contract_last = (((1,), (1,)), ((), ()))

    def dup_keys(kv, kh):
        k = kv[:, kh * HEAD_DIM:(kh + 1) * HEAD_DIM]
        return jnp.concatenate([k, k], axis=1)

    def dup_values(kv, kh):
        v = kv[:, KV_WIDTH + kh * HEAD_DIM:KV_WIDTH + (kh + 1) * HEAD_DIM]
        return jnp.concatenate([v, v, jnp.ones((kv.shape[0], LANES), kv.dtype)], axis=1)

    def scores(kh):
        cols = [q_ref[:, (kh * Q_PER_KV + 2 * c) * HEAD_DIM:(kh * Q_PER_KV + 2 * c + 2) * HEAD_DIM]
                for c in range(Q_PER_KV // 2)]
        qs = jnp.concatenate([jnp.where(low, cols[g // 2], zero) if g % 2 == 0 else jnp.where(low, zero, cols[g // 2])
                              for g in range(Q_PER_KV)], axis=0)
        s_m = lax.dot_general(qs, dup_keys(kv_m, kh), contract_last, preferred_element_type=F32)
        s_w = lax.dot_general(qs, dup_keys(kv_w, kh), contract_last, preferred_element_type=F32)
        return s_m + bias_ref[kh, :, :BLOCK], s_w + bias_ref[kh, :, BLOCK:]

    def finish(kh, s_m, s_w):
        sink = jnp.zeros((rows, 1), F32)
        for g in range(Q_PER_KV):
            sink = jnp.where(head_of_row == g, sink_ref[layer, kh * Q_PER_KV + g], sink)
        m = jnp.maximum(jnp.maximum(jnp.max(s_m, axis=-1, keepdims=True), jnp.max(s_w, axis=-1, keepdims=True)), sink)
        p_m = jnp.exp2(s_m - m).astype(BF16)
        p_w = jnp.exp2(s_w - m).astype(BF16)
        pv = (jnp.dot(p_m, dup_values(kv_m, kh), preferred_element_type=F32)
              + jnp.dot(p_w, dup_values(kv_w, kh), preferred_element_type=F32))
        denom = pv[:, 2 * HEAD_DIM:] + jnp.exp2(sink - m)
        o = pv[:, :2 * HEAD_DIM] * (1.0 / denom)
        for c in range(Q_PER_KV // 2):
            pair = jnp.where(low, o[2 * c * BLOCK:(2 * c + 1) * BLOCK], o[(2 * c + 1) * BLOCK:(2 * c + 2) * BLOCK])
            o_ref[:, (kh * Q_PER_KV + 2 * c) * HEAD_DIM:(kh * Q_PER_KV + 2 * c + 2) * HEAD_DIM] = pair.astype(o_ref.dtype)

    pending = scores(0)
    for kh in range(N_KV_HEADS):
        ahead = scores(kh + 1) if kh + 1 < N_KV_HEADS else None
        finish(kh, *pending)
        pending = ahead


def _attention(qkv, bias_tables, sinks, layer, bsz, nb):
    t = qkv.shape[0]
    kv_col = ATTN_WIDTH // (2 * KV_WIDTH)
    kv_spec = lambda row_fn: pl.BlockSpec((BLOCK, 2 * KV_WIDTH), lambda n, b: (row_fn(n, b), kv_col))
    return pl.pallas_call(
        functools.partial(_attn_kernel, layer=layer),
        grid=(nb, bsz),
        in_specs=[
            pl.BlockSpec(memory_space=pltpu.SMEM),
            pl.BlockSpec((BLOCK, ATTN_WIDTH), lambda n, b: (n * bsz + b, 0)),
            kv_spec(lambda n, b: b),
            kv_spec(lambda n, b: jnp.maximum(n - 1, 0) * bsz + b),
            kv_spec(lambda n, b: n * bsz + b),
            pl.BlockSpec((None, N_KV_HEADS, Q_PER_KV * BLOCK, 3 * BLOCK),
                         lambda n, b: (jnp.minimum(n, 2), 0, 0, 0)),
        ],
        out_specs=pl.BlockSpec((BLOCK, ATTN_WIDTH), lambda n, b: (n * bsz + b, 0)),
        out_shape=jax.ShapeDtypeStruct((t, ATTN_WIDTH), BF16),
        compiler_params=_cparams("parallel", "parallel"),
        name="swa",
    )(sinks, qkv, qkv, qkv, qkv, bias_tables)


def _t5_bucket(dist):
    n = jnp.maximum(dist, 0)
    max_exact = N_BUCKETS // 2
    nf = jnp.maximum(n, 1).astype(F32)
    large = max_exact + (jnp.log(nf / max_exact) / math.log(MAX_DISTANCE / max_exact)
                         * (N_BUCKETS - max_exact)).astype(jnp.int32)
    large = jnp.minimum(large, N_BUCKETS - 1)
    return jnp.where(n < max_exact, n, large)


def _bias_tables(rel_bias):
    blk = jnp.arange(3, dtype=jnp.int32)[:, None]
    j = jnp.arange(BLOCK, dtype=jnp.int32)[None, :]
    q_pos = blk * BLOCK + j
    k_pos = jnp.concatenate([jnp.broadcast_to(j, (3, BLOCK)), (blk - 1) * BLOCK + j, q_pos], axis=1)
    dist = q_pos[:, :, None] - k_pos[:, None, :]
    kp = k_pos[:, None, :]
    is_meta_seg = (jnp.arange(3 * BLOCK) < BLOCK)[None, None, :]
    valid = (dist >= 0) & jnp.where(is_meta_seg, kp >= META_PAD, (kp >= BLOCK) & (dist < BLOCK))
    onehot = jax.nn.one_hot(_t5_bucket(dist), N_BUCKETS, dtype=F32)
    bias = jnp.einsum("tqkb,bh->thqk", onehot, rel_bias.astype(F32), precision=lax.Precision.HIGHEST)
    bias = jnp.where(valid[:, None], bias * LOG2_E, NEG_INF)
    return bias.reshape(3, N_KV_HEADS, Q_PER_KV * BLOCK, 3 * BLOCK)


def _s5_discretize_kernel(lr_ref, li_ref, ls_ref, bre_ref, bim_ref, ar_ref, ai_ref, bbr_ref, bbi_ref):
    lr, li = lr_ref[...], li_ref[...]
    dt = jnp.exp(ls_ref[...])
    decay = jnp.exp(lr * dt)
    ar, ai = decay * jnp.cos(li * dt), decay * jnp.sin(li * dt)
    den = lr * lr + li * li
    nr, ni = ar - 1.0, ai
    zr = (nr * lr + ni * li) / den
    zi = (ni * lr - nr * li) / den
    bre, bim = bre_ref[...], bim_ref[...]
    ar_ref[...] = ar
    ai_ref[...] = ai
    bbr_ref[...] = zr * bre - zi * bim
    bbi_ref[...] = zr * bim + zi * bre


def _s5_discretize(lam_re, lam_im, log_step, b_re, b_im):
    depth, g, n, p = b_re.shape
    flat = lambda a: a.astype(F32).reshape(depth * g, n * p)
    rep = lambda a: flat(jnp.broadcast_to(a.astype(F32)[..., None], (depth, g, n, p)))
    ls = flat(jnp.broadcast_to(log_step.astype(F32)[:, :, None, None], (depth, g, n, p)))
    shp = jax.ShapeDtypeStruct((depth * g, n * p), F32)
    ar, ai, bbr, bbi = pl.pallas_call(
        _s5_discretize_kernel, out_shape=[shp] * 4, name="s5_discretize",
    )(rep(lam_re), rep(lam_im), ls, flat(b_re), flat(b_im))
    state = lambda a: a.reshape(depth, g, n, p)[..., 0]
    return state(ar), state(ai), bbr.reshape(depth, g, n, p), bbi.reshape(depth, g, n, p)


def _s5_scan_weights(ar, ai, bbr, bbi, c_re, c_im, d_skip):
    depth, g, n, p = bbr.shape
    halves, q_per, gl_per = 2, CHUNKS_PER_SLAB, GROUPS_PER_CHUNK
    gi_per = q_per * gl_per
    slabs = g // (halves * gi_per)
    n_ch = halves * gi_per * p
    gi_of_ch = (jnp.arange(n_ch) // p) % gi_per
    lane_blocks = [(q, ri, gl) for q in range(q_per) for ri in range(2) for gl in range(gl_per)]

    b_ri = jnp.stack([bbr, bbi], axis=1).reshape(depth, 2, halves, slabs, gi_per, n, p)
    b_rows = b_ri.transpose(0, 3, 1, 2, 4, 6, 5).reshape(depth, slabs, 2, n_ch, n)
    wb = jnp.concatenate([jnp.where((gi_of_ch == gl_per * q + gl)[:, None], b_rows[:, :, ri], 0.0)
                          for q, ri, gl in lane_blocks], axis=-1)

    c_ri = jnp.stack([c_re.astype(F32), -c_im.astype(F32)], axis=1).reshape(depth, 2, halves, slabs, gi_per, p, n)
    c_cols = c_ri.transpose(0, 3, 1, 6, 2, 4, 5).reshape(depth, slabs, 2, n, n_ch)
    wc = jnp.concatenate([jnp.where((gi_of_ch == gl_per * q + gl)[None, :], c_cols[:, :, ri], 0.0)
                          for q, ri, gl in lane_blocks], axis=-2)

    def per_seq(a):
        return jnp.repeat(a.astype(F32).reshape(depth, halves, -1), SSM_SEQS // halves, axis=1)

    return wb.astype(BF16), wc.astype(BF16), per_seq(ar), per_seq(ai), per_seq(d_skip)


def _s5_scan_kernel(x_ref, wu_ref, wb_ref, wc_ref, ar_ref, ai_ref, d_ref, y_ref, u_scr, s_ref, y_scr, h_ref,
                    *, n_batch, unroll):
    steps = BLOCK
    rows = steps * SSM_SEQS
    n_slabs = wb_ref.shape[0]
    half_w = n_slabs * LANES

    @pl.when(pl.program_id(0) == 0)
    def _():
        h_ref[...] = jnp.zeros_like(h_ref)

    def seq_rows(b, half):
        return pl.ds(half * n_batch + b, steps, stride=SSM_SEQS)

    def channels(half, j):
        return slice(half * half_w + j * LANES, half * half_w + (j + 1) * LANES)

    u_all = jnp.dot(x_ref[...], wu_ref[...], preferred_element_type=F32)
    for b in range(n_batch):
        for half in range(2):
            for j in range(n_slabs):
                u_scr[j, seq_rows(b, half), :] = u_all[b * steps:(b + 1) * steps, channels(half, j)]

    half0 = (lax.broadcasted_iota(jnp.int32, (rows, 1), 0) & (SSM_SEQS - 1)) < (SSM_SEQS // 2)

    for j in range(n_slabs):
        uj = u_scr[j]
        lhs = jnp.concatenate([jnp.where(half0, uj, 0.0), jnp.where(half0, 0.0, uj)], axis=1).astype(BF16)
        s_ref[:, j * SLAB_STATE_LANES:(j + 1) * SLAB_STATE_LANES] = jnp.dot(
            lhs, wb_ref[j], preferred_element_type=F32)

    n_chunks = n_slabs * CHUNKS_PER_SLAB
    for c0 in range(0, n_chunks, CHUNKS_PER_SLAB):
        chunks = range(c0, c0 + CHUNKS_PER_SLAB)
        a_r = [ar_ref[:, c * LANES:(c + 1) * LANES] for c in chunks]
        a_i = [ai_ref[:, c * LANES:(c + 1) * LANES] for c in chunks]
        init = []
        for c in chunks:
            init += [h_ref[:, 2 * c * LANES:(2 * c + 1) * LANES], h_ref[:, (2 * c + 1) * LANES:(2 * c + 2) * LANES]]

        def body(t, carry, chunks=chunks, a_r=a_r, a_i=a_i):
            r0 = pl.multiple_of(t * SSM_SEQS, SSM_SEQS)
            new = []
            for k, c in enumerate(chunks):
                hr, hi = carry[2 * k], carry[2 * k + 1]
                re_sl = (pl.ds(r0, SSM_SEQS), pl.ds(2 * c * LANES, LANES))
                im_sl = (pl.ds(r0, SSM_SEQS), pl.ds((2 * c + 1) * LANES, LANES))
                nr = a_r[k] * hr - a_i[k] * hi + s_ref[re_sl]
                ni = a_r[k] * hi + a_i[k] * hr + s_ref[im_sl]
                s_ref[re_sl] = nr
                s_ref[im_sl] = ni
                new += [nr, ni]
            return tuple(new)

        final = lax.fori_loop(0, steps, body, tuple(init), unroll=unroll)
        for k, c in enumerate(chunks):
            h_ref[:, 2 * c * LANES:(2 * c + 1) * LANES] = final[2 * k]
            h_ref[:, (2 * c + 1) * LANES:(2 * c + 2) * LANES] = final[2 * k + 1]

    for j in range(n_slabs):
        hs = s_ref[:, j * SLAB_STATE_LANES:(j + 1) * SLAB_STATE_LANES].astype(BF16)
        both = jnp.dot(hs, wc_ref[j], preferred_element_type=F32)
        yj = jnp.where(half0, both[:, :LANES], both[:, LANES:])
        dj = jnp.tile(d_ref[:, j * LANES:(j + 1) * LANES], (steps, 1))
        y_scr[j] = jax.nn.gelu(yj + dj * u_scr[j])

    for b in range(n_batch):
        for half in range(2):
            for j in range(n_slabs):
                y_ref[b * steps:(b + 1) * steps, channels(half, j)] = y_scr[j, seq_rows(b, half), :].astype(y_ref.dtype)


def _s5_scan(hb, w_u, wb, wc, a_r, a_i, d_rows, layer, n_batch):
    t, d = hb.shape
    tile = n_batch * BLOCK
    rows = BLOCK * SSM_SEQS
    n_slabs = wb.shape[1]
    state_lanes = n_slabs * SLAB_STATE_LANES
    ssm_w = w_u.shape[2]
    return pl.pallas_call(
        functools.partial(_s5_scan_kernel, n_batch=n_batch, unroll=True),
        grid=(t // tile,),
        in_specs=[
            pl.BlockSpec((tile, d), lambda i: (i, 0)),
            _layer_spec(w_u.shape, layer), _layer_spec(wb.shape, layer), _layer_spec(wc.shape, layer),
            _layer_spec(a_r.shape, layer), _layer_spec(a_i.shape, layer), _layer_spec(d_rows.shape, layer),
        ],
        out_specs=pl.BlockSpec((tile, ssm_w), lambda i: (i, 0)),
        out_shape=jax.ShapeDtypeStruct((t, ssm_w), BF16),
        scratch_shapes=[pltpu.VMEM((n_slabs, rows, LANES), F32), pltpu.VMEM((rows, state_lanes), F32),
                        pltpu.VMEM((n_slabs, rows, LANES), F32), pltpu.VMEM((SSM_SEQS, state_lanes), F32)],
        compiler_params=_cparams("arbitrary"),
        name="s5_scan",
    )(hb, w_u, wb, wc, a_r, a_i, d_rows)


def _mix_kernel(ya_ref, ys_ref, gate_ref, h_ref, wglu_ref, wau_ref, wsu_ref, wout_ref, g_ref, b_ref,
                hf_ref, hb_ref, *, alpha, n_batch):
    tm, d = h_ref.shape
    y = ys_ref[...]
    glu = jnp.dot(y, wglu_ref[...], preferred_element_type=F32)
    y_ssm = (y.astype(F32) * _sigmoid(glu)).astype(BF16)
    up_a = jnp.dot(ya_ref[...], wau_ref[...], preferred_element_type=F32)
    up_s = jnp.dot(y_ssm, wsu_ref[...], preferred_element_type=F32)
    merged = gate_ref[:, :d].astype(F32) * up_a + gate_ref[:, d:].astype(F32) * up_s
    mixed = jnp.dot(merged.astype(BF16), wout_ref[...], preferred_element_type=F32)
    out = _layer_norm(alpha * h_ref[...] + mixed, g_ref[...], b_ref[...])
    out = jnp.where(_real_row_mask(pl.program_id(0) * tm, tm, n_batch), out, 0.0)
    hf_ref[...] = out
    hb_ref[...] = out.astype(BF16)


MIX_ROWS = 2 * BLOCK


def _mix(y_attn, y_ssm, gates, h, wglu, wau, wsu, wout, g, b, layer, alpha, n_batch):
    t, d = h.shape
    tm = MIX_ROWS
    row = lambda w: pl.BlockSpec((tm, w), lambda i: (i, 0))
    return pl.pallas_call(
        functools.partial(_mix_kernel, alpha=alpha, n_batch=n_batch),
        grid=(t // tm,),
        in_specs=[row(y_attn.shape[1]), row(y_ssm.shape[1]), row(gates.shape[1]), row(d),
                  _layer_spec(wglu.shape, layer), _layer_spec(wau.shape, layer), _layer_spec(wsu.shape, layer),
                  _layer_spec(wout.shape, layer), _layer_spec(g.shape, layer), _layer_spec(b.shape, layer)],
        out_specs=[row(d), row(d)],
        out_shape=[jax.ShapeDtypeStruct((t, d), F32), jax.ShapeDtypeStruct((t, d), BF16)],
        compiler_params=_cparams("parallel"),
        name="mix_ln",
    )(y_attn, y_ssm, gates, h, wglu, wau, wsu, wout, g, b)


def _mlp_kernel(xb_ref, xf_ref, wup_ref, wdn_ref, g_ref, b_ref, *refs, alpha, n_batch, final):
    out_refs, acc_ref = refs[:-1], refs[-1]
    i, j = pl.program_id(0), pl.program_id(1)
    tm, d = xb_ref.shape

    @pl.when(j == 0)
    def _():
        acc_ref[...] = jnp.zeros_like(acc_ref)

    a = jnp.maximum(jnp.dot(xb_ref[...], wup_ref[...], preferred_element_type=F32), 0.0)
    acc_ref[...] += jnp.dot((a * a).astype(BF16), wdn_ref[...], preferred_element_type=F32)

    @pl.when(j == pl.num_programs(1) - 1)
    def _():
        out = _layer_norm(alpha * xf_ref[...] + acc_ref[...], g_ref[...], b_ref[...])
        if final:
            out_refs[0][...] = out.reshape(n_batch, BLOCK, d)
        else:
            out = jnp.where(_real_row_mask(i * tm, tm, n_batch), out, 0.0)
            out_refs[0][...] = out
            out_refs[1][...] = out.astype(BF16)


MLP_FF_CHUNK = 1024


def _mlp(hb, hf, wup, wdn, g, b, layer, alpha, n_batch, final):
    t, d = hf.shape
    n_chunks, tf = wup.shape[1], wup.shape[3]
    tm = n_batch * BLOCK
    first = 1 if final else 0
    row = lambda: pl.BlockSpec((tm, d), lambda i, j: (i + first, 0))
    if final:
        out_specs = [pl.BlockSpec((n_batch, BLOCK, d), lambda i, j: (0, i, 0))]
        out_shape = [jax.ShapeDtypeStruct((n_batch, t // n_batch - BLOCK, d), F32)]
    else:
        out_specs = [row(), row()]
        out_shape = [jax.ShapeDtypeStruct((t, d), F32), jax.ShapeDtypeStruct((t, d), BF16)]
    return pl.pallas_call(
        functools.partial(_mlp_kernel, alpha=alpha, n_batch=n_batch, final=final),
        grid=(t // tm - first, n_chunks),
        in_specs=[row(), row(),
                  pl.BlockSpec((None, None, d, tf), lambda i, j: (layer, j, 0, 0)),
                  pl.BlockSpec((None, tf, d), lambda i, j: (layer, j, 0)),
                  _layer_spec(g.shape, layer), _layer_spec(b.shape, layer)],
        out_specs=out_specs,
        out_shape=out_shape,
        scratch_shapes=[pltpu.VMEM((tm, d), F32)],
        compiler_params=_cparams("parallel", "arbitrary"),
        name="mlp_ln",
    )(hb, hf, wup, wdn, g, b)


def kernel(x, meta_tokens, ln_emb_g, ln_emb_b, rel_bias, in_proj, gate_b, attn_sinks, ssm_lambda_re,
           ssm_lambda_im, ssm_log_step, ssm_b_re, ssm_b_im, ssm_c_re, ssm_c_im, ssm_d, ssm_w_glu, w_attn_up,
           w_ssm_up, w_out, ln_mix_g, ln_mix_b, w_mlp_up, w_mlp_down, ln_mlp_g, ln_mlp_b):
    bsz, seq, d = x.shape
    depth = in_proj.shape[0]
    assert seq % BLOCK == 0 and 2 * bsz == SSM_SEQS and meta_tokens.shape[0] == N_META
    nb = seq // BLOCK + 1
    alpha = (2 * depth) ** 0.25
    ssm_w = ssm_w_glu.shape[2]
    row2 = lambda a: a.astype(F32).reshape(1, -1)
    rows = lambda a: a.astype(F32)[:, None, :]
    bf = lambda a: a.astype(BF16)

    meta_pad = jnp.concatenate([jnp.zeros((META_PAD, d), x.dtype), meta_tokens.astype(x.dtype)], axis=0)
    hf, hb = _embed_ln(x, meta_pad, row2(ln_emb_g), row2(ln_emb_b))
    bias_tables = _bias_tables(rel_bias)

    q_scale = jnp.where(jnp.arange(QKV_WIDTH) < ATTN_WIDTH, HEAD_DIM ** -0.5 * LOG2_E, 1.0).astype(in_proj.dtype)
    w_qkv = bf(in_proj[:, :, :QKV_WIDTH] * q_scale)
    w_u = bf(in_proj[:, :, QKV_WIDTH:QKV_WIDTH + ssm_w])
    w_gate = bf(in_proj[:, :, QKV_WIDTH + ssm_w:])
    w_glu, w_au, w_su, w_o = bf(ssm_w_glu), bf(w_attn_up), bf(w_ssm_up), bf(w_out)
    w_up = bf(w_mlp_up).reshape(depth, d, -1, MLP_FF_CHUNK).transpose(0, 2, 1, 3)
    w_dn = bf(w_mlp_down)
    sinks = attn_sinks.astype(F32) * LOG2_E
    ar, ai, bbr, bbi = _s5_discretize(ssm_lambda_re, ssm_lambda_im, ssm_log_step, ssm_b_re, ssm_b_im)
    wb, wc, a_r, a_i, d_rows = _s5_scan_weights(ar, ai, bbr, bbi, ssm_c_re, ssm_c_im, ssm_d)

    for l in range(depth):
        qkv, gates = _in_proj(hb, w_qkv, w_gate, rows(gate_b), l, bsz * BLOCK)
        y_attn = _attention(qkv, bias_tables, sinks, l, bsz, nb)
        y_gelu = _s5_scan(hb, w_u, wb, wc, a_r, a_i, d_rows, l, bsz)
        hf, hb = _mix(y_attn, y_gelu, gates, hf, w_glu, w_au, w_su, w_o, rows(ln_mix_g), rows(ln_mix_b),
                      l, alpha, bsz)
        out = _mlp(hb, hf, w_up, w_dn, rows(ln_mlp_g), rows(ln_mlp_b), l, alpha, bsz, final=(l == depth - 1))
        if l < depth - 1:
            hf, hb = out

    return out[0]
```

```python
import functools
import math

import jax
import jax.numpy as jnp
from jax import lax
from jax.experimental import pallas as pl
from jax.experimental.pallas import tpu as pltpu

F32 = jnp.float32
BF16 = jnp.bfloat16

N_META = 16
BLOCK = 128
META_PAD = BLOCK - N_META
HEAD_DIM = 64
N_Q_HEADS = 16
N_KV_HEADS = 4
Q_PER_KV = N_Q_HEADS // N_KV_HEADS
ATTN_WIDTH = N_Q_HEADS * HEAD_DIM
KV_WIDTH = N_KV_HEADS * HEAD_DIM
QKV_WIDTH = ATTN_WIDTH + 2 * KV_WIDTH
SSM_GROUP = 16
SSM_STATE = 64
N_BUCKETS = 32
MAX_DISTANCE = 128
LN_EPS = 1e-5
NEG_INF = -1e30
LOG2_E = math.log2(math.e)

LANES = 128
SUBLANES = 8
VMEM_LIMIT_BYTES = 56 * 1024 * 1024

SSM_SEQS = SUBLANES
STATES_PER_CHUNK = LANES
GROUPS_PER_CHUNK = STATES_PER_CHUNK // SSM_STATE
CHUNKS_PER_SLAB = 4
SLAB_STATE_LANES = CHUNKS_PER_SLAB * 2 * STATES_PER_CHUNK


def _cparams(*sem):
    return pltpu.CompilerParams(dimension_semantics=sem, vmem_limit_bytes=VMEM_LIMIT_BYTES)


def _resident(shape):
    nd = len(shape)
    return pl.BlockSpec(shape, lambda *_: (0,) * nd, pipeline_mode=pl.Buffered(1))


def _layer_spec(shape, layer):
    nd = len(shape)
    return pl.BlockSpec((None,) + tuple(shape[1:]), lambda *_: (layer,) + (0,) * (nd - 1),
                        pipeline_mode=pl.Buffered(1))


def _layer_norm(x, g, b):
    mu = jnp.mean(x, axis=-1, keepdims=True)
    xc = x - mu
    var = jnp.mean(xc * xc, axis=-1, keepdims=True)
    return xc * lax.rsqrt(var + LN_EPS) * g + b


def _sigmoid(z):
    return 1.0 / (1.0 + jnp.exp(-z))


def _real_row_mask(first_row, n_rows, n_batch):
    r = first_row + lax.broadcasted_iota(jnp.int32, (n_rows, 1), 0)
    is_pad = (r < n_batch * BLOCK) & ((r & (BLOCK - 1)) < META_PAD)
    return jnp.logical_not(is_pad)


def _embed_ln_kernel(meta_ref, x_ref, g_ref, b_ref, hf_ref, hb_ref):
    is_meta = pl.program_id(0) == 0
    rows = lax.broadcasted_iota(jnp.int32, (BLOCK, 1), 0)
    keep = jnp.logical_or(jnp.logical_not(is_meta), rows >= META_PAD)
    for bi in range(x_ref.shape[0]):
        xin = jnp.where(is_meta, meta_ref[...], x_ref[bi])
        y = jnp.where(keep, _layer_norm(xin, g_ref[...], b_ref[...]), 0.0)
        hf_ref[bi * BLOCK:(bi + 1) * BLOCK, :] = y
        hb_ref[bi * BLOCK:(bi + 1) * BLOCK, :] = y.astype(BF16)


def _embed_ln(x, meta_pad, g, b):
    bsz, seq, d = x.shape
    nb = seq // BLOCK + 1
    t = bsz * nb * BLOCK
    tile = pl.BlockSpec((bsz * BLOCK, d), lambda n: (n, 0))
    return pl.pallas_call(
        _embed_ln_kernel,
        grid=(nb,),
        in_specs=[
            _resident(meta_pad.shape),
            pl.BlockSpec((bsz, BLOCK, d), lambda n: (0, jnp.maximum(n - 1, 0), 0)),
            _resident(g.shape), _resident(b.shape),
        ],
        out_specs=[tile, tile],
        out_shape=[jax.ShapeDtypeStruct((t, d), F32), jax.ShapeDtypeStruct((t, d), BF16)],
        compiler_params=_cparams("parallel"),
        name="embed_ln",
    )(meta_pad, x, g, b)


GATE_COL_CHUNK = 1024


def _in_proj_kernel(x_ref, w_ref, gb_ref, qkv_ref, gate_ref):
    x = x_ref[...]
    qkv_ref[...] = jnp.dot(x, w_ref[:, :QKV_WIDTH], preferred_element_type=F32).astype(qkv_ref.dtype)
    gate0 = w_ref.shape[1] - gate_ref.shape[1]
    for c in range(0, gate_ref.shape[1], GATE_COL_CHUNK):
        z = jnp.dot(x, w_ref[:, gate0 + c:gate0 + c + GATE_COL_CHUNK], preferred_element_type=F32)
        z = z + gb_ref[:, c:c + GATE_COL_CHUNK]
        gate_ref[:, c:c + GATE_COL_CHUNK] = _sigmoid(z).astype(gate_ref.dtype)


def _in_proj(hb, w_in, w_layer, gate_bias, layer, tm):
    t, d = hb.shape
    gate_w = gate_bias.shape[2]
    row = lambda w: pl.BlockSpec((tm, w), lambda i: (i, 0))
    return pl.pallas_call(
        _in_proj_kernel,
        grid=(t // tm,),
        in_specs=[row(d), _layer_spec(w_in.shape, w_layer), _layer_spec(gate_bias.shape, layer)],
        out_specs=[row(QKV_WIDTH), row(gate_w)],
        out_shape=[jax.ShapeDtypeStruct((t, QKV_WIDTH), BF16), jax.ShapeDtypeStruct((t, gate_w), BF16)],
        compiler_params=_cparams("parallel"),
        name="in_proj",
    )(hb, w_in, gate_bias)


def _attn_kernel(sink_ref, q_ref, kvm_ref, kvp_ref, kvc_ref, bias_ref, o_ref, *, layer, n_batch):
    rows = Q_PER_KV * BLOCK
    low = lax.broadcasted_iota(jnp.int32, (1, LANES), 1) < HEAD_DIM
    head_of_row = lax.broadcasted_iota(jnp.int32, (rows, 1), 0) // BLOCK
    zero = jnp.zeros((), q_ref.dtype)
    ones = jnp.ones((2 * BLOCK, LANES), kvm_ref.dtype)
    contract_last = (((1,), (1,)), ((), ()))

    def batch_rows(b):
        return pl.ds(pl.multiple_of(b * BLOCK, BLOCK), BLOCK)

    def dup_keys(kv, kh):
        k = kv[:, kh * HEAD_DIM:(kh + 1) * HEAD_DIM]
        return jnp.concatenate([k, k], axis=1)

    def dup_values(kv, kh):
        v = kv[:, KV_WIDTH + kh * HEAD_DIM:KV_WIDTH + (kh + 1) * HEAD_DIM]
        return jnp.concatenate([v, v, ones[:kv.shape[0]]], axis=1)

    def scores(b, kh):
        kv_m = kvm_ref[batch_rows(b), :]
        kv_w = jnp.concatenate([kvp_ref[batch_rows(b), :], kvc_ref[batch_rows(b), :]], axis=0)
        cols = [q_ref[batch_rows(b), (kh * Q_PER_KV + 2 * c) * HEAD_DIM:(kh * Q_PER_KV + 2 * c + 2) * HEAD_DIM]
                for c in range(Q_PER_KV // 2)]
        qs = jnp.concatenate([jnp.where(low, cols[g // 2], zero) if g % 2 == 0 else jnp.where(low, zero, cols[g // 2])
                              for g in range(Q_PER_KV)], axis=0)
        s_m = lax.dot_general(qs, dup_keys(kv_m, kh), contract_last, preferred_element_type=F32)
        s_w = lax.dot_general(qs, dup_keys(kv_w, kh), contract_last, preferred_element_type=F32)
        return kv_m, kv_w, s_m + bias_ref[kh, :, :BLOCK], s_w + bias_ref[kh, :, BLOCK:]

    def finish(b, kh, kv_m, kv_w, s_m, s_w):
        sink = jnp.zeros((rows, 1), F32)
        for g in range(Q_PER_KV):
            sink = jnp.where(head_of_row == g, sink_ref[layer, kh * Q_PER_KV + g], sink)
        m = jnp.maximum(jnp.maximum(jnp.max(s_m, axis=-1, keepdims=True), jnp.max(s_w, axis=-1, keepdims=True)), sink)
        p_m = jnp.exp2(s_m - m).astype(BF16)
        p_w = jnp.exp2(s_w - m).astype(BF16)
        pv = (jnp.dot(p_m, dup_values(kv_m, kh), preferred_element_type=F32)
              + jnp.dot(p_w, dup_values(kv_w, kh), preferred_element_type=F32))
        denom = pv[:, 2 * HEAD_DIM:] + jnp.exp2(sink - m)
        o = pv[:, :2 * HEAD_DIM] * (1.0 / denom)
        for c in range(Q_PER_KV // 2):
            pair = jnp.where(low, o[2 * c * BLOCK:(2 * c + 1) * BLOCK], o[(2 * c + 1) * BLOCK:(2 * c + 2) * BLOCK])
            o_ref[batch_rows(b), (kh * Q_PER_KV + 2 * c) * HEAD_DIM:(kh * Q_PER_KV + 2 * c + 2) * HEAD_DIM] = (
                pair.astype(o_ref.dtype))

    def one_batch_row(b, carry):
        pending = scores(b, 0)
        for kh in range(N_KV_HEADS):
            ahead = scores(b, kh + 1) if kh + 1 < N_KV_HEADS else None
            finish(b, kh, *pending)
            pending = ahead
        return carry

    lax.fori_loop(0, n_batch, one_batch_row, 0)


def _attention(qkv, bias_tables, sinks, layer, bsz, nb):
    t = qkv.shape[0]
    tile = bsz * BLOCK
    kv_col = ATTN_WIDTH // (2 * KV_WIDTH)
    kv_spec = lambda row_fn: pl.BlockSpec((tile, 2 * KV_WIDTH), lambda n: (row_fn(n), kv_col))
    return pl.pallas_call(
        functools.partial(_attn_kernel, layer=layer, n_batch=bsz),
        grid=(nb,),
        in_specs=[
            pl.BlockSpec(memory_space=pltpu.SMEM),
            pl.BlockSpec((tile, ATTN_WIDTH), lambda n: (n, 0)),
            kv_spec(lambda n: 0),
            kv_spec(lambda n: jnp.maximum(n - 1, 0)),
            kv_spec(lambda n: n),
            pl.BlockSpec((None, N_KV_HEADS, Q_PER_KV * BLOCK, 3 * BLOCK), lambda n: (jnp.minimum(n, 2), 0, 0, 0)),
        ],
        out_specs=pl.BlockSpec((tile, ATTN_WIDTH), lambda n: (n, 0)),
        out_shape=jax.ShapeDtypeStruct((t, ATTN_WIDTH), BF16),
        compiler_params=_cparams("parallel"),
        name="swa",
    )(sinks, qkv, qkv, qkv, qkv, bias_tables)


def _t5_bucket(dist):
    n = jnp.maximum(dist, 0)
    max_exact = N_BUCKETS // 2
    nf = jnp.maximum(n, 1).astype(F32)
    large = max_exact + (jnp.log(nf / max_exact) / math.log(MAX_DISTANCE / max_exact)
                         * (N_BUCKETS - max_exact)).astype(jnp.int32)
    large = jnp.minimum(large, N_BUCKETS - 1)
    return jnp.where(n < max_exact, n, large)


def _bias_tables(rel_bias):
    blk = jnp.arange(3, dtype=jnp.int32)[:, None]
    j = jnp.arange(BLOCK, dtype=jnp.int32)[None, :]
    q_pos = blk * BLOCK + j
    k_pos = jnp.concatenate([jnp.broadcast_to(j, (3, BLOCK)), (blk - 1) * BLOCK + j, q_pos], axis=1)
    dist = q_pos[:, :, None] - k_pos[:, None, :]
    kp = k_pos[:, None, :]
    is_meta_seg = (jnp.arange(3 * BLOCK) < BLOCK)[None, None, :]
    valid = (dist >= 0) & jnp.where(is_meta_seg, kp >= META_PAD, (kp >= BLOCK) & (dist < BLOCK))
    onehot = jax.nn.one_hot(_t5_bucket(dist), N_BUCKETS, dtype=F32)
    bias = jnp.einsum("tqkb,bh->thqk", onehot, rel_bias.astype(F32), precision=lax.Precision.HIGHEST)
    bias = jnp.where(valid[:, None], bias * LOG2_E, NEG_INF)
    return bias.reshape(3, N_KV_HEADS, Q_PER_KV * BLOCK, 3 * BLOCK)


def _s5_discretize_kernel(lr_ref, li_ref, ls_ref, bre_ref, bim_ref, ar_ref, ai_ref, bbr_ref, bbi_ref):
    lr, li = lr_ref[...], li_ref[...]
    dt = jnp.exp(ls_ref[...])
    decay = jnp.exp(lr * dt)
    ar, ai = decay * jnp.cos(li * dt), decay * jnp.sin(li * dt)
    den = lr * lr + li * li
    nr, ni = ar - 1.0, ai
    zr = (nr * lr + ni * li) / den
    zi = (ni * lr - nr * li) / den
    bre, bim = bre_ref[...], bim_ref[...]
    ar_ref[...] = ar
    ai_ref[...] = ai
    bbr_ref[...] = zr * bre - zi * bim
    bbi_ref[...] = zr * bim + zi * bre


def _s5_discretize(lam_re, lam_im, log_step, b_re, b_im):
    depth, g, n, p = b_re.shape
    flat = lambda a: a.astype(F32).reshape(depth * g, n * p)
    rep = lambda a: flat(jnp.broadcast_to(a.astype(F32)[..., None], (depth, g, n, p)))
    ls = flat(jnp.broadcast_to(log_step.astype(F32)[:, :, None, None], (depth, g, n, p)))
    shp = jax.ShapeDtypeStruct((depth * g, n * p), F32)
    ar, ai, bbr, bbi = pl.pallas_call(
        _s5_discretize_kernel, out_shape=[shp] * 4, name="s5_discretize",
    )(rep(lam_re), rep(lam_im), ls, flat(b_re), flat(b_im))
    state = lambda a: a.reshape(depth, g, n, p)[..., 0]
    return state(ar), state(ai), bbr.reshape(depth, g, n, p), bbi.reshape(depth, g, n, p)


def _s5_scan_weights(ar, ai, bbr, bbi, c_re, c_im, d_skip):
    depth, g, n, p = bbr.shape
    halves, q_per, gl_per = 2, CHUNKS_PER_SLAB, GROUPS_PER_CHUNK
    gi_per = q_per * gl_per
    slabs = g // (halves * gi_per)
    n_ch = halves * gi_per * p
    gi_of_ch = (jnp.arange(n_ch) // p) % gi_per
    lane_blocks = [(q, ri, gl) for q in range(q_per) for ri in range(2) for gl in range(gl_per)]

    b_ri = jnp.stack([bbr, bbi], axis=1).reshape(depth, 2, halves, slabs, gi_per, n, p)
    b_rows = b_ri.transpose(0, 3, 1, 2, 4, 6, 5).reshape(depth, slabs, 2, n_ch, n)
    wb = jnp.concatenate([jnp.where((gi_of_ch == gl_per * q + gl)[:, None], b_rows[:, :, ri], 0.0)
                          for q, ri, gl in lane_blocks], axis=-1)

    c_ri = jnp.stack([c_re.astype(F32), -c_im.astype(F32)], axis=1).reshape(depth, 2, halves, slabs, gi_per, p, n)
    c_cols = c_ri.transpose(0, 3, 1, 6, 2, 4, 5).reshape(depth, slabs, 2, n, n_ch)
    wc = jnp.concatenate([jnp.where((gi_of_ch == gl_per * q + gl)[None, :], c_cols[:, :, ri], 0.0)
                          for q, ri, gl in lane_blocks], axis=-2)

    def per_seq(a):
        return jnp.repeat(a.astype(F32).reshape(depth, halves, -1), SSM_SEQS // halves, axis=1)

    return wb.astype(BF16), wc.astype(BF16), per_seq(ar), per_seq(ai), per_seq(d_skip)


def _s5_scan_kernel(x_ref, wu0_ref, wu1_ref, wb_ref, wc_ref, ar_ref, ai_ref, d_ref, y_ref,
                    u_scr, s_ref, y_scr, h_ref, *, n_batch, unroll):
    steps = BLOCK
    rows = steps * SSM_SEQS
    n_slabs = wb_ref.shape[0]
    half_w = n_slabs * LANES

    @pl.when(pl.program_id(0) == 0)
    def _():
        h_ref[...] = jnp.zeros_like(h_ref)

    def seq_rows(b, half):
        return pl.ds(half * n_batch + b, steps, stride=SSM_SEQS)

    def channels(half, j):
        return slice(half * half_w + j * LANES, half * half_w + (j + 1) * LANES)

    x = x_ref[...]
    u_half = [jnp.dot(x, w[...], preferred_element_type=F32) for w in (wu0_ref, wu1_ref)]
    for b in range(n_batch):
        for half in range(2):
            for j in range(n_slabs):
                u_scr[j, seq_rows(b, half), :] = u_half[half][b * steps:(b + 1) * steps, j * LANES:(j + 1) * LANES]

    half0 = (lax.broadcasted_iota(jnp.int32, (rows, 1), 0) & (SSM_SEQS - 1)) < (SSM_SEQS // 2)

    for j in range(n_slabs):
        uj = u_scr[j]
        lhs = jnp.concatenate([jnp.where(half0, uj, 0.0), jnp.where(half0, 0.0, uj)], axis=1).astype(BF16)
        s_ref[:, j * SLAB_STATE_LANES:(j + 1) * SLAB_STATE_LANES] = jnp.dot(
            lhs, wb_ref[j], preferred_element_type=F32)

    n_chunks = n_slabs * CHUNKS_PER_SLAB
    for c0 in range(0, n_chunks, CHUNKS_PER_SLAB):
        chunks = range(c0, c0 + CHUNKS_PER_SLAB)
        a_r = [ar_ref[:, c * LANES:(c + 1) * LANES] for c in chunks]
        a_i = [ai_ref[:, c * LANES:(c + 1) * LANES] for c in chunks]
        init = []
        for c in chunks:
            init += [h_ref[:, 2 * c * LANES:(2 * c + 1) * LANES], h_ref[:, (2 * c + 1) * LANES:(2 * c + 2) * LANES]]

        def body(t, carry, chunks=chunks, a_r=a_r, a_i=a_i):
            r0 = pl.multiple_of(t * SSM_SEQS, SSM_SEQS)
            new = []
            for k, c in enumerate(chunks):
                hr, hi = carry[2 * k], carry[2 * k + 1]
                re_sl = (pl.ds(r0, SSM_SEQS), pl.ds(2 * c * LANES, LANES))
                im_sl = (pl.ds(r0, SSM_SEQS), pl.ds((2 * c + 1) * LANES, LANES))
                nr = a_r[k] * hr - a_i[k] * hi + s_ref[re_sl]
                ni = a_r[k] * hi + a_i[k] * hr + s_ref[im_sl]
                s_ref[re_sl] = nr
                s_ref[im_sl] = ni
                new += [nr, ni]
            return tuple(new)

        final = lax.fori_loop(0, steps, body, tuple(init), unroll=unroll)
        for k, c in enumerate(chunks):
            h_ref[:, 2 * c * LANES:(2 * c + 1) * LANES] = final[2 * k]
            h_ref[:, (2 * c + 1) * LANES:(2 * c + 2) * LANES] = final[2 * k + 1]

    for j in range(n_slabs):
        hs = s_ref[:, j * SLAB_STATE_LANES:(j + 1) * SLAB_STATE_LANES].astype(BF16)
        both = jnp.dot(hs, wc_ref[j], preferred_element_type=F32)
        yj = jnp.where(half0, both[:, :LANES], both[:, LANES:])
        dj = jnp.tile(d_ref[:, j * LANES:(j + 1) * LANES], (steps, 1))
        y_scr[j] = jax.nn.gelu(yj + dj * u_scr[j])

    for b in range(n_batch):
        for half in range(2):
            for j in range(n_slabs):
                y_ref[b * steps:(b + 1) * steps, channels(half, j)] = y_scr[j, seq_rows(b, half), :].astype(y_ref.dtype)


def _s5_scan(hb, w_in, w_layer, wb, wc, a_r, a_i, d_rows, layer, n_batch):
    t, d = hb.shape
    tile = n_batch * BLOCK
    rows = BLOCK * SSM_SEQS
    n_slabs = wb.shape[1]
    state_lanes = n_slabs * SLAB_STATE_LANES
    half_w = n_slabs * LANES
    ssm_w = 2 * half_w
    u_block = QKV_WIDTH // half_w
    assert QKV_WIDTH % half_w == 0
    u_spec = lambda half: pl.BlockSpec((None, d, half_w), lambda i: (w_layer, 0, u_block + half),
                                       pipeline_mode=pl.Buffered(1))
    return pl.pallas_call(
        functools.partial(_s5_scan_kernel, n_batch=n_batch, unroll=True),
        grid=(t // tile,),
        in_specs=[
            pl.BlockSpec((tile, d), lambda i: (i, 0)),
            u_spec(0), u_spec(1), _layer_spec(wb.shape, layer), _layer_spec(wc.shape, layer),
            _layer_spec(a_r.shape, layer), _layer_spec(a_i.shape, layer), _layer_spec(d_rows.shape, layer),
        ],
        out_specs=pl.BlockSpec((tile, ssm_w), lambda i: (i, 0)),
        out_shape=jax.ShapeDtypeStruct((t, ssm_w), BF16),
        scratch_shapes=[pltpu.VMEM((n_slabs, rows, LANES), F32), pltpu.VMEM((rows, state_lanes), F32),
                        pltpu.VMEM((n_slabs, rows, LANES), F32), pltpu.VMEM((SSM_SEQS, state_lanes), F32)],
        compiler_params=_cparams("arbitrary"),
        name="s5_scan",
    )(hb, w_in, w_in, wb, wc, a_r, a_i, d_rows)


def _mix_kernel(ya_ref, ys_ref, gate_ref, h_ref, wglu_ref, wau_ref, wsu_ref, wout_ref, g_ref, b_ref,
                hf_ref, hb_ref, *, alpha, n_batch):
    tm, d = h_ref.shape
    y = ys_ref[...]
    glu = jnp.dot(y, wglu_ref[...], preferred_element_type=F32)
    y_ssm = (y.astype(F32) * _sigmoid(glu)).astype(BF16)
    up_a = jnp.dot(ya_ref[...], wau_ref[...], preferred_element_type=F32)
    up_s = jnp.dot(y_ssm, wsu_ref[...], preferred_element_type=F32)
    merged = gate_ref[:, :d].astype(F32) * up_a + gate_ref[:, d:].astype(F32) * up_s
    mixed = jnp.dot(merged.astype(BF16), wout_ref[...], preferred_element_type=F32)
    out = _layer_norm(alpha * h_ref[...] + mixed, g_ref[...], b_ref[...])
    out = jnp.where(_real_row_mask(pl.program_id(0) * tm, tm, n_batch), out, 0.0)
    hf_ref[...] = out
    hb_ref[...] = out.astype(BF16)


MIX_ROWS = 2 * BLOCK


def _mix(y_attn, y_ssm, gates, h, wglu, wau, wsu, wout, g, b, layer, alpha, n_batch):
    t, d = h.shape
    tm = MIX_ROWS
    row = lambda w: pl.BlockSpec((tm, w), lambda i: (i, 0))
    return pl.pallas_call(
        functools.partial(_mix_kernel, alpha=alpha, n_batch=n_batch),
        grid=(t // tm,),
        in_specs=[row(y_attn.shape[1]), row(y_ssm.shape[1]), row(gates.shape[1]), row(d),
                  _layer_spec(wglu.shape, layer), _layer_spec(wau.shape, layer), _layer_spec(wsu.shape, layer),
                  _layer_spec(wout.shape, layer), _layer_spec(g.shape, layer), _layer_spec(b.shape, layer)],
        out_specs=[row(d), row(d)],
        out_shape=[jax.ShapeDtypeStruct((t, d), F32), jax.ShapeDtypeStruct((t, d), BF16)],
        compiler_params=_cparams("parallel"),
        name="mix_ln",
    )(y_attn, y_ssm, gates, h, wglu, wau, wsu, wout, g, b)


def _mlp_kernel(xb_ref, xf_ref, wup_ref, wdn_ref, g_ref, b_ref, *refs, alpha, n_batch, final, cast_next):
    if cast_next:
        (win_f, scale_ref, wup_f, wdn_f), refs = refs[:4], refs[4:]
        out_refs, (win_o, wup_o, wdn_o), acc_ref = refs[:-4], refs[-4:-1], refs[-1]
    else:
        out_refs, acc_ref = refs[:-1], refs[-1]
    i, j = pl.program_id(0), pl.program_id(1)
    tm, d = xb_ref.shape

    @pl.when(j == 0)
    def _():
        acc_ref[...] = jnp.zeros_like(acc_ref)

    a = jnp.maximum(jnp.dot(xb_ref[...], wup_ref[...], preferred_element_type=F32), 0.0)
    acc_ref[...] += jnp.dot((a * a).astype(BF16), wdn_ref[...], preferred_element_type=F32)
    if cast_next:
        win_o[...] = (win_f[...] * scale_ref[...]).astype(win_o.dtype)
        wup_o[...] = wup_f[...].astype(wup_o.dtype)
        wdn_o[...] = wdn_f[...].astype(wdn_o.dtype)

    @pl.when(j == pl.num_programs(1) - 1)
    def _():
        out = _layer_norm(alpha * xf_ref[...] + acc_ref[...], g_ref[...], b_ref[...])
        if final:
            out_refs[0][...] = out.reshape(n_batch, BLOCK, d)
        else:
            out = jnp.where(_real_row_mask(i * tm, tm, n_batch), out, 0.0)
            out_refs[0][...] = out
            out_refs[1][...] = out.astype(BF16)


def _even_parts(total, max_parts, align):
    return max(p for p in range(1, max_parts + 1) if total % p == 0 and (total // p) % align == 0)


def _mlp(hb, hf, wup, wdn, w_layer, g, b, layer, alpha, n_batch, final, cast_next=None):
    t, d = hf.shape
    dff = wup.shape[2]
    tm = n_batch * BLOCK
    tf = 1024
    first = 1 if final else 0
    n_i, n_j = t // tm - first, dff // tf
    row = lambda: pl.BlockSpec((tm, d), lambda i, j: (i + first, 0))
    if final:
        out_specs = [pl.BlockSpec((n_batch, BLOCK, d), lambda i, j: (0, i, 0))]
        out_shape = [jax.ShapeDtypeStruct((n_batch, t // n_batch - BLOCK, d), F32)]
    else:
        out_specs = [row(), row()]
        out_shape = [jax.ShapeDtypeStruct((t, d), F32), jax.ShapeDtypeStruct((t, d), BF16)]
    in_specs = [row(), row(),
                pl.BlockSpec((None, d, tf), lambda i, j: (w_layer, 0, j)),
                pl.BlockSpec((None, tf, d), lambda i, j: (w_layer, j, 0)),
                _layer_spec(g.shape, layer), _layer_spec(b.shape, layer)]
    args = [hb, hf, wup, wdn, g, b]
    if cast_next is not None:
        w_in_f, scale, w_up_f, w_dn_f, nxt = cast_next
        def blocks(shape):
            r, c = shape[1], shape[2]
            p_c = _even_parts(c, n_j, LANES)
            p_r = _even_parts(r, (n_i * n_j) // p_c, 2 * SUBLANES)

            def idx(i, j):
                s = jnp.minimum(i * n_j + j, p_r * p_c - 1)
                return s // p_c, s % p_c
            return (r // p_r, c // p_c), idx

        for w_f in (w_in_f, w_up_f, w_dn_f):
            blk, idx = blocks(w_f.shape)
            in_specs.append(pl.BlockSpec((None,) + blk, lambda i, j, idx=idx: (nxt,) + idx(i, j)))
            args.append(w_f)
            out_specs.append(pl.BlockSpec((None,) + blk, lambda i, j, idx=idx: (0,) + idx(i, j)))
            out_shape.append(jax.ShapeDtypeStruct((1,) + w_f.shape[1:], BF16))
            if w_f is w_in_f:
                in_specs.append(pl.BlockSpec((None, 1, blk[1]), lambda i, j, idx=idx: (0, 0, idx(i, j)[1])))
                args.append(scale)
    return pl.pallas_call(
        functools.partial(_mlp_kernel, alpha=alpha, n_batch=n_batch, final=final, cast_next=cast_next is not None),
        grid=(n_i, n_j),
        in_specs=in_specs,
        out_specs=out_specs,
        out_shape=out_shape,
        scratch_shapes=[pltpu.VMEM((tm, d), F32)],
        compiler_params=_cparams("parallel", "arbitrary"),
        name="mlp_ln",
    )(*args)


def kernel(x, meta_tokens, ln_emb_g, ln_emb_b, rel_bias, in_proj, gate_b, attn_sinks, ssm_lambda_re,
           ssm_lambda_im, ssm_log_step, ssm_b_re, ssm_b_im, ssm_c_re, ssm_c_im, ssm_d, ssm_w_glu, w_attn_up,
           w_ssm_up, w_out, ln_mix_g, ln_mix_b, w_mlp_up, w_mlp_down, ln_mlp_g, ln_mlp_b):
    bsz, seq, d = x.shape
    depth = in_proj.shape[0]
    assert seq % BLOCK == 0 and 2 * bsz == SSM_SEQS and meta_tokens.shape[0] == N_META
    nb = seq // BLOCK + 1
    alpha = (2 * depth) ** 0.25
    ssm_w = ssm_w_glu.shape[2]
    row2 = lambda a: a.astype(F32).reshape(1, -1)
    rows = lambda a: a.astype(F32)[:, None, :]
    bf = lambda a: a.astype(BF16)

    meta_pad = jnp.concatenate([jnp.zeros((META_PAD, d), x.dtype), meta_tokens.astype(x.dtype)], axis=0)
    hf, hb = _embed_ln(x, meta_pad, row2(ln_emb_g), row2(ln_emb_b))
    bias_tables = _bias_tables(rel_bias)

    q_scale = jnp.where(jnp.arange(in_proj.shape[2]) < ATTN_WIDTH, HEAD_DIM ** -0.5 * LOG2_E, 1.0)
    q_scale = q_scale.astype(in_proj.dtype)[None, None, :]
    w_glu, w_au, w_su, w_o = bf(ssm_w_glu), bf(w_attn_up), bf(w_ssm_up), bf(w_out)
    sinks = attn_sinks.astype(F32) * LOG2_E
    ar, ai, bbr, bbi = _s5_discretize(ssm_lambda_re, ssm_lambda_im, ssm_log_step, ssm_b_re, ssm_b_im)
    wb, wc, a_r, a_i, d_rows = _s5_scan_weights(ar, ai, bbr, bbi, ssm_c_re, ssm_c_im, ssm_d)

    w_in, w_up, w_dn = bf(in_proj[:1] * q_scale), bf(w_mlp_up[:1]), bf(w_mlp_down[:1])

    for l in range(depth):
        final = l == depth - 1
        qkv, gates = _in_proj(hb, w_in, 0, rows(gate_b), l, bsz * BLOCK)
        y_attn = _attention(qkv, bias_tables, sinks, l, bsz, nb)
        y_gelu = _s5_scan(hb, w_in, 0, wb, wc, a_r, a_i, d_rows, l, bsz)
        hf, hb = _mix(y_attn, y_gelu, gates, hf, w_glu, w_au, w_su, w_o, rows(ln_mix_g), rows(ln_mix_b),
                      l, alpha, bsz)
        cast_next = None if final else (in_proj, q_scale, w_mlp_up, w_mlp_down, l + 1)
        out = _mlp(hb, hf, w_up, w_dn, 0, rows(ln_mlp_g), rows(ln_mlp_b), l, alpha, bsz, final, cast_next)
        if not final:
            hf, hb, w_in, w_up, w_dn = out

    return out[0]
```

```python
import functools
import math

import jax
import jax.numpy as jnp
from jax import lax
from jax.experimental import pallas as pl
from jax.experimental.pallas import tpu as pltpu

F32 = jnp.float32
BF16 = jnp.bfloat16

N_META = 16
BLOCK = 128
META_PAD = BLOCK - N_META
HEAD_DIM = 64
N_Q_HEADS = 16
N_KV_HEADS = 4
Q_PER_KV = N_Q_HEADS // N_KV_HEADS
ATTN_WIDTH = N_Q_HEADS * HEAD_DIM
KV_WIDTH = N_KV_HEADS * HEAD_DIM
QKV_WIDTH = ATTN_WIDTH + 2 * KV_WIDTH
SSM_GROUP = 16
SSM_STATE = 64
N_BUCKETS = 32
MAX_DISTANCE = 128
LN_EPS = 1e-5
NEG_INF = -1e30
LOG2_E = math.log2(math.e)

LANES = 128
SUBLANES = 8
VMEM_LIMIT_BYTES = 56 * 1024 * 1024

SSM_SEQS = SUBLANES
STATES_PER_CHUNK = LANES
GROUPS_PER_CHUNK = STATES_PER_CHUNK // SSM_STATE
CHUNKS_PER_SLAB = 4
SLAB_STATE_LANES = CHUNKS_PER_SLAB * 2 * STATES_PER_CHUNK


def _cparams(*sem):
    return pltpu.CompilerParams(dimension_semantics=sem, vmem_limit_bytes=VMEM_LIMIT_BYTES)


def _resident(shape):
    nd = len(shape)
    return pl.BlockSpec(shape, lambda *_: (0,) * nd, pipeline_mode=pl.Buffered(1))


def _layer_spec(shape, layer):
    nd = len(shape)
    return pl.BlockSpec((None,) + tuple(shape[1:]), lambda *_: (layer,) + (0,) * (nd - 1),
                        pipeline_mode=pl.Buffered(1))


def _layer_norm(x, g, b):
    mu = jnp.mean(x, axis=-1, keepdims=True)
    xc = x - mu
    var = jnp.mean(xc * xc, axis=-1, keepdims=True)
    return xc * lax.rsqrt(var + LN_EPS) * g + b


def _sigmoid(z):
    return 1.0 / (1.0 + jnp.exp(-z))


def _real_row_mask(first_row, n_rows, n_batch):
    r = first_row + lax.broadcasted_iota(jnp.int32, (n_rows, 1), 0)
    is_pad = (r < n_batch * BLOCK) & ((r & (BLOCK - 1)) < META_PAD)
    return jnp.logical_not(is_pad)


def _embed_ln_kernel(meta_ref, x_ref, g_ref, b_ref, hf_ref, hb_ref):
    is_meta = pl.program_id(0) == 0
    rows = lax.broadcasted_iota(jnp.int32, (BLOCK, 1), 0)
    keep = jnp.logical_or(jnp.logical_not(is_meta), rows >= META_PAD)
    for bi in range(x_ref.shape[0]):
        xin = jnp.where(is_meta, meta_ref[...], x_ref[bi])
        y = jnp.where(keep, _layer_norm(xin, g_ref[...], b_ref[...]), 0.0)
        hf_ref[bi * BLOCK:(bi + 1) * BLOCK, :] = y
        hb_ref[bi * BLOCK:(bi + 1) * BLOCK, :] = y.astype(BF16)


def _embed_ln(x, meta_pad, g, b):
    bsz, seq, d = x.shape
    nb = seq // BLOCK + 1
    t = bsz * nb * BLOCK
    tile = pl.BlockSpec((bsz * BLOCK, d), lambda n: (n, 0))
    return pl.pallas_call(
        _embed_ln_kernel,
        grid=(nb,),
        in_specs=[
            _resident(meta_pad.shape),
            pl.BlockSpec((bsz, BLOCK, d), lambda n: (0, jnp.maximum(n - 1, 0), 0)),
            _resident(g.shape), _resident(b.shape),
        ],
        out_specs=[tile, tile],
        out_shape=[jax.ShapeDtypeStruct((t, d), F32), jax.ShapeDtypeStruct((t, d), BF16)],
        compiler_params=_cparams("parallel"),
        name="embed_ln",
    )(meta_pad, x, g, b)


GATE_COL_CHUNK = 1024


def _in_proj_kernel(x_ref, w_ref, gb_ref, qkv_ref, gate_ref):
    x = x_ref[...]
    qkv_ref[...] = jnp.dot(x, w_ref[:, :QKV_WIDTH], preferred_element_type=F32).astype(qkv_ref.dtype)
    gate0 = w_ref.shape[1] - gate_ref.shape[1]
    for c in range(0, gate_ref.shape[1], GATE_COL_CHUNK):
        z = jnp.dot(x, w_ref[:, gate0 + c:gate0 + c + GATE_COL_CHUNK], preferred_element_type=F32)
        z = z + gb_ref[:, c:c + GATE_COL_CHUNK]
        gate_ref[:, c:c + GATE_COL_CHUNK] = _sigmoid(z).astype(gate_ref.dtype)


def _in_proj(hb, w_in, w_layer, gate_bias, layer, tm):
    t, d = hb.shape
    gate_w = gate_bias.shape[2]
    row = lambda w: pl.BlockSpec((tm, w), lambda i: (i, 0))
    return pl.pallas_call(
        _in_proj_kernel,
        grid=(t // tm,),
        in_specs=[row(d), _layer_spec(w_in.shape, w_layer), _layer_spec(gate_bias.shape, layer)],
        out_specs=[row(QKV_WIDTH), row(gate_w)],
        out_shape=[jax.ShapeDtypeStruct((t, QKV_WIDTH), BF16), jax.ShapeDtypeStruct((t, gate_w), BF16)],
        compiler_params=_cparams("parallel"),
        name="in_proj",
    )(hb, w_in, gate_bias)


def _attn_kernel(sink_ref, q_ref, kvm_ref, kvp_ref, kvc_ref, bias_ref, o_ref, *, layer):
    kv_m = kvm_ref[...]
    kv_w = jnp.concatenate([kvp_ref[...], kvc_ref[...]], axis=0)
    rows = Q_PER_KV * BLOCK
    low = lax.broadcasted_iota(jnp.int32, (1, LANES), 1) < HEAD_DIM
    head_of_row = lax.broadcasted_iota(jnp.int32, (rows, 1), 0) // BLOCK
    zero = jnp.zeros((), q_ref.dtype)
    contract_last = (((1,), (1,)), ((), ()))

    def dup_keys(kv, kh):
        k = kv[:, kh * HEAD_DIM:(kh + 1) * HEAD_DIM]
        return jnp.concatenate([k, k], axis=1)

    def dup_values(kv, kh):
        v = kv[:, KV_WIDTH + kh * HEAD_DIM:KV_WIDTH + (kh + 1) * HEAD_DIM]
        return jnp.concatenate([v, v, jnp.ones((kv.shape[0], LANES), kv.dtype)], axis=1)

    def scores(kh):
        cols = [q_ref[:, (kh * Q_PER_KV + 2 * c) * HEAD_DIM:(kh * Q_PER_KV + 2 * c + 2) * HEAD_DIM]
                for c in range(Q_PER_KV // 2)]
        qs = jnp.concatenate([jnp.where(low, cols[g // 2], zero) if g % 2 == 0 else jnp.where(low, zero, cols[g // 2])
                              for g in range(Q_PER_KV)], axis=0)
        s_m = lax.dot_general(qs, dup_keys(kv_m, kh), contract_last, preferred_element_type=F32)
        s_w = lax.dot_general(qs, dup_keys(kv_w, kh), contract_last, preferred_element_type=F32)
        return s_m + bias_ref[kh, :, :BLOCK], s_w + bias_ref[kh, :, BLOCK:]

    def finish(kh, s_m, s_w):
        sink = jnp.zeros((rows, 1), F32)
        for g in range(Q_PER_KV):
            sink = jnp.where(head_of_row == g, sink_ref[layer, kh * Q_PER_KV + g], sink)
        m = jnp.maximum(jnp.maximum(jnp.max(s_m, axis=-1, keepdims=True), jnp.max(s_w, axis=-1, keepdims=True)), sink)
        p_m = jnp.exp2(s_m - m).astype(BF16)
        p_w = jnp.exp2(s_w - m).astype(BF16)
        pv = (jnp.dot(p_m, dup_values(kv_m, kh), preferred_element_type=F32)
              + jnp.dot(p_w, dup_values(kv_w, kh), preferred_element_type=F32))
        denom = pv[:, 2 * HEAD_DIM:] + jnp.exp2(sink - m)
        o = pv[:, :2 * HEAD_DIM] * (1.0 / denom)
        for c in range(Q_PER_KV // 2):
            pair = jnp.where(low, o[2 * c * BLOCK:(2 * c + 1) * BLOCK], o[(2 * c + 1) * BLOCK:(2 * c + 2) * BLOCK])
            o_ref[:, (kh * Q_PER_KV + 2 * c) * HEAD_DIM:(kh * Q_PER_KV + 2 * c + 2) * HEAD_DIM] = pair.astype(o_ref.dtype)

    pending = scores(0)
    for kh in range(N_KV_HEADS):
        ahead = scores(kh + 1) if kh + 1 < N_KV_HEADS else None
        finish(kh, *pending)
        pending = ahead


def _attention(qkv, bias_tables, sinks, layer, bsz, nb):
    t = qkv.shape[0]
    kv_col = ATTN_WIDTH // (2 * KV_WIDTH)
    kv_spec = lambda row_fn: pl.BlockSpec((BLOCK, 2 * KV_WIDTH), lambda n, b: (row_fn(n, b), kv_col))
    return pl.pallas_call(
        functools.partial(_attn_kernel, layer=layer),
        grid=(nb, bsz),
        in_specs=[
            pl.BlockSpec(memory_space=pltpu.SMEM),
            pl.BlockSpec((BLOCK, ATTN_WIDTH), lambda n, b: (n * bsz + b, 0)),
            kv_spec(lambda n, b: b),
            kv_spec(lambda n, b: jnp.maximum(n - 1, 0) * bsz + b),
            kv_spec(lambda n, b: n * bsz + b),
            pl.BlockSpec((None, N_KV_HEADS, Q_PER_KV * BLOCK, 3 * BLOCK),
                         lambda n, b: (jnp.minimum(n, 2), 0, 0, 0)),
        ],
        out_specs=pl.BlockSpec((BLOCK, ATTN_WIDTH), lambda n, b: (n * bsz + b, 0)),
        out_shape=jax.ShapeDtypeStruct((t, ATTN_WIDTH), BF16),
        compiler_params=_cparams("parallel", "parallel"),
        name="swa",
    )(sinks, qkv, qkv, qkv, qkv, bias_tables)


def _t5_bucket(dist):
    n = jnp.maximum(dist, 0)
    max_exact = N_BUCKETS // 2
    nf = jnp.maximum(n, 1).astype(F32)
    large = max_exact + (jnp.log(nf / max_exact) / math.log(MAX_DISTANCE / max_exact)
                         * (N_BUCKETS - max_exact)).astype(jnp.int32)
    large = jnp.minimum(large, N_BUCKETS - 1)
    return jnp.where(n < max_exact, n, large)


def _bias_tables(rel_bias):
    blk = jnp.arange(3, dtype=jnp.int32)[:, None]
    j = jnp.arange(BLOCK, dtype=jnp.int32)[None, :]
    q_pos = blk * BLOCK + j
    k_pos = jnp.concatenate([jnp.broadcast_to(j, (3, BLOCK)), (blk - 1) * BLOCK + j, q_pos], axis=1)
    dist = q_pos[:, :, None] - k_pos[:, None, :]
    kp = k_pos[:, None, :]
    is_meta_seg = (jnp.arange(3 * BLOCK) < BLOCK)[None, None, :]
    valid = (dist >= 0) & jnp.where(is_meta_seg, kp >= META_PAD, (kp >= BLOCK) & (dist < BLOCK))
    onehot = jax.nn.one_hot(_t5_bucket(dist), N_BUCKETS, dtype=F32)
    bias = jnp.einsum("tqkb,bh->thqk", onehot, rel_bias.astype(F32), precision=lax.Precision.HIGHEST)
    bias = jnp.where(valid[:, None], bias * LOG2_E, NEG_INF)
    return bias.reshape(3, N_KV_HEADS, Q_PER_KV * BLOCK, 3 * BLOCK)


def _s5_discretize_kernel(lr_ref, li_ref, ls_ref, bre_ref, bim_ref, ar_ref, ai_ref, bbr_ref, bbi_ref):
    lr, li = lr_ref[...], li_ref[...]
    dt = jnp.exp(ls_ref[...])
    decay = jnp.exp(lr * dt)
    ar, ai = decay * jnp.cos(li * dt), decay * jnp.sin(li * dt)
    den = lr * lr + li * li
    nr, ni = ar - 1.0, ai
    zr = (nr * lr + ni * li) / den
    zi = (ni * lr - nr * li) / den
    bre, bim = bre_ref[...], bim_ref[...]
    ar_ref[...] = ar
    ai_ref[...] = ai
    bbr_ref[...] = zr * bre - zi * bim
    bbi_ref[...] = zr * bim + zi * bre


def _s5_discretize(lam_re, lam_im, log_step, b_re, b_im):
    depth, g, n, p = b_re.shape
    flat = lambda a: a.astype(F32).reshape(depth * g, n * p)
    rep = lambda a: flat(jnp.broadcast_to(a.astype(F32)[..., None], (depth, g, n, p)))
    ls = flat(jnp.broadcast_to(log_step.astype(F32)[:, :, None, None], (depth, g, n, p)))
    shp = jax.ShapeDtypeStruct((depth * g, n * p), F32)
    ar, ai, bbr, bbi = pl.pallas_call(
        _s5_discretize_kernel, out_shape=[shp] * 4, name="s5_discretize",
    )(rep(lam_re), rep(lam_im), ls, flat(b_re), flat(b_im))
    state = lambda a: a.reshape(depth, g, n, p)[..., 0]
    return state(ar), state(ai), bbr.reshape(depth, g, n, p), bbi.reshape(depth, g, n, p)


def _s5_scan_weights(ar, ai, bbr, bbi, c_re, c_im, d_skip):
    depth, g, n, p = bbr.shape
    halves, q_per, gl_per = 2, CHUNKS_PER_SLAB, GROUPS_PER_CHUNK
    gi_per = q_per * gl_per
    slabs = g // (halves * gi_per)
    n_ch = halves * gi_per * p
    gi_of_ch = (jnp.arange(n_ch) // p) % gi_per
    lane_blocks = [(q, ri, gl) for q in range(q_per) for ri in range(2) for gl in range(gl_per)]

    b_ri = jnp.stack([bbr, bbi], axis=1).reshape(depth, 2, halves, slabs, gi_per, n, p)
    b_rows = b_ri.transpose(0, 3, 1, 2, 4, 6, 5).reshape(depth, slabs, 2, n_ch, n)
    wb = jnp.concatenate([jnp.where((gi_of_ch == gl_per * q + gl)[:, None], b_rows[:, :, ri], 0.0)
                          for q, ri, gl in lane_blocks], axis=-1)

    c_ri = jnp.stack([c_re.astype(F32), -c_im.astype(F32)], axis=1).reshape(depth, 2, halves, slabs, gi_per, p, n)
    c_cols = c_ri.transpose(0, 3, 1, 6, 2, 4, 5).reshape(depth, slabs, 2, n, n_ch)
    wc = jnp.concatenate([jnp.where((gi_of_ch == gl_per * q + gl)[None, :], c_cols[:, :, ri], 0.0)
                          for q, ri, gl in lane_blocks], axis=-2)

    def per_seq(a):
        return jnp.repeat(a.astype(F32).reshape(depth, halves, -1), SSM_SEQS // halves, axis=1)

    return wb.astype(BF16), wc.astype(BF16), per_seq(ar), per_seq(ai), per_seq(d_skip)


def _s5_scan_kernel(x_ref, wu0_ref, wu1_ref, wb_ref, wc_ref, ar_ref, ai_ref, d_ref, y_ref,
                    u_scr, s_ref, y_scr, h_ref, *, n_batch, unroll):
    steps = BLOCK
    rows = steps * SSM_SEQS
    n_slabs = wb_ref.shape[0]
    half_w = n_slabs * LANES

    @pl.when(pl.program_id(0) == 0)
    def _():
        h_ref[...] = jnp.zeros_like(h_ref)

    def seq_rows(b, half):
        return pl.ds(half * n_batch + b, steps, stride=SSM_SEQS)

    def channels(half, j):
        return slice(half * half_w + j * LANES, half * half_w + (j + 1) * LANES)

    x = x_ref[...]
    u_half = [jnp.dot(x, w[...], preferred_element_type=F32) for w in (wu0_ref, wu1_ref)]
    for b in range(n_batch):
        for half in range(2):
            for j in range(n_slabs):
                u_scr[j, seq_rows(b, half), :] = u_half[half][b * steps:(b + 1) * steps, j * LANES:(j + 1) * LANES]

    half0 = (lax.broadcasted_iota(jnp.int32, (rows, 1), 0) & (SSM_SEQS - 1)) < (SSM_SEQS // 2)

    for j in range(n_slabs):
        uj = u_scr[j]
        lhs = jnp.concatenate([jnp.where(half0, uj, 0.0), jnp.where(half0, 0.0, uj)], axis=1).astype(BF16)
        s_ref[:, j * SLAB_STATE_LANES:(j + 1) * SLAB_STATE_LANES] = jnp.dot(
            lhs, wb_ref[j], preferred_element_type=F32)

    n_chunks = n_slabs * CHUNKS_PER_SLAB
    for c0 in range(0, n_chunks, CHUNKS_PER_SLAB):
        chunks = range(c0, c0 + CHUNKS_PER_SLAB)
        a_r = [ar_ref[:, c * LANES:(c + 1) * LANES] for c in chunks]
        a_i = [ai_ref[:, c * LANES:(c + 1) * LANES] for c in chunks]
        init = []
        for c in chunks:
            init += [h_ref[:, 2 * c * LANES:(2 * c + 1) * LANES], h_ref[:, (2 * c + 1) * LANES:(2 * c + 2) * LANES]]

        def body(t, carry, chunks=chunks, a_r=a_r, a_i=a_i):
            r0 = pl.multiple_of(t * SSM_SEQS, SSM_SEQS)
            new = []
            for k, c in enumerate(chunks):
                hr, hi = carry[2 * k], carry[2 * k + 1]
                re_sl = (pl.ds(r0, SSM_SEQS), pl.ds(2 * c * LANES, LANES))
                im_sl = (pl.ds(r0, SSM_SEQS), pl.ds((2 * c + 1) * LANES, LANES))
                nr = a_r[k] * hr - a_i[k] * hi + s_ref[re_sl]
                ni = a_r[k] * hi + a_i[k] * hr + s_ref[im_sl]
                s_ref[re_sl] = nr
                s_ref[im_sl] = ni
                new += [nr, ni]
            return tuple(new)

        final = lax.fori_loop(0, steps, body, tuple(init), unroll=unroll)
        for k, c in enumerate(chunks):
            h_ref[:, 2 * c * LANES:(2 * c + 1) * LANES] = final[2 * k]
            h_ref[:, (2 * c + 1) * LANES:(2 * c + 2) * LANES] = final[2 * k + 1]

    for j in range(n_slabs):
        hs = s_ref[:, j * SLAB_STATE_LANES:(j + 1) * SLAB_STATE_LANES].astype(BF16)
        both = jnp.dot(hs, wc_ref[j], preferred_element_type=F32)
        yj = jnp.where(half0, both[:, :LANES], both[:, LANES:])
        dj = jnp.tile(d_ref[:, j * LANES:(j + 1) * LANES], (steps, 1))
        y_scr[j] = jax.nn.gelu(yj + dj * u_scr[j])

    for b in range(n_batch):
        for half in range(2):
            for j in range(n_slabs):
                y_ref[b * steps:(b + 1) * steps, channels(half, j)] = y_scr[j, seq_rows(b, half), :].astype(y_ref.dtype)


def _s5_scan(hb, w_in, w_layer, wb, wc, a_r, a_i, d_rows, layer, n_batch):
    t, d = hb.shape
    tile = n_batch * BLOCK
    rows = BLOCK * SSM_SEQS
    n_slabs = wb.shape[1]
    state_lanes = n_slabs * SLAB_STATE_LANES
    half_w = n_slabs * LANES
    ssm_w = 2 * half_w
    u_block = QKV_WIDTH // half_w
    assert QKV_WIDTH % half_w == 0
    u_spec = lambda half: pl.BlockSpec((None, d, half_w), lambda i: (w_layer, 0, u_block + half),
                                       pipeline_mode=pl.Buffered(1))
    return pl.pallas_call(
        functools.partial(_s5_scan_kernel, n_batch=n_batch, unroll=True),
        grid=(t // tile,),
        in_specs=[
            pl.BlockSpec((tile, d), lambda i: (i, 0)),
            u_spec(0), u_spec(1), _layer_spec(wb.shape, layer), _layer_spec(wc.shape, layer),
            _layer_spec(a_r.shape, layer), _layer_spec(a_i.shape, layer), _layer_spec(d_rows.shape, layer),
        ],
        out_specs=pl.BlockSpec((tile, ssm_w), lambda i: (i, 0)),
        out_shape=jax.ShapeDtypeStruct((t, ssm_w), BF16),
        scratch_shapes=[pltpu.VMEM((n_slabs, rows, LANES), F32), pltpu.VMEM((rows, state_lanes), F32),
                        pltpu.VMEM((n_slabs, rows, LANES), F32), pltpu.VMEM((SSM_SEQS, state_lanes), F32)],
        compiler_params=_cparams("arbitrary"),
        name="s5_scan",
    )(hb, w_in, w_in, wb, wc, a_r, a_i, d_rows)


def _mix_kernel(ya_ref, ys_ref, gate_ref, h_ref, wglu_ref, wau_ref, wsu_ref, wout_ref, g_ref, b_ref,
                hf_ref, hb_ref, *, alpha, n_batch):
    tm, d = h_ref.shape
    y = ys_ref[...]
    glu = jnp.dot(y, wglu_ref[...], preferred_element_type=F32)
    y_ssm = (y.astype(F32) * _sigmoid(glu)).astype(BF16)
    up_a = jnp.dot(ya_ref[...], wau_ref[...], preferred_element_type=F32)
    up_s = jnp.dot(y_ssm, wsu_ref[...], preferred_element_type=F32)
    merged = gate_ref[:, :d].astype(F32) * up_a + gate_ref[:, d:].astype(F32) * up_s
    mixed = jnp.dot(merged.astype(BF16), wout_ref[...], preferred_element_type=F32)
    out = _layer_norm(alpha * h_ref[...] + mixed, g_ref[...], b_ref[...])
    out = jnp.where(_real_row_mask(pl.program_id(0) * tm, tm, n_batch), out, 0.0)
    hf_ref[...] = out
    hb_ref[...] = out.astype(BF16)


MIX_ROWS = 2 * BLOCK


def _mix(y_attn, y_ssm, gates, h, wglu, wau, wsu, wout, g, b, layer, alpha, n_batch):
    t, d = h.shape
    tm = MIX_ROWS
    row = lambda w: pl.BlockSpec((tm, w), lambda i: (i, 0))
    return pl.pallas_call(
        functools.partial(_mix_kernel, alpha=alpha, n_batch=n_batch),
        grid=(t // tm,),
        in_specs=[row(y_attn.shape[1]), row(y_ssm.shape[1]), row(gates.shape[1]), row(d),
                  _layer_spec(wglu.shape, layer), _layer_spec(wau.shape, layer), _layer_spec(wsu.shape, layer),
                  _layer_spec(wout.shape, layer), _layer_spec(g.shape, layer), _layer_spec(b.shape, layer)],
        out_specs=[row(d), row(d)],
        out_shape=[jax.ShapeDtypeStruct((t, d), F32), jax.ShapeDtypeStruct((t, d), BF16)],
        compiler_params=_cparams("parallel"),
        name="mix_ln",
    )(y_attn, y_ssm, gates, h, wglu, wau, wsu, wout, g, b)


def _mlp_kernel(xb_ref, xf_ref, wup_ref, wdn_ref, g_ref, b_ref, *refs, alpha, n_batch, final, cast_next):
    if cast_next:
        (win_f, scale_ref, wup_f, wdn_f), refs = refs[:4], refs[4:]
        out_refs, (win_o, wup_o, wdn_o), acc_ref = refs[:-4], refs[-4:-1], refs[-1]
    else:
        out_refs, acc_ref = refs[:-1], refs[-1]
    i, j = pl.program_id(0), pl.program_id(1)
    tm, d = xb_ref.shape

    @pl.when(j == 0)
    def _():
        acc_ref[...] = jnp.zeros_like(acc_ref)

    a = jnp.maximum(jnp.dot(xb_ref[...], wup_ref[...], preferred_element_type=F32), 0.0)
    acc_ref[...] += jnp.dot((a * a).astype(BF16), wdn_ref[...], preferred_element_type=F32)
    if cast_next:
        win_o[...] = (win_f[...] * scale_ref[...]).astype(win_o.dtype)
        wup_o[...] = wup_f[...].astype(wup_o.dtype)
        wdn_o[...] = wdn_f[...].astype(wdn_o.dtype)

    @pl.when(j == pl.num_programs(1) - 1)
    def _():
        out = _layer_norm(alpha * xf_ref[...] + acc_ref[...], g_ref[...], b_ref[...])
        if final:
            out_refs[0][...] = out.reshape(n_batch, BLOCK, d)
        else:
            out = jnp.where(_real_row_mask(i * tm, tm, n_batch), out, 0.0)
            out_refs[0][...] = out
            out_refs[1][...] = out.astype(BF16)


CAST_BLOCK_BYTES = 4 * 1024 * 1024


def _cast_kernel(x_ref, o_ref):
    o_ref[...] = x_ref[...].astype(o_ref.dtype)


def _cast_layer(w, layer):
    _, r, c = w.shape
    rb = max(n for n in range(2 * SUBLANES, r + 1, 2 * SUBLANES)
             if r % n == 0 and n * c * w.dtype.itemsize <= CAST_BLOCK_BYTES)
    return pl.pallas_call(
        _cast_kernel,
        grid=(r // rb,),
        in_specs=[pl.BlockSpec((None, rb, c), lambda i: (layer, i, 0))],
        out_specs=pl.BlockSpec((None, rb, c), lambda i: (0, i, 0)),
        out_shape=jax.ShapeDtypeStruct((1, r, c), BF16),
        compiler_params=_cparams("parallel"),
        name="cast_weight",
    )(w)


def _even_parts(total, max_parts, align):
    return max(p for p in range(1, max_parts + 1) if total % p == 0 and (total // p) % align == 0)


def _mlp(hb, hf, wup, wdn, w_layer, g, b, layer, alpha, n_batch, final, cast_next=None):
    t, d = hf.shape
    dff = wup.shape[2]
    tm = n_batch * BLOCK
    tf = 1024
    first = 1 if final else 0
    n_i, n_j = t // tm - first, dff // tf
    row = lambda: pl.BlockSpec((tm, d), lambda i, j: (i + first, 0))
    if final:
        out_specs = [pl.BlockSpec((n_batch, BLOCK, d), lambda i, j: (0, i, 0))]
        out_shape = [jax.ShapeDtypeStruct((n_batch, t // n_batch - BLOCK, d), F32)]
    else:
        out_specs = [row(), row()]
        out_shape = [jax.ShapeDtypeStruct((t, d), F32), jax.ShapeDtypeStruct((t, d), BF16)]
    in_specs = [row(), row(),
                pl.BlockSpec((None, d, tf), lambda i, j: (w_layer, 0, j)),
                pl.BlockSpec((None, tf, d), lambda i, j: (w_layer, j, 0)),
                _layer_spec(g.shape, layer), _layer_spec(b.shape, layer)]
    args = [hb, hf, wup, wdn, g, b]
    if cast_next is not None:
        w_in_f, scale, w_up_f, w_dn_f, nxt = cast_next
        def blocks(shape):
            r, c = shape[1], shape[2]
            p_c = _even_parts(c, n_j, LANES)
            p_r = _even_parts(r, (n_i * n_j) // p_c, 2 * SUBLANES)

            def idx(i, j):
                s = jnp.minimum(i * n_j + j, p_r * p_c - 1)
                return s // p_c, s % p_c
            return (r // p_r, c // p_c), idx

        for w_f in (w_in_f, w_up_f, w_dn_f):
            blk, idx = blocks(w_f.shape)
            in_specs.append(pl.BlockSpec((None,) + blk, lambda i, j, idx=idx: (nxt,) + idx(i, j)))
            args.append(w_f)
            out_specs.append(pl.BlockSpec((None,) + blk, lambda i, j, idx=idx: (0,) + idx(i, j)))
            out_shape.append(jax.ShapeDtypeStruct((1,) + w_f.shape[1:], BF16))
            if w_f is w_in_f:
                in_specs.append(pl.BlockSpec((None, 1, blk[1]), lambda i, j, idx=idx: (0, 0, idx(i, j)[1])))
                args.append(scale)
    return pl.pallas_call(
        functools.partial(_mlp_kernel, alpha=alpha, n_batch=n_batch, final=final, cast_next=cast_next is not None),
        grid=(n_i, n_j),
        in_specs=in_specs,
        out_specs=out_specs,
        out_shape=out_shape,
        scratch_shapes=[pltpu.VMEM((tm, d), F32)],
        compiler_params=_cparams("parallel", "arbitrary"),
        name="mlp_ln",
    )(*args)


def kernel(x, meta_tokens, ln_emb_g, ln_emb_b, rel_bias, in_proj, gate_b, attn_sinks, ssm_lambda_re,
           ssm_lambda_im, ssm_log_step, ssm_b_re, ssm_b_im, ssm_c_re, ssm_c_im, ssm_d, ssm_w_glu, w_attn_up,
           w_ssm_up, w_out, ln_mix_g, ln_mix_b, w_mlp_up, w_mlp_down, ln_mlp_g, ln_mlp_b):
    bsz, seq, d = x.shape
    depth = in_proj.shape[0]
    assert seq % BLOCK == 0 and 2 * bsz == SSM_SEQS and meta_tokens.shape[0] == N_META
    nb = seq // BLOCK + 1
    alpha = (2 * depth) ** 0.25
    ssm_w = ssm_w_glu.shape[2]
    row2 = lambda a: a.astype(F32).reshape(1, -1)
    rows = lambda a: a.astype(F32)[:, None, :]
    bf = lambda a: a.astype(BF16)

    meta_pad = jnp.concatenate([jnp.zeros((META_PAD, d), x.dtype), meta_tokens.astype(x.dtype)], axis=0)
    hf, hb = _embed_ln(x, meta_pad, row2(ln_emb_g), row2(ln_emb_b))
    bias_tables = _bias_tables(rel_bias)

    q_scale = jnp.where(jnp.arange(in_proj.shape[2]) < ATTN_WIDTH, HEAD_DIM ** -0.5 * LOG2_E, 1.0)
    q_scale = q_scale.astype(in_proj.dtype)[None, None, :]
    w_glu, w_au, w_su, w_o = bf(ssm_w_glu), bf(w_attn_up), bf(w_ssm_up), bf(w_out)
    sinks = attn_sinks.astype(F32) * LOG2_E
    ar, ai, bbr, bbi = _s5_discretize(ssm_lambda_re, ssm_lambda_im, ssm_log_step, ssm_b_re, ssm_b_im)
    wb, wc, a_r, a_i, d_rows = _s5_scan_weights(ar, ai, bbr, bbi, ssm_c_re, ssm_c_im, ssm_d)

    w_in, w_up, w_dn = bf(in_proj[:1] * q_scale), _cast_layer(w_mlp_up, 0), _cast_layer(w_mlp_down, 0)

    for l in range(depth):
        final = l == depth - 1
        qkv, gates = _in_proj(hb, w_in, 0, rows(gate_b), l, bsz * BLOCK)
        y_attn = _attention(qkv, bias_tables, sinks, l, bsz, nb)
        y_gelu = _s5_scan(hb, w_in, 0, wb, wc, a_r, a_i, d_rows, l, bsz)
        hf, hb = _mix(y_attn, y_gelu, gates, hf, w_glu, w_au, w_su, w_o, rows(ln_mix_g), rows(ln_mix_b),
                      l, alpha, bsz)
        cast_next = None if final else (in_proj, q_scale, w_mlp_up, w_mlp_down, l + 1)
        out = _mlp(hb, hf, w_up, w_dn, 0, rows(ln_mlp_g), rows(ln_mlp_b), l, alpha, bsz, final, cast_next)
        if not final:
            hf, hb, w_in, w_up, w_dn = out

    return out[0]
```

```python
import functools
import math

import jax
import jax.numpy as jnp
from jax import lax
from jax.experimental import pallas as pl
from jax.experimental.pallas import tpu as pltpu

F32 = jnp.float32
BF16 = jnp.bfloat16

N_META = 16
BLOCK = 128
META_PAD = BLOCK - N_META
HEAD_DIM = 64
N_Q_HEADS = 16
N_KV_HEADS = 4
Q_PER_KV = N_Q_HEADS // N_KV_HEADS
ATTN_WIDTH = N_Q_HEADS * HEAD_DIM
KV_WIDTH = N_KV_HEADS * HEAD_DIM
QKV_WIDTH = ATTN_WIDTH + 2 * KV_WIDTH
SSM_GROUP = 16
SSM_STATE = 64
N_BUCKETS = 32
MAX_DISTANCE = 128
LN_EPS = 1e-5
NEG_INF = -1e30
LOG2_E = math.log2(math.e)

LANES = 128
SUBLANES = 8
VMEM_LIMIT_BYTES = 56 * 1024 * 1024

SSM_SEQS = SUBLANES
STATES_PER_CHUNK = LANES
GROUPS_PER_CHUNK = STATES_PER_CHUNK // SSM_STATE
CHUNKS_PER_SLAB = 4
SLAB_STATE_LANES = CHUNKS_PER_SLAB * 2 * STATES_PER_CHUNK


def _cparams(*sem):
    return pltpu.CompilerParams(dimension_semantics=sem, vmem_limit_bytes=VMEM_LIMIT_BYTES)


def _resident(shape):
    nd = len(shape)
    return pl.BlockSpec(shape, lambda *_: (0,) * nd, pipeline_mode=pl.Buffered(1))


def _layer_spec(shape, layer):
    nd = len(shape)
    return pl.BlockSpec((None,) + tuple(shape[1:]), lambda *_: (layer,) + (0,) * (nd - 1),
                        pipeline_mode=pl.Buffered(1))


def _even_parts(total, max_parts, align):
    return max(p for p in range(1, max_parts + 1) if total % p == 0 and (total // p) % align == 0)


def _cast_job(w, layer, n_steps, scale=None):
    _, r, c = w.shape
    parts = _even_parts(r, n_steps, 2 * SUBLANES)
    blk = (None, r // parts, c)
    row = lambda i: jnp.minimum(i, parts - 1)
    in_specs, args = [pl.BlockSpec(blk, lambda i: (layer, row(i), 0))], [w]
    if scale is not None:
        in_specs.append(_resident(scale.shape))
        args.append(scale)
    return dict(in_specs=in_specs, args=args, has_scale=scale is not None,
                out_spec=pl.BlockSpec(blk, lambda i: (0, row(i), 0)),
                out_shape=jax.ShapeDtypeStruct((1, r, c), BF16))


def _split_cast_refs(refs, casts, n_out):
    n_in = sum(2 if s else 1 for s in casts)
    cast_in, rest = refs[:n_in], refs[n_in:]
    outs, cast_out, scratch = rest[:n_out], rest[n_out:n_out + len(casts)], rest[n_out + len(casts):]
    jobs, k = [], 0
    for has_scale, dst in zip(casts, cast_out):
        jobs.append((cast_in[k], cast_in[k + 1] if has_scale else None, dst))
        k += 2 if has_scale else 1
    return outs, jobs, scratch


def _run_casts(jobs):
    for src, scale, dst in jobs:
        v = src[...] if scale is None else src[...] * scale[...]
        dst[...] = v.astype(dst.dtype)


def _layer_norm(x, g, b):
    mu = jnp.mean(x, axis=-1, keepdims=True)
    xc = x - mu
    var = jnp.mean(xc * xc, axis=-1, keepdims=True)
    return xc * lax.rsqrt(var + LN_EPS) * g + b


def _sigmoid(z):
    return 1.0 / (1.0 + jnp.exp(-z))


def _real_row_mask(first_row, n_rows, n_batch):
    r = first_row + lax.broadcasted_iota(jnp.int32, (n_rows, 1), 0)
    is_pad = (r < n_batch * BLOCK) & ((r & (BLOCK - 1)) < META_PAD)
    return jnp.logical_not(is_pad)


def _embed_ln_kernel(meta_ref, x_ref, g_ref, b_ref, hf_ref, hb_ref):
    is_meta = pl.program_id(0) == 0
    rows = lax.broadcasted_iota(jnp.int32, (BLOCK, 1), 0)
    keep = jnp.logical_or(jnp.logical_not(is_meta), rows >= META_PAD)
    for bi in range(x_ref.shape[0]):
        xin = jnp.where(is_meta, meta_ref[...], x_ref[bi])
        y = jnp.where(keep, _layer_norm(xin, g_ref[...], b_ref[...]), 0.0)
        hf_ref[bi * BLOCK:(bi + 1) * BLOCK, :] = y
        hb_ref[bi * BLOCK:(bi + 1) * BLOCK, :] = y.astype(BF16)


def _embed_ln(x, meta_pad, g, b):
    bsz, seq, d = x.shape
    nb = seq // BLOCK + 1
    t = bsz * nb * BLOCK
    tile = pl.BlockSpec((bsz * BLOCK, d), lambda n: (n, 0))
    return pl.pallas_call(
        _embed_ln_kernel,
        grid=(nb,),
        in_specs=[
            _resident(meta_pad.shape),
            pl.BlockSpec((bsz, BLOCK, d), lambda n: (0, jnp.maximum(n - 1, 0), 0)),
            _resident(g.shape), _resident(b.shape),
        ],
        out_specs=[tile, tile],
        out_shape=[jax.ShapeDtypeStruct((t, d), F32), jax.ShapeDtypeStruct((t, d), BF16)],
        compiler_params=_cparams("parallel"),
        name="embed_ln",
    )(meta_pad, x, g, b)


GATE_COL_CHUNK = 1024


def _in_proj_kernel(x_ref, w_ref, gb_ref, qkv_ref, gate_ref):
    x = x_ref[...]
    qkv_ref[...] = jnp.dot(x, w_ref[:, :QKV_WIDTH], preferred_element_type=F32).astype(qkv_ref.dtype)
    gate0 = w_ref.shape[1] - gate_ref.shape[1]
    for c in range(0, gate_ref.shape[1], GATE_COL_CHUNK):
        z = jnp.dot(x, w_ref[:, gate0 + c:gate0 + c + GATE_COL_CHUNK], preferred_element_type=F32)
        z = z + gb_ref[:, c:c + GATE_COL_CHUNK]
        gate_ref[:, c:c + GATE_COL_CHUNK] = _sigmoid(z).astype(gate_ref.dtype)


def _in_proj(hb, w_in, w_layer, gate_bias, layer, tm):
    t, d = hb.shape
    gate_w = gate_bias.shape[2]
    row = lambda w: pl.BlockSpec((tm, w), lambda i: (i, 0))
    return pl.pallas_call(
        _in_proj_kernel,
        grid=(t // tm,),
        in_specs=[row(d), _layer_spec(w_in.shape, w_layer), _layer_spec(gate_bias.shape, layer)],
        out_specs=[row(QKV_WIDTH), row(gate_w)],
        out_shape=[jax.ShapeDtypeStruct((t, QKV_WIDTH), BF16), jax.ShapeDtypeStruct((t, gate_w), BF16)],
        compiler_params=_cparams("parallel"),
        name="in_proj",
    )(hb, w_in, gate_bias)


def _attn_kernel(sink_ref, q_ref, kvm_ref, kvp_ref, kvc_ref, bias_ref, o_ref, *, layer):
    kv_m = kvm_ref[...]
    kv_w = jnp.concatenate([kvp_ref[...], kvc_ref[...]], axis=0)
    rows = Q_PER_KV * BLOCK
    low = lax.broadcasted_iota(jnp.int32, (1, LANES), 1) < HEAD_DIM
    head_of_row = lax.broadcasted_iota(jnp.int32, (rows, 1), 0) // BLOCK
    zero = jnp.zeros((), q_ref.dtype)
    contract_last = (((1,), (1,)), ((), ()))

    def dup_keys(kv, kh):
        k = kv[:, kh * HEAD_DIM:(kh + 1) * HEAD_DIM]
        return jnp.concatenate([k, k], axis=1)

    def dup_values(kv, kh):
        v = kv[:, KV_WIDTH + kh * HEAD_DIM:KV_WIDTH + (kh + 1) * HEAD_DIM]
        return jnp.concatenate([v, v, jnp.ones((kv.shape[0], LANES), kv.dtype)], axis=1)

    def scores(kh):
        cols = [q_ref[:, (kh * Q_PER_KV + 2 * c) * HEAD_DIM:(kh * Q_PER_KV + 2 * c + 2) * HEAD_DIM]
                for c in range(Q_PER_KV // 2)]
        qs = jnp.concatenate([jnp.where(low, cols[g // 2], zero) if g % 2 == 0 else jnp.where(low, zero, cols[g // 2])
                              for g in range(Q_PER_KV)], axis=0)
        s_m = lax.dot_general(qs, dup_keys(kv_m, kh), contract_last, preferred_element_type=F32)
        s_w = lax.dot_general(qs, dup_keys(kv_w, kh), contract_last, preferred_element_type=F32)
        return s_m + bias_ref[kh, :, :BLOCK], s_w + bias_ref[kh, :, BLOCK:]

    def finish(kh, s_m, s_w):
        sink = jnp.zeros((rows, 1), F32)
        for g in range(Q_PER_KV):
            sink = jnp.where(head_of_row == g, sink_ref[layer, kh * Q_PER_KV + g], sink)
        m = jnp.maximum(jnp.maximum(jnp.max(s_m, axis=-1, keepdims=True), jnp.max(s_w, axis=-1, keepdims=True)), sink)
        p_m = jnp.exp2(s_m - m).astype(BF16)
        p_w = jnp.exp2(s_w - m).astype(BF16)
        pv = (jnp.dot(p_m, dup_values(kv_m, kh), preferred_element_type=F32)
              + jnp.dot(p_w, dup_values(kv_w, kh), preferred_element_type=F32))
        denom = pv[:, 2 * HEAD_DIM:] + jnp.exp2(sink - m)
        o = pv[:, :2 * HEAD_DIM] * (1.0 / denom)
        for c in range(Q_PER_KV // 2):
            pair = jnp.where(low, o[2 * c * BLOCK:(2 * c + 1) * BLOCK], o[(2 * c + 1) * BLOCK:(2 * c + 2) * BLOCK])
            o_ref[:, (kh * Q_PER_KV + 2 * c) * HEAD_DIM:(kh * Q_PER_KV + 2 * c + 2) * HEAD_DIM] = pair.astype(o_ref.dtype)

    pending = scores(0)
    for kh in range(N_KV_HEADS):
        ahead = scores(kh + 1) if kh + 1 < N_KV_HEADS else None
        finish(kh, *pending)
        pending = ahead


def _attention(qkv, bias_tables, sinks, layer, bsz, nb):
    t = qkv.shape[0]
    kv_col = ATTN_WIDTH // (2 * KV_WIDTH)
    kv_spec = lambda row_fn: pl.BlockSpec((BLOCK, 2 * KV_WIDTH), lambda n, b: (row_fn(n, b), kv_col))
    return pl.pallas_call(
        functools.partial(_attn_kernel, layer=layer),
        grid=(nb, bsz),
        in_specs=[
            pl.BlockSpec(memory_space=pltpu.SMEM),
            pl.BlockSpec((BLOCK, ATTN_WIDTH), lambda n, b: (n * bsz + b, 0)),
            kv_spec(lambda n, b: b),
            kv_spec(lambda n, b: jnp.maximum(n - 1, 0) * bsz + b),
            kv_spec(lambda n, b: n * bsz + b),
            pl.BlockSpec((None, N_KV_HEADS, Q_PER_KV * BLOCK, 3 * BLOCK),
                         lambda n, b: (jnp.minimum(n, 2), 0, 0, 0)),
        ],
        out_specs=pl.BlockSpec((BLOCK, ATTN_WIDTH), lambda n, b: (n * bsz + b, 0)),
        out_shape=jax.ShapeDtypeStruct((t, ATTN_WIDTH), BF16),
        compiler_params=_cparams("parallel", "parallel"),
        name="swa",
    )(sinks, qkv, qkv, qkv, qkv, bias_tables)


def _t5_bucket(dist):
    n = jnp.maximum(dist, 0)
    max_exact = N_BUCKETS // 2
    nf = jnp.maximum(n, 1).astype(F32)
    large = max_exact + (jnp.log(nf / max_exact) / math.log(MAX_DISTANCE / max_exact)
                         * (N_BUCKETS - max_exact)).astype(jnp.int32)
    large = jnp.minimum(large, N_BUCKETS - 1)
    return jnp.where(n < max_exact, n, large)


def _bias_tables(rel_bias):
    blk = jnp.arange(3, dtype=jnp.int32)[:, None]
    j = jnp.arange(BLOCK, dtype=jnp.int32)[None, :]
    q_pos = blk * BLOCK + j
    k_pos = jnp.concatenate([jnp.broadcast_to(j, (3, BLOCK)), (blk - 1) * BLOCK + j, q_pos], axis=1)
    dist = q_pos[:, :, None] - k_pos[:, None, :]
    kp = k_pos[:, None, :]
    is_meta_seg = (jnp.arange(3 * BLOCK) < BLOCK)[None, None, :]
    valid = (dist >= 0) & jnp.where(is_meta_seg, kp >= META_PAD, (kp >= BLOCK) & (dist < BLOCK))
    onehot = jax.nn.one_hot(_t5_bucket(dist), N_BUCKETS, dtype=F32)
    bias = jnp.einsum("tqkb,bh->thqk", onehot, rel_bias.astype(F32), precision=lax.Precision.HIGHEST)
    bias = jnp.where(valid[:, None], bias * LOG2_E, NEG_INF)
    return bias.reshape(3, N_KV_HEADS, Q_PER_KV * BLOCK, 3 * BLOCK)


def _s5_discretize_kernel(lr_ref, li_ref, ls_ref, bre_ref, bim_ref, ar_ref, ai_ref, bbr_ref, bbi_ref):
    lr, li = lr_ref[...], li_ref[...]
    dt = jnp.exp(ls_ref[...])
    decay = jnp.exp(lr * dt)
    ar, ai = decay * jnp.cos(li * dt), decay * jnp.sin(li * dt)
    den = lr * lr + li * li
    nr, ni = ar - 1.0, ai
    zr = (nr * lr + ni * li) / den
    zi = (ni * lr - nr * li) / den
    bre, bim = bre_ref[...], bim_ref[...]
    ar_ref[...] = ar
    ai_ref[...] = ai
    bbr_ref[...] = zr * bre - zi * bim
    bbi_ref[...] = zr * bim + zi * bre


def _s5_discretize(lam_re, lam_im, log_step, b_re, b_im):
    depth, g, n, p = b_re.shape
    flat = lambda a: a.astype(F32).reshape(depth * g, n * p)
    rep = lambda a: flat(jnp.broadcast_to(a.astype(F32)[..., None], (depth, g, n, p)))
    ls = flat(jnp.broadcast_to(log_step.astype(F32)[:, :, None, None], (depth, g, n, p)))
    shp = jax.ShapeDtypeStruct((depth * g, n * p), F32)
    ar, ai, bbr, bbi = pl.pallas_call(
        _s5_discretize_kernel, out_shape=[shp] * 4, name="s5_discretize",
    )(rep(lam_re), rep(lam_im), ls, flat(b_re), flat(b_im))
    state = lambda a: a.reshape(depth, g, n, p)[..., 0]
    return state(ar), state(ai), bbr.reshape(depth, g, n, p), bbi.reshape(depth, g, n, p)


def _s5_scan_weights(ar, ai, bbr, bbi, c_re, c_im, d_skip):
    depth, g, n, p = bbr.shape
    halves, q_per, gl_per = 2, CHUNKS_PER_SLAB, GROUPS_PER_CHUNK
    gi_per = q_per * gl_per
    slabs = g // (halves * gi_per)
    n_ch = halves * gi_per * p
    gi_of_ch = (jnp.arange(n_ch) // p) % gi_per
    lane_blocks = [(q, ri, gl) for q in range(q_per) for ri in range(2) for gl in range(gl_per)]

    b_ri = jnp.stack([bbr, bbi], axis=1).reshape(depth, 2, halves, slabs, gi_per, n, p)
    b_rows = b_ri.transpose(0, 3, 1, 2, 4, 6, 5).reshape(depth, slabs, 2, n_ch, n)
    wb = jnp.concatenate([jnp.where((gi_of_ch == gl_per * q + gl)[:, None], b_rows[:, :, ri], 0.0)
                          for q, ri, gl in lane_blocks], axis=-1)

    c_ri = jnp.stack([c_re.astype(F32), -c_im.astype(F32)], axis=1).reshape(depth, 2, halves, slabs, gi_per, p, n)
    c_cols = c_ri.transpose(0, 3, 1, 6, 2, 4, 5).reshape(depth, slabs, 2, n, n_ch)
    wc = jnp.concatenate([jnp.where((gi_of_ch == gl_per * q + gl)[None, :], c_cols[:, :, ri], 0.0)
                          for q, ri, gl in lane_blocks], axis=-2)

    def per_seq(a):
        return jnp.repeat(a.astype(F32).reshape(depth, halves, -1), SSM_SEQS // halves, axis=1)

    return wb.astype(BF16), wc.astype(BF16), per_seq(ar), per_seq(ai), per_seq(d_skip)


def _s5_scan_kernel(x_ref, wu0_ref, wu1_ref, wb_ref, wc_ref, ar_ref, ai_ref, d_ref, *refs, n_batch, unroll, casts):
    (y_ref,), cast_jobs, (u_scr, s_ref, y_scr, h_ref) = _split_cast_refs(refs, casts, 1)
    steps = BLOCK
    rows = steps * SSM_SEQS
    n_slabs = wb_ref.shape[0]
    half_w = n_slabs * LANES

    @pl.when(pl.program_id(0) == 0)
    def _():
        h_ref[...] = jnp.zeros_like(h_ref)

    def seq_rows(b, half):
        return pl.ds(half * n_batch + b, steps, stride=SSM_SEQS)

    def channels(half, j):
        return slice(half * half_w + j * LANES, half * half_w + (j + 1) * LANES)

    x = x_ref[...]
    u_half = [jnp.dot(x, w[...], preferred_element_type=F32) for w in (wu0_ref, wu1_ref)]
    for b in range(n_batch):
        for half in range(2):
            for j in range(n_slabs):
                u_scr[j, seq_rows(b, half), :] = u_half[half][b * steps:(b + 1) * steps, j * LANES:(j + 1) * LANES]

    half0 = (lax.broadcasted_iota(jnp.int32, (rows, 1), 0) & (SSM_SEQS - 1)) < (SSM_SEQS // 2)

    for j in range(n_slabs):
        uj = u_scr[j]
        lhs = jnp.concatenate([jnp.where(half0, uj, 0.0), jnp.where(half0, 0.0, uj)], axis=1).astype(BF16)
        s_ref[:, j * SLAB_STATE_LANES:(j + 1) * SLAB_STATE_LANES] = jnp.dot(
            lhs, wb_ref[j], preferred_element_type=F32)

    n_chunks = n_slabs * CHUNKS_PER_SLAB
    for c0 in range(0, n_chunks, CHUNKS_PER_SLAB):
        chunks = range(c0, c0 + CHUNKS_PER_SLAB)
        a_r = [ar_ref[:, c * LANES:(c + 1) * LANES] for c in chunks]
        a_i = [ai_ref[:, c * LANES:(c + 1) * LANES] for c in chunks]
        init = []
        for c in chunks:
            init += [h_ref[:, 2 * c * LANES:(2 * c + 1) * LANES], h_ref[:, (2 * c + 1) * LANES:(2 * c + 2) * LANES]]

        def body(t, carry, chunks=chunks, a_r=a_r, a_i=a_i):
            r0 = pl.multiple_of(t * SSM_SEQS, SSM_SEQS)
            new = []
            for k, c in enumerate(chunks):
                hr, hi = carry[2 * k], carry[2 * k + 1]
                re_sl = (pl.ds(r0, SSM_SEQS), pl.ds(2 * c * LANES, LANES))
                im_sl = (pl.ds(r0, SSM_SEQS), pl.ds((2 * c + 1) * LANES, LANES))
                nr = a_r[k] * hr - a_i[k] * hi + s_ref[re_sl]
                ni = a_r[k] * hi + a_i[k] * hr + s_ref[im_sl]
                s_ref[re_sl] = nr
                s_ref[im_sl] = ni
                new += [nr, ni]
            return tuple(new)

        final = lax.fori_loop(0, steps, body, tuple(init), unroll=unroll)
        for k, c in enumerate(chunks):
            h_ref[:, 2 * c * LANES:(2 * c + 1) * LANES] = final[2 * k]
            h_ref[:, (2 * c + 1) * LANES:(2 * c + 2) * LANES] = final[2 * k + 1]

    for j in range(n_slabs):
        hs = s_ref[:, j * SLAB_STATE_LANES:(j + 1) * SLAB_STATE_LANES].astype(BF16)
        both = jnp.dot(hs, wc_ref[j], preferred_element_type=F32)
        yj = jnp.where(half0, both[:, :LANES], both[:, LANES:])
        dj = jnp.tile(d_ref[:, j * LANES:(j + 1) * LANES], (steps, 1))
        y_scr[j] = jax.nn.gelu(yj + dj * u_scr[j])

    for b in range(n_batch):
        for half in range(2):
            for j in range(n_slabs):
                y_ref[b * steps:(b + 1) * steps, channels(half, j)] = y_scr[j, seq_rows(b, half), :].astype(y_ref.dtype)
    _run_casts(cast_jobs)


def _s5_scan(hb, w_in, w_layer, wb, wc, a_r, a_i, d_rows, layer, n_batch, cast_jobs=()):
    t, d = hb.shape
    tile = n_batch * BLOCK
    rows = BLOCK * SSM_SEQS
    n_slabs = wb.shape[1]
    state_lanes = n_slabs * SLAB_STATE_LANES
    half_w = n_slabs * LANES
    ssm_w = 2 * half_w
    u_block = QKV_WIDTH // half_w
    assert QKV_WIDTH % half_w == 0
    u_spec = lambda half: pl.BlockSpec((None, d, half_w), lambda i: (w_layer, 0, u_block + half),
                                       pipeline_mode=pl.Buffered(1))
    return pl.pallas_call(
        functools.partial(_s5_scan_kernel, n_batch=n_batch, unroll=True,
                          casts=tuple(job["has_scale"] for job in cast_jobs)),
        grid=(t // tile,),
        in_specs=[
            pl.BlockSpec((tile, d), lambda i: (i, 0)),
            u_spec(0), u_spec(1), _layer_spec(wb.shape, layer), _layer_spec(wc.shape, layer),
            _layer_spec(a_r.shape, layer), _layer_spec(a_i.shape, layer), _layer_spec(d_rows.shape, layer),
        ] + [s for job in cast_jobs for s in job["in_specs"]],
        out_specs=[pl.BlockSpec((tile, ssm_w), lambda i: (i, 0))] + [job["out_spec"] for job in cast_jobs],
        out_shape=[jax.ShapeDtypeStruct((t, ssm_w), BF16)] + [job["out_shape"] for job in cast_jobs],
        scratch_shapes=[pltpu.VMEM((n_slabs, rows, LANES), F32), pltpu.VMEM((rows, state_lanes), F32),
                        pltpu.VMEM((n_slabs, rows, LANES), F32), pltpu.VMEM((SSM_SEQS, state_lanes), F32)],
        compiler_params=_cparams("arbitrary"),
        name="s5_scan",
    )(hb, w_in, w_in, wb, wc, a_r, a_i, d_rows, *[a for job in cast_jobs for a in job["args"]])


def _mix_kernel(ya_ref, ys_ref, gate_ref, h_ref, wglu_ref, wau_ref, wsu_ref, wout_ref, g_ref, b_ref,
                *refs, alpha, n_batch, casts):
    (hf_ref, hb_ref), cast_jobs, _ = _split_cast_refs(refs, casts, 2)
    tm, d = h_ref.shape
    y = ys_ref[...]
    glu = jnp.dot(y, wglu_ref[...], preferred_element_type=F32)
    y_ssm = (y.astype(F32) * _sigmoid(glu)).astype(BF16)
    up_a = jnp.dot(ya_ref[...], wau_ref[...], preferred_element_type=F32)
    up_s = jnp.dot(y_ssm, wsu_ref[...], preferred_element_type=F32)
    merged = gate_ref[:, :d].astype(F32) * up_a + gate_ref[:, d:].astype(F32) * up_s
    mixed = jnp.dot(merged.astype(BF16), wout_ref[...], preferred_element_type=F32)
    out = _layer_norm(alpha * h_ref[...] + mixed, g_ref[...], b_ref[...])
    out = jnp.where(_real_row_mask(pl.program_id(0) * tm, tm, n_batch), out, 0.0)
    hf_ref[...] = out
    hb_ref[...] = out.astype(BF16)
    _run_casts(cast_jobs)


MIX_ROWS = 2 * BLOCK


def _mix(y_attn, y_ssm, gates, h, wglu, wau, wsu, wout, g, b, layer, alpha, n_batch, cast_jobs=()):
    t, d = h.shape
    tm = MIX_ROWS
    row = lambda w: pl.BlockSpec((tm, w), lambda i: (i, 0))
    return pl.pallas_call(
        functools.partial(_mix_kernel, alpha=alpha, n_batch=n_batch,
                          casts=tuple(job["has_scale"] for job in cast_jobs)),
        grid=(t // tm,),
        in_specs=[row(y_attn.shape[1]), row(y_ssm.shape[1]), row(gates.shape[1]), row(d),
                  _layer_spec(wglu.shape, layer), _layer_spec(wau.shape, layer), _layer_spec(wsu.shape, layer),
                  _layer_spec(wout.shape, layer), _layer_spec(g.shape, layer), _layer_spec(b.shape, layer)]
                 + [s for job in cast_jobs for s in job["in_specs"]],
        out_specs=[row(d), row(d)] + [job["out_spec"] for job in cast_jobs],
        out_shape=[jax.ShapeDtypeStruct((t, d), F32), jax.ShapeDtypeStruct((t, d), BF16)]
                  + [job["out_shape"] for job in cast_jobs],
        compiler_params=_cparams("arbitrary"),
        name="mix_ln",
    )(y_attn, y_ssm, gates, h, wglu, wau, wsu, wout, g, b, *[a for job in cast_jobs for a in job["args"]])


def _mlp_kernel(xb_ref, xf_ref, wup_ref, wdn_ref, g_ref, b_ref, *refs, alpha, n_batch, final):
    out_refs, acc_ref = refs[:-1], refs[-1]
    i, j = pl.program_id(0), pl.program_id(1)
    tm, d = xb_ref.shape

    @pl.when(j == 0)
    def _():
        acc_ref[...] = alpha * xf_ref[...]

    a = jnp.maximum(jnp.dot(xb_ref[...], wup_ref[...], preferred_element_type=F32), 0.0)
    acc_ref[...] += jnp.dot((a * a).astype(BF16), wdn_ref[...], preferred_element_type=F32)

    @pl.when(j == pl.num_programs(1) - 1)
    def _():
        out = _layer_norm(acc_ref[...], g_ref[...], b_ref[...])
        if final:
            out_refs[0][...] = out.reshape(n_batch, BLOCK, d)
        else:
            out = jnp.where(_real_row_mask(i * tm, tm, n_batch), out, 0.0)
            out_refs[0][...] = out
            out_refs[1][...] = out.astype(BF16)


def _mlp(hb, hf, wup, wdn, g, b, layer, alpha, n_batch, final):
    t, d = hf.shape
    dff = wup.shape[2]
    tm = n_batch * BLOCK
    tf = 1024
    first = 1 if final else 0
    row = lambda: pl.BlockSpec((tm, d), lambda i, j: (i + first, 0))
    if final:
        out_specs = [pl.BlockSpec((n_batch, BLOCK, d), lambda i, j: (0, i, 0))]
        out_shape = [jax.ShapeDtypeStruct((n_batch, t // n_batch - BLOCK, d), F32)]
    else:
        out_specs = [row(), row()]
        out_shape = [jax.ShapeDtypeStruct((t, d), F32), jax.ShapeDtypeStruct((t, d), BF16)]
    return pl.pallas_call(
        functools.partial(_mlp_kernel, alpha=alpha, n_batch=n_batch, final=final),
        grid=(t // tm - first, dff // tf),
        in_specs=[row(), row(),
                  pl.BlockSpec((None, d, tf), lambda i, j: (0, 0, j)),
                  pl.BlockSpec((None, tf, d), lambda i, j: (0, j, 0)),
                  _layer_spec(g.shape, layer), _layer_spec(b.shape, layer)],
        out_specs=out_specs,
        out_shape=out_shape,
        scratch_shapes=[pltpu.VMEM((tm, d), F32)],
        compiler_params=_cparams("parallel", "arbitrary"),
        name="mlp_ln",
    )(hb, hf, wup, wdn, g, b)


def kernel(x, meta_tokens, ln_emb_g, ln_emb_b, rel_bias, in_proj, gate_b, attn_sinks, ssm_lambda_re,
           ssm_lambda_im, ssm_log_step, ssm_b_re, ssm_b_im, ssm_c_re, ssm_c_im, ssm_d, ssm_w_glu, w_attn_up,
           w_ssm_up, w_out, ln_mix_g, ln_mix_b, w_mlp_up, w_mlp_down, ln_mlp_g, ln_mlp_b):
    bsz, seq, d = x.shape
    depth = in_proj.shape[0]
    assert seq % BLOCK == 0 and 2 * bsz == SSM_SEQS and meta_tokens.shape[0] == N_META
    nb = seq // BLOCK + 1
    alpha = (2 * depth) ** 0.25
    row2 = lambda a: a.astype(F32).reshape(1, -1)
    rows = lambda a: a.astype(F32)[:, None, :]
    bf = lambda a: a.astype(BF16)

    meta_pad = jnp.concatenate([jnp.zeros((META_PAD, d), x.dtype), meta_tokens.astype(x.dtype)], axis=0)
    hf, hb = _embed_ln(x, meta_pad, row2(ln_emb_g), row2(ln_emb_b))
    bias_tables = _bias_tables(rel_bias)

    q_scale = jnp.where(jnp.arange(in_proj.shape[2]) < ATTN_WIDTH, HEAD_DIM ** -0.5 * LOG2_E, 1.0)
    q_scale = q_scale.astype(in_proj.dtype)[None, :]
    w_glu, w_au, w_su, w_o = bf(ssm_w_glu), bf(w_attn_up), bf(w_ssm_up), bf(w_out)
    sinks = attn_sinks.astype(F32) * LOG2_E
    ar, ai, bbr, bbi = _s5_discretize(ssm_lambda_re, ssm_lambda_im, ssm_log_step, ssm_b_re, ssm_b_im)
    wb, wc, a_r, a_i, d_rows = _s5_scan_weights(ar, ai, bbr, bbi, ssm_c_re, ssm_c_im, ssm_d)

    t = hf.shape[0]
    scan_steps, mix_steps = t // (bsz * BLOCK), t // MIX_ROWS
    w_in = bf(in_proj[:1] * q_scale)

    for l in range(depth):
        final = l == depth - 1
        qkv, gates = _in_proj(hb, w_in, 0, rows(gate_b), l, bsz * BLOCK)
        y_attn = _attention(qkv, bias_tables, sinks, l, bsz, nb)
        y_gelu, w_up = _s5_scan(hb, w_in, 0, wb, wc, a_r, a_i, d_rows, l, bsz,
                                cast_jobs=[_cast_job(w_mlp_up, l, scan_steps)])
        mix_jobs = [_cast_job(w_mlp_down, l, mix_steps)]
        if not final:
            mix_jobs.append(_cast_job(in_proj, l + 1, mix_steps, scale=q_scale))
        hf, hb, w_dn, *w_next = _mix(y_attn, y_gelu, gates, hf, w_glu, w_au, w_su, w_o, rows(ln_mix_g),
                                     rows(ln_mix_b), l, alpha, bsz, cast_jobs=mix_jobs)
        out = _mlp(hb, hf, w_up, w_dn, rows(ln_mlp_g), rows(ln_mlp_b), l, alpha, bsz, final)
        if not final:
            (hf, hb), (w_in,) = out, w_next

    return out[0]
```

```python
import functools
import math

import jax
import jax.numpy as jnp
from jax import lax
from jax.experimental import pallas as pl
from jax.experimental.pallas import tpu as pltpu

F32 = jnp.float32
BF16 = jnp.bfloat16

N_META = 16
BLOCK = 128
META_PAD = BLOCK - N_META
HEAD_DIM = 64
N_Q_HEADS = 16
N_KV_HEADS = 4
Q_PER_KV = N_Q_HEADS // N_KV_HEADS
ATTN_WIDTH = N_Q_HEADS * HEAD_DIM
KV_WIDTH = N_KV_HEADS * HEAD_DIM
QKV_WIDTH = ATTN_WIDTH + 2 * KV_WIDTH
SSM_GROUP = 16
SSM_STATE = 64
N_BUCKETS = 32
MAX_DISTANCE = 128
LN_EPS = 1e-5
NEG_INF = -1e30
LOG2_E = math.log2(math.e)

LANES = 128
SUBLANES = 8
VMEM_LIMIT_BYTES = 56 * 1024 * 1024

SSM_SEQS = SUBLANES
STATES_PER_CHUNK = LANES
GROUPS_PER_CHUNK = STATES_PER_CHUNK // SSM_STATE
CHUNKS_PER_SLAB = 4
SLAB_STATE_LANES = CHUNKS_PER_SLAB * 2 * STATES_PER_CHUNK


def _cparams(*sem):
    return pltpu.CompilerParams(dimension_semantics=sem, vmem_limit_bytes=VMEM_LIMIT_BYTES)


def _resident(shape):
    nd = len(shape)
    return pl.BlockSpec(shape, lambda *_: (0,) * nd, pipeline_mode=pl.Buffered(1))


def _layer_spec(shape, layer):
    nd = len(shape)
    return pl.BlockSpec((None,) + tuple(shape[1:]), lambda *_: (layer,) + (0,) * (nd - 1),
                        pipeline_mode=pl.Buffered(1))


def _even_parts(total, max_parts, align):
    return max(p for p in range(1, max_parts + 1) if total % p == 0 and (total // p) % align == 0)


def _cast_job(w, layer, n_steps, scale=None):
    _, r, c = w.shape
    parts = _even_parts(r, n_steps, 2 * SUBLANES)
    blk = (None, r // parts, c)
    row = lambda i: jnp.minimum(i, parts - 1)
    in_specs, args = [pl.BlockSpec(blk, lambda i: (layer, row(i), 0))], [w]
    if scale is not None:
        in_specs.append(_resident(scale.shape))
        args.append(scale)
    return dict(in_specs=in_specs, args=args, has_scale=scale is not None,
                out_spec=pl.BlockSpec(blk, lambda i: (0, row(i), 0)),
                out_shape=jax.ShapeDtypeStruct((1, r, c), BF16))


def _split_cast_refs(refs, casts, n_out):
    n_in = sum(2 if s else 1 for s in casts)
    cast_in, rest = refs[:n_in], refs[n_in:]
    outs, cast_out, scratch = rest[:n_out], rest[n_out:n_out + len(casts)], rest[n_out + len(casts):]
    jobs, k = [], 0
    for has_scale, dst in zip(casts, cast_out):
        jobs.append((cast_in[k], cast_in[k + 1] if has_scale else None, dst))
        k += 2 if has_scale else 1
    return outs, jobs, scratch


def _run_casts(jobs):
    for src, scale, dst in jobs:
        v = src[...] if scale is None else src[...] * scale[...]
        dst[...] = v.astype(dst.dtype)


def _layer_norm(x, g, b):
    mu = jnp.mean(x, axis=-1, keepdims=True)
    xc = x - mu
    var = jnp.mean(xc * xc, axis=-1, keepdims=True)
    return xc * lax.rsqrt(var + LN_EPS) * g + b


def _sigmoid(z):
    return 1.0 / (1.0 + jnp.exp(-z))


def _real_row_mask(first_row, n_rows, n_batch):
    r = first_row + lax.broadcasted_iota(jnp.int32, (n_rows, 1), 0)
    is_pad = (r < n_batch * BLOCK) & ((r & (BLOCK - 1)) < META_PAD)
    return jnp.logical_not(is_pad)


def _embed_ln_kernel(meta_ref, x_ref, g_ref, b_ref, hf_ref, hb_ref):
    is_meta = pl.program_id(0) == 0
    rows = lax.broadcasted_iota(jnp.int32, (BLOCK, 1), 0)
    keep = jnp.logical_or(jnp.logical_not(is_meta), rows >= META_PAD)
    for bi in range(x_ref.shape[0]):
        xin = jnp.where(is_meta, meta_ref[...], x_ref[bi])
        y = jnp.where(keep, _layer_norm(xin, g_ref[...], b_ref[...]), 0.0)
        hf_ref[bi * BLOCK:(bi + 1) * BLOCK, :] = y
        hb_ref[bi * BLOCK:(bi + 1) * BLOCK, :] = y.astype(BF16)


def _embed_ln(x, meta_pad, g, b):
    bsz, seq, d = x.shape
    nb = seq // BLOCK + 1
    t = bsz * nb * BLOCK
    tile = pl.BlockSpec((bsz * BLOCK, d), lambda n: (n, 0))
    return pl.pallas_call(
        _embed_ln_kernel,
        grid=(nb,),
        in_specs=[
            _resident(meta_pad.shape),
            pl.BlockSpec((bsz, BLOCK, d), lambda n: (0, jnp.maximum(n - 1, 0), 0)),
            _resident(g.shape), _resident(b.shape),
        ],
        out_specs=[tile, tile],
        out_shape=[jax.ShapeDtypeStruct((t, d), F32), jax.ShapeDtypeStruct((t, d), BF16)],
        compiler_params=_cparams("parallel"),
        name="embed_ln",
    )(meta_pad, x, g, b)


GATE_COL_CHUNK = 1024


def _in_proj_kernel(x_ref, w_ref, gb_ref, qkv_ref, gate_ref):
    x = x_ref[...]
    qkv_ref[...] = jnp.dot(x, w_ref[:, :QKV_WIDTH], preferred_element_type=F32).astype(qkv_ref.dtype)
    gate0 = w_ref.shape[1] - gate_ref.shape[1]
    for c in range(0, gate_ref.shape[1], GATE_COL_CHUNK):
        z = jnp.dot(x, w_ref[:, gate0 + c:gate0 + c + GATE_COL_CHUNK], preferred_element_type=F32)
        z = z + gb_ref[:, c:c + GATE_COL_CHUNK]
        gate_ref[:, c:c + GATE_COL_CHUNK] = _sigmoid(z).astype(gate_ref.dtype)


def _in_proj(hb, w_in, w_layer, gate_bias, layer, tm):
    t, d = hb.shape
    gate_w = gate_bias.shape[2]
    row = lambda w: pl.BlockSpec((tm, w), lambda i: (i, 0))
    return pl.pallas_call(
        _in_proj_kernel,
        grid=(t // tm,),
        in_specs=[row(d), _layer_spec(w_in.shape, w_layer), _layer_spec(gate_bias.shape, layer)],
        out_specs=[row(QKV_WIDTH), row(gate_w)],
        out_shape=[jax.ShapeDtypeStruct((t, QKV_WIDTH), BF16), jax.ShapeDtypeStruct((t, gate_w), BF16)],
        compiler_params=_cparams("parallel"),
        name="in_proj",
    )(hb, w_in, gate_bias)


def _attn_kernel(sink_ref, q_ref, kvm_ref, kvp_ref, kvc_ref, bias_ref, o_ref, *, layer):
    kv_m = kvm_ref[...]
    kv_w = jnp.concatenate([kvp_ref[...], kvc_ref[...]], axis=0)
    rows = Q_PER_KV * BLOCK
    low = lax.broadcasted_iota(jnp.int32, (1, LANES), 1) < HEAD_DIM
    head_of_row = lax.broadcasted_iota(jnp.int32, (rows, 1), 0) // BLOCK
    zero = jnp.zeros((), q_ref.dtype)
    contract_last = (((1,), (1,)), ((), ()))

    def dup_keys(kv, kh):
        k = kv[:, kh * HEAD_DIM:(kh + 1) * HEAD_DIM]
        return jnp.concatenate([k, k], axis=1)

    def dup_values(kv, kh):
        v = kv[:, KV_WIDTH + kh * HEAD_DIM:KV_WIDTH + (kh + 1) * HEAD_DIM]
        return jnp.concatenate([v, v, jnp.ones((kv.shape[0], LANES), kv.dtype)], axis=1)

    def scores(kh):
        cols = [q_ref[:, (kh * Q_PER_KV + 2 * c) * HEAD_DIM:(kh * Q_PER_KV + 2 * c + 2) * HEAD_DIM]
                for c in range(Q_PER_KV // 2)]
        qs = jnp.concatenate([jnp.where(low, cols[g // 2], zero) if g % 2 == 0 else jnp.where(low, zero, cols[g // 2])
                              for g in range(Q_PER_KV)], axis=0)
        s_m = lax.dot_general(qs, dup_keys(kv_m, kh), contract_last, preferred_element_type=F32)
        s_w = lax.dot_general(qs, dup_keys(kv_w, kh), contract_last, preferred_element_type=F32)
        return s_m + bias_ref[kh, :, :BLOCK], s_w + bias_ref[kh, :, BLOCK:]

    def finish(kh, s_m, s_w):
        sink = jnp.zeros((rows, 1), F32)
        for g in range(Q_PER_KV):
            sink = jnp.where(head_of_row == g, sink_ref[layer, kh * Q_PER_KV + g], sink)
        m = jnp.maximum(jnp.maximum(jnp.max(s_m, axis=-1, keepdims=True), jnp.max(s_w, axis=-1, keepdims=True)), sink)
        p_m = jnp.exp2(s_m - m).astype(BF16)
        p_w = jnp.exp2(s_w - m).astype(BF16)
        pv = (jnp.dot(p_m, dup_values(kv_m, kh), preferred_element_type=F32)
              + jnp.dot(p_w, dup_values(kv_w, kh), preferred_element_type=F32))
        denom = pv[:, 2 * HEAD_DIM:] + jnp.exp2(sink - m)
        o = pv[:, :2 * HEAD_DIM] * (1.0 / denom)
        for c in range(Q_PER_KV // 2):
            pair = jnp.where(low, o[2 * c * BLOCK:(2 * c + 1) * BLOCK], o[(2 * c + 1) * BLOCK:(2 * c + 2) * BLOCK])
            o_ref[:, (kh * Q_PER_KV + 2 * c) * HEAD_DIM:(kh * Q_PER_KV + 2 * c + 2) * HEAD_DIM] = pair.astype(o_ref.dtype)

    pending = scores(0)
    for kh in range(N_KV_HEADS):
        ahead = scores(kh + 1) if kh + 1 < N_KV_HEADS else None
        finish(kh, *pending)
        pending = ahead


def _attention(qkv, bias_tables, sinks, layer, bsz, nb):
    t = qkv.shape[0]
    kv_col = ATTN_WIDTH // (2 * KV_WIDTH)
    kv_spec = lambda row_fn: pl.BlockSpec((BLOCK, 2 * KV_WIDTH), lambda n, b: (row_fn(n, b), kv_col))
    return pl.pallas_call(
        functools.partial(_attn_kernel, layer=layer),
        grid=(nb, bsz),
        in_specs=[
            pl.BlockSpec(memory_space=pltpu.SMEM),
            pl.BlockSpec((BLOCK, ATTN_WIDTH), lambda n, b: (n * bsz + b, 0)),
            kv_spec(lambda n, b: b),
            kv_spec(lambda n, b: jnp.maximum(n - 1, 0) * bsz + b),
            kv_spec(lambda n, b: n * bsz + b),
            pl.BlockSpec((None, N_KV_HEADS, Q_PER_KV * BLOCK, 3 * BLOCK),
                         lambda n, b: (jnp.minimum(n, 2), 0, 0, 0)),
        ],
        out_specs=pl.BlockSpec((BLOCK, ATTN_WIDTH), lambda n, b: (n * bsz + b, 0)),
        out_shape=jax.ShapeDtypeStruct((t, ATTN_WIDTH), BF16),
        compiler_params=_cparams("parallel", "parallel"),
        name="swa",
    )(sinks, qkv, qkv, qkv, qkv, bias_tables)


def _t5_bucket(dist):
    n = jnp.maximum(dist, 0)
    max_exact = N_BUCKETS // 2
    nf = jnp.maximum(n, 1).astype(F32)
    large = max_exact + (jnp.log(nf / max_exact) / math.log(MAX_DISTANCE / max_exact)
                         * (N_BUCKETS - max_exact)).astype(jnp.int32)
    large = jnp.minimum(large, N_BUCKETS - 1)
    return jnp.where(n < max_exact, n, large)


def _bias_tables(rel_bias):
    blk = jnp.arange(3, dtype=jnp.int32)[:, None]
    j = jnp.arange(BLOCK, dtype=jnp.int32)[None, :]
    q_pos = blk * BLOCK + j
    k_pos = jnp.concatenate([jnp.broadcast_to(j, (3, BLOCK)), (blk - 1) * BLOCK + j, q_pos], axis=1)
    dist = q_pos[:, :, None] - k_pos[:, None, :]
    kp = k_pos[:, None, :]
    is_meta_seg = (jnp.arange(3 * BLOCK) < BLOCK)[None, None, :]
    valid = (dist >= 0) & jnp.where(is_meta_seg, kp >= META_PAD, (kp >= BLOCK) & (dist < BLOCK))
    onehot = jax.nn.one_hot(_t5_bucket(dist), N_BUCKETS, dtype=F32)
    bias = jnp.einsum("tqkb,bh->thqk", onehot, rel_bias.astype(F32), precision=lax.Precision.HIGHEST)
    bias = jnp.where(valid[:, None], bias * LOG2_E, NEG_INF)
    return bias.reshape(3, N_KV_HEADS, Q_PER_KV * BLOCK, 3 * BLOCK)


def _s5_discretize_kernel(lr_ref, li_ref, ls_ref, bre_ref, bim_ref, ar_ref, ai_ref, bbr_ref, bbi_ref):
    lr, li = lr_ref[...], li_ref[...]
    dt = jnp.exp(ls_ref[...])
    decay = jnp.exp(lr * dt)
    ar, ai = decay * jnp.cos(li * dt), decay * jnp.sin(li * dt)
    den = lr * lr + li * li
    nr, ni = ar - 1.0, ai
    zr = (nr * lr + ni * li) / den
    zi = (ni * lr - nr * li) / den
    bre, bim = bre_ref[...], bim_ref[...]
    ar_ref[...] = ar
    ai_ref[...] = ai
    bbr_ref[...] = zr * bre - zi * bim
    bbi_ref[...] = zr * bim + zi * bre


def _s5_discretize(lam_re, lam_im, log_step, b_re, b_im):
    depth, g, n, p = b_re.shape
    flat = lambda a: a.astype(F32).reshape(depth * g, n * p)
    rep = lambda a: flat(jnp.broadcast_to(a.astype(F32)[..., None], (depth, g, n, p)))
    ls = flat(jnp.broadcast_to(log_step.astype(F32)[:, :, None, None], (depth, g, n, p)))
    shp = jax.ShapeDtypeStruct((depth * g, n * p), F32)
    ar, ai, bbr, bbi = pl.pallas_call(
        _s5_discretize_kernel, out_shape=[shp] * 4, name="s5_discretize",
    )(rep(lam_re), rep(lam_im), ls, flat(b_re), flat(b_im))
    state = lambda a: a.reshape(depth, g, n, p)[..., 0]
    return state(ar), state(ai), bbr.reshape(depth, g, n, p), bbi.reshape(depth, g, n, p)


def _s5_scan_weights(ar, ai, bbr, bbi, c_re, c_im, d_skip):
    depth, g, n, p = bbr.shape
    halves, q_per, gl_per = 2, CHUNKS_PER_SLAB, GROUPS_PER_CHUNK
    gi_per = q_per * gl_per
    slabs = g // (halves * gi_per)
    n_ch = halves * gi_per * p
    gi_of_ch = (jnp.arange(n_ch) // p) % gi_per
    lane_blocks = [(q, ri, gl) for q in range(q_per) for ri in range(2) for gl in range(gl_per)]

    b_ri = jnp.stack([bbr, bbi], axis=1).reshape(depth, 2, halves, slabs, gi_per, n, p)
    b_rows = b_ri.transpose(0, 3, 1, 2, 4, 6, 5).reshape(depth, slabs, 2, n_ch, n)
    wb = jnp.concatenate([jnp.where((gi_of_ch == gl_per * q + gl)[:, None], b_rows[:, :, ri], 0.0)
                          for q, ri, gl in lane_blocks], axis=-1)

    c_ri = jnp.stack([c_re.astype(F32), -c_im.astype(F32)], axis=1).reshape(depth, 2, halves, slabs, gi_per, p, n)
    c_cols = c_ri.transpose(0, 3, 1, 6, 2, 4, 5).reshape(depth, slabs, 2, n, n_ch)
    wc = jnp.concatenate([jnp.where((gi_of_ch == gl_per * q + gl)[None, :], c_cols[:, :, ri], 0.0)
                          for q, ri, gl in lane_blocks], axis=-2)

    def per_seq(a):
        return jnp.repeat(a.astype(F32).reshape(depth, halves, -1), SSM_SEQS // halves, axis=1)

    return wb.astype(BF16), wc.astype(BF16), per_seq(ar), per_seq(ai), per_seq(d_skip)


SCAN_SUB_BLOCKS = 4


def _s5_scan_kernel(x_ref, wu0_ref, wu1_ref, wb_ref, wc_ref, ar_ref, ai_ref, d_ref, *refs, n_batch, unroll, casts):
    (y_ref,), cast_jobs, (u_scr, s_ref, y_scr, h_ref) = _split_cast_refs(refs, casts, 1)
    steps = BLOCK
    sub_steps = steps // SCAN_SUB_BLOCKS
    sub_rows = sub_steps * SSM_SEQS
    n_slabs = wb_ref.shape[0]
    half_w = n_slabs * LANES
    n_chunks = n_slabs * CHUNKS_PER_SLAB

    @pl.when(pl.program_id(0) == 0)
    def _():
        h_ref[...] = jnp.zeros_like(h_ref)

    def seq_rows(q, b, half):
        return pl.ds(q * sub_rows + half * n_batch + b, sub_steps, stride=SSM_SEQS)

    def block_rows(q):
        return slice(q * sub_rows, (q + 1) * sub_rows)

    def channels(half, j):
        return slice(half * half_w + j * LANES, half * half_w + (j + 1) * LANES)

    half0 = (lax.broadcasted_iota(jnp.int32, (sub_rows, 1), 0) & (SSM_SEQS - 1)) < (SSM_SEQS // 2)

    def project_u(q):
        x = jnp.concatenate([x_ref[b * steps + q * sub_steps:b * steps + (q + 1) * sub_steps, :]
                             for b in range(n_batch)], axis=0)
        u_half = [jnp.dot(x, w[...], preferred_element_type=F32) for w in (wu0_ref, wu1_ref)]
        for b in range(n_batch):
            for half in range(2):
                for j in range(n_slabs):
                    u_scr[j, seq_rows(q, b, half), :] = (
                        u_half[half][b * sub_steps:(b + 1) * sub_steps, j * LANES:(j + 1) * LANES])

    def drive(q, j):
        uj = u_scr[j, block_rows(q), :]
        lhs = jnp.concatenate([jnp.where(half0, uj, 0.0), jnp.where(half0, 0.0, uj)], axis=1).astype(BF16)
        s_ref[block_rows(q), j * SLAB_STATE_LANES:(j + 1) * SLAB_STATE_LANES] = jnp.dot(
            lhs, wb_ref[j], preferred_element_type=F32)

    def recur(q, j):
        chunks = range(j * CHUNKS_PER_SLAB, (j + 1) * CHUNKS_PER_SLAB)
        a_r = [ar_ref[:, c * LANES:(c + 1) * LANES] for c in chunks]
        a_i = [ai_ref[:, c * LANES:(c + 1) * LANES] for c in chunks]
        init = []
        for c in chunks:
            init += [h_ref[:, 2 * c * LANES:(2 * c + 1) * LANES], h_ref[:, (2 * c + 1) * LANES:(2 * c + 2) * LANES]]

        state = init
        for t in range(q * sub_steps, (q + 1) * sub_steps):
            rows_t = slice(t * SSM_SEQS, (t + 1) * SSM_SEQS)
            new = []
            for k, c in enumerate(chunks):
                hr, hi = state[2 * k], state[2 * k + 1]
                re_sl = (rows_t, slice(2 * c * LANES, (2 * c + 1) * LANES))
                im_sl = (rows_t, slice((2 * c + 1) * LANES, (2 * c + 2) * LANES))
                nr = a_r[k] * hr - a_i[k] * hi + s_ref[re_sl]
                ni = a_r[k] * hi + a_i[k] * hr + s_ref[im_sl]
                s_ref[re_sl] = nr
                s_ref[im_sl] = ni
                new += [nr, ni]
            state = new
        for k, c in enumerate(chunks):
            h_ref[:, 2 * c * LANES:(2 * c + 1) * LANES] = state[2 * k]
            h_ref[:, (2 * c + 1) * LANES:(2 * c + 2) * LANES] = state[2 * k + 1]

    def readout(q, j):
        hs = s_ref[block_rows(q), j * SLAB_STATE_LANES:(j + 1) * SLAB_STATE_LANES].astype(BF16)
        both = jnp.dot(hs, wc_ref[j], preferred_element_type=F32)
        yj = jnp.where(half0, both[:, :LANES], both[:, LANES:])
        dj = jnp.tile(d_ref[:, j * LANES:(j + 1) * LANES], (sub_steps, 1))
        y_scr[j, block_rows(q), :] = jax.nn.gelu(yj + dj * u_scr[j, block_rows(q), :])

    def emit(q):
        for b in range(n_batch):
            for half in range(2):
                for j in range(n_slabs):
                    y_ref[b * steps + q * sub_steps:b * steps + (q + 1) * sub_steps, channels(half, j)] = (
                        y_scr[j, seq_rows(q, b, half), :].astype(y_ref.dtype))

    n_sub = SCAN_SUB_BLOCKS
    for s in range(n_sub + 3):
        if s < n_sub:
            project_u(s)
        for j in range(n_slabs):
            if 0 <= s - 1 < n_sub:
                drive(s - 1, j)
            if 0 <= s - 2 < n_sub:
                recur(s - 2, j)
            if 0 <= s - 3 < n_sub:
                readout(s - 3, j)
        if 0 <= s - 3 < n_sub:
            emit(s - 3)
    _run_casts(cast_jobs)


def _s5_scan(hb, w_in, w_layer, wb, wc, a_r, a_i, d_rows, layer, n_batch, cast_jobs=()):
    t, d = hb.shape
    tile = n_batch * BLOCK
    rows = BLOCK * SSM_SEQS
    n_slabs = wb.shape[1]
    state_lanes = n_slabs * SLAB_STATE_LANES
    half_w = n_slabs * LANES
    ssm_w = 2 * half_w
    u_block = QKV_WIDTH // half_w
    assert QKV_WIDTH % half_w == 0
    u_spec = lambda half: pl.BlockSpec((None, d, half_w), lambda i: (w_layer, 0, u_block + half),
                                       pipeline_mode=pl.Buffered(1))
    return pl.pallas_call(
        functools.partial(_s5_scan_kernel, n_batch=n_batch, unroll=True,
                          casts=tuple(job["has_scale"] for job in cast_jobs)),
        grid=(t // tile,),
        in_specs=[
            pl.BlockSpec((tile, d), lambda i: (i, 0)),
            u_spec(0), u_spec(1), _layer_spec(wb.shape, layer), _layer_spec(wc.shape, layer),
            _layer_spec(a_r.shape, layer), _layer_spec(a_i.shape, layer), _layer_spec(d_rows.shape, layer),
        ] + [s for job in cast_jobs for s in job["in_specs"]],
        out_specs=[pl.BlockSpec((tile, ssm_w), lambda i: (i, 0))] + [job["out_spec"] for job in cast_jobs],
        out_shape=[jax.ShapeDtypeStruct((t, ssm_w), BF16)] + [job["out_shape"] for job in cast_jobs],
        scratch_shapes=[pltpu.VMEM((n_slabs, rows, LANES), F32), pltpu.VMEM((rows, state_lanes), F32),
                        pltpu.VMEM((n_slabs, rows, LANES), F32), pltpu.VMEM((SSM_SEQS, state_lanes), F32)],
        compiler_params=_cparams("arbitrary"),
        name="s5_scan",
    )(hb, w_in, w_in, wb, wc, a_r, a_i, d_rows, *[a for job in cast_jobs for a in job["args"]])


def _mix_kernel(ya_ref, ys_ref, gate_ref, h_ref, wglu_ref, wau_ref, wsu_ref, wout_ref, g_ref, b_ref,
                *refs, alpha, n_batch, casts):
    (hf_ref, hb_ref), cast_jobs, _ = _split_cast_refs(refs, casts, 2)
    tm, d = h_ref.shape
    y = ys_ref[...]
    glu = jnp.dot(y, wglu_ref[...], preferred_element_type=F32)
    y_ssm = (y.astype(F32) * _sigmoid(glu)).astype(BF16)
    up_a = jnp.dot(ya_ref[...], wau_ref[...], preferred_element_type=F32)
    up_s = jnp.dot(y_ssm, wsu_ref[...], preferred_element_type=F32)
    merged = gate_ref[:, :d].astype(F32) * up_a + gate_ref[:, d:].astype(F32) * up_s
    mixed = jnp.dot(merged.astype(BF16), wout_ref[...], preferred_element_type=F32)
    out = _layer_norm(alpha * h_ref[...] + mixed, g_ref[...], b_ref[...])
    out = jnp.where(_real_row_mask(pl.program_id(0) * tm, tm, n_batch), out, 0.0)
    hf_ref[...] = out
    hb_ref[...] = out.astype(BF16)
    _run_casts(cast_jobs)


MIX_ROWS = 2 * BLOCK


def _mix(y_attn, y_ssm, gates, h, wglu, wau, wsu, wout, g, b, layer, alpha, n_batch, cast_jobs=()):
    t, d = h.shape
    tm = MIX_ROWS
    row = lambda w: pl.BlockSpec((tm, w), lambda i: (i, 0))
    return pl.pallas_call(
        functools.partial(_mix_kernel, alpha=alpha, n_batch=n_batch,
                          casts=tuple(job["has_scale"] for job in cast_jobs)),
        grid=(t // tm,),
        in_specs=[row(y_attn.shape[1]), row(y_ssm.shape[1]), row(gates.shape[1]), row(d),
                  _layer_spec(wglu.shape, layer), _layer_spec(wau.shape, layer), _layer_spec(wsu.shape, layer),
                  _layer_spec(wout.shape, layer), _layer_spec(g.shape, layer), _layer_spec(b.shape, layer)]
                 + [s for job in cast_jobs for s in job["in_specs"]],
        out_specs=[row(d), row(d)] + [job["out_spec"] for job in cast_jobs],
        out_shape=[jax.ShapeDtypeStruct((t, d), F32), jax.ShapeDtypeStruct((t, d), BF16)]
                  + [job["out_shape"] for job in cast_jobs],
        compiler_params=_cparams("arbitrary"),
        name="mix_ln",
    )(y_attn, y_ssm, gates, h, wglu, wau, wsu, wout, g, b, *[a for job in cast_jobs for a in job["args"]])


def _mlp_kernel(xb_ref, xf_ref, wup_ref, wdn_ref, g_ref, b_ref, *refs, alpha, n_batch, final):
    out_refs, acc_ref = refs[:-1], refs[-1]
    i, j = pl.program_id(0), pl.program_id(1)
    tm, d = xb_ref.shape

    @pl.when(j == 0)
    def _():
        acc_ref[...] = alpha * xf_ref[...]

    a = jnp.maximum(jnp.dot(xb_ref[...], wup_ref[...], preferred_element_type=F32), 0.0)
    acc_ref[...] += jnp.dot((a * a).astype(BF16), wdn_ref[...], preferred_element_type=F32)

    @pl.when(j == pl.num_programs(1) - 1)
    def _():
        out = _layer_norm(acc_ref[...], g_ref[...], b_ref[...])
        if final:
            out_refs[0][...] = out.reshape(n_batch, BLOCK, d)
        else:
            out = jnp.where(_real_row_mask(i * tm, tm, n_batch), out, 0.0)
            out_refs[0][...] = out
            out_refs[1][...] = out.astype(BF16)


def _mlp(hb, hf, wup, wdn, g, b, layer, alpha, n_batch, final):
    t, d = hf.shape
    dff = wup.shape[2]
    tm = n_batch * BLOCK
    tf = 1024
    first = 1 if final else 0
    row = lambda: pl.BlockSpec((tm, d), lambda i, j: (i + first, 0))
    if final:
        out_specs = [pl.BlockSpec((n_batch, BLOCK, d), lambda i, j: (0, i, 0))]
        out_shape = [jax.ShapeDtypeStruct((n_batch, t // n_batch - BLOCK, d), F32)]
    else:
        out_specs = [row(), row()]
        out_shape = [jax.ShapeDtypeStruct((t, d), F32), jax.ShapeDtypeStruct((t, d), BF16)]
    return pl.pallas_call(
        functools.partial(_mlp_kernel, alpha=alpha, n_batch=n_batch, final=final),
        grid=(t // tm - first, dff // tf),
        in_specs=[row(), row(),
                  pl.BlockSpec((None, d, tf), lambda i, j: (0, 0, j)),
                  pl.BlockSpec((None, tf, d), lambda i, j: (0, j, 0)),
                  _layer_spec(g.shape, layer), _layer_spec(b.shape, layer)],
        out_specs=out_specs,
        out_shape=out_shape,
        scratch_shapes=[pltpu.VMEM((tm, d), F32)],
        compiler_params=_cparams("parallel", "arbitrary"),
        name="mlp_ln",
    )(hb, hf, wup, wdn, g, b)


def kernel(x, meta_tokens, ln_emb_g, ln_emb_b, rel_bias, in_proj, gate_b, attn_sinks, ssm_lambda_re,
           ssm_lambda_im, ssm_log_step, ssm_b_re, ssm_b_im, ssm_c_re, ssm_c_im, ssm_d, ssm_w_glu, w_attn_up,
           w_ssm_up, w_out, ln_mix_g, ln_mix_b, w_mlp_up, w_mlp_down, ln_mlp_g, ln_mlp_b):
    bsz, seq, d = x.shape
    depth = in_proj.shape[0]
    assert seq % BLOCK == 0 and 2 * bsz == SSM_SEQS and meta_tokens.shape[0] == N_META
    nb = seq // BLOCK + 1
    alpha = (2 * depth) ** 0.25
    row2 = lambda a: a.astype(F32).reshape(1, -1)
    rows = lambda a: a.astype(F32)[:, None, :]
    bf = lambda a: a.astype(BF16)

    meta_pad = jnp.concatenate([jnp.zeros((META_PAD, d), x.dtype), meta_tokens.astype(x.dtype)], axis=0)
    hf, hb = _embed_ln(x, meta_pad, row2(ln_emb_g), row2(ln_emb_b))
    bias_tables = _bias_tables(rel_bias)

    q_scale = jnp.where(jnp.arange(in_proj.shape[2]) < ATTN_WIDTH, HEAD_DIM ** -0.5 * LOG2_E, 1.0)
    q_scale = q_scale.astype(in_proj.dtype)[None, :]
    w_glu, w_au, w_su, w_o = bf(ssm_w_glu), bf(w_attn_up), bf(w_ssm_up), bf(w_out)
    sinks = attn_sinks.astype(F32) * LOG2_E
    ar, ai, bbr, bbi = _s5_discretize(ssm_lambda_re, ssm_lambda_im, ssm_log_step, ssm_b_re, ssm_b_im)
    wb, wc, a_r, a_i, d_rows = _s5_scan_weights(ar, ai, bbr, bbi, ssm_c_re, ssm_c_im, ssm_d)

    t = hf.shape[0]
    scan_steps, mix_steps = t // (bsz * BLOCK), t // MIX_ROWS
    w_in = bf(in_proj[:1] * q_scale)

    for l in range(depth):
        final = l == depth - 1
        qkv, gates = _in_proj(hb, w_in, 0, rows(gate_b), l, bsz * BLOCK)
        y_attn = _attention(qkv, bias_tables, sinks, l, bsz, nb)
        y_gelu, w_up = _s5_scan(hb, w_in, 0, wb, wc, a_r, a_i, d_rows, l, bsz,
                                cast_jobs=[_cast_job(w_mlp_up, l, scan_steps)])
        mix_jobs = [_cast_job(w_mlp_down, l, mix_steps)]
        if not final:
            mix_jobs.append(_cast_job(in_proj, l + 1, mix_steps, scale=q_scale))
        hf, hb, w_dn, *w_next = _mix(y_attn, y_gelu, gates, hf, w_glu, w_au, w_su, w_o, rows(ln_mix_g),
                                     rows(ln_mix_b), l, alpha, bsz, cast_jobs=mix_jobs)
        out = _mlp(hb, hf, w_up, w_dn, rows(ln_mlp_g), rows(ln_mlp_b), l, alpha, bsz, final)
        if not final:
            (hf, hb), (w_in,) = out, w_next

    return out[0]
```

```python
import functools
import math

import jax
import jax.numpy as jnp
from jax import lax
from jax.experimental import pallas as pl
from jax.experimental.pallas import tpu as pltpu

F32 = jnp.float32
BF16 = jnp.bfloat16

N_META = 16
BLOCK = 128
META_PAD = BLOCK - N_META
HEAD_DIM = 64
N_Q_HEADS = 16
N_KV_HEADS = 4
Q_PER_KV = N_Q_HEADS // N_KV_HEADS
ATTN_WIDTH = N_Q_HEADS * HEAD_DIM
KV_WIDTH = N_KV_HEADS * HEAD_DIM
QKV_WIDTH = ATTN_WIDTH + 2 * KV_WIDTH
SSM_GROUP = 16
SSM_STATE = 64
N_BUCKETS = 32
MAX_DISTANCE = 128
LN_EPS = 1e-5
NEG_INF = -1e30
LOG2_E = math.log2(math.e)

LANES = 128
SUBLANES = 8
VMEM_LIMIT_BYTES = 56 * 1024 * 1024

SSM_SEQS = SUBLANES
STATES_PER_CHUNK = LANES
GROUPS_PER_CHUNK = STATES_PER_CHUNK // SSM_STATE
CHUNKS_PER_SLAB = 4
SLAB_STATE_LANES = CHUNKS_PER_SLAB * 2 * STATES_PER_CHUNK


def _cparams(*sem):
    return pltpu.CompilerParams(dimension_semantics=sem, vmem_limit_bytes=VMEM_LIMIT_BYTES)


def _resident(shape):
    nd = len(shape)
    return pl.BlockSpec(shape, lambda *_: (0,) * nd, pipeline_mode=pl.Buffered(1))


def _layer_spec(shape, layer):
    nd = len(shape)
    return pl.BlockSpec((None,) + tuple(shape[1:]), lambda *_: (layer,) + (0,) * (nd - 1),
                        pipeline_mode=pl.Buffered(1))


def _even_parts(total, max_parts, align):
    return max(p for p in range(1, max_parts + 1) if total % p == 0 and (total // p) % align == 0)


def _cast_job(w, layer, n_steps, scale=None):
    _, r, c = w.shape
    parts = _even_parts(r, n_steps, 2 * SUBLANES)
    blk = (None, r // parts, c)
    row = lambda i: jnp.minimum(i, parts - 1)
    in_specs, args = [pl.BlockSpec(blk, lambda i: (layer, row(i), 0))], [w]
    if scale is not None:
        in_specs.append(_resident(scale.shape))
        args.append(scale)
    return dict(in_specs=in_specs, args=args, has_scale=scale is not None,
                out_spec=pl.BlockSpec(blk, lambda i: (0, row(i), 0)),
                out_shape=jax.ShapeDtypeStruct((1, r, c), BF16))


def _split_cast_refs(refs, casts, n_out):
    n_in = sum(2 if s else 1 for s in casts)
    cast_in, rest = refs[:n_in], refs[n_in:]
    outs, cast_out, scratch = rest[:n_out], rest[n_out:n_out + len(casts)], rest[n_out + len(casts):]
    jobs, k = [], 0
    for has_scale, dst in zip(casts, cast_out):
        jobs.append((cast_in[k], cast_in[k + 1] if has_scale else None, dst))
        k += 2 if has_scale else 1
    return outs, jobs, scratch


def _run_casts(jobs):
    for src, scale, dst in jobs:
        v = src[...] if scale is None else src[...] * scale[...]
        dst[...] = v.astype(dst.dtype)


def _layer_norm(x, g, b):
    mu = jnp.mean(x, axis=-1, keepdims=True)
    xc = x - mu
    var = jnp.mean(xc * xc, axis=-1, keepdims=True)
    return xc * lax.rsqrt(var + LN_EPS) * g + b


def _sigmoid(z):
    return 1.0 / (1.0 + jnp.exp(-z))


def _real_row_mask(first_row, n_rows, n_batch):
    r = first_row + lax.broadcasted_iota(jnp.int32, (n_rows, 1), 0)
    is_pad = (r < n_batch * BLOCK) & ((r & (BLOCK - 1)) < META_PAD)
    return jnp.logical_not(is_pad)


def _embed_ln_kernel(meta_ref, x_ref, g_ref, b_ref, hf_ref, hb_ref):
    is_meta = pl.program_id(0) == 0
    rows = lax.broadcasted_iota(jnp.int32, (BLOCK, 1), 0)
    keep = jnp.logical_or(jnp.logical_not(is_meta), rows >= META_PAD)
    for bi in range(x_ref.shape[0]):
        xin = jnp.where(is_meta, meta_ref[...], x_ref[bi])
        y = jnp.where(keep, _layer_norm(xin, g_ref[...], b_ref[...]), 0.0)
        hf_ref[bi * BLOCK:(bi + 1) * BLOCK, :] = y
        hb_ref[bi * BLOCK:(bi + 1) * BLOCK, :] = y.astype(BF16)


def _embed_ln(x, meta_pad, g, b):
    bsz, seq, d = x.shape
    nb = seq // BLOCK + 1
    t = bsz * nb * BLOCK
    tile = pl.BlockSpec((bsz * BLOCK, d), lambda n: (n, 0))
    return pl.pallas_call(
        _embed_ln_kernel,
        grid=(nb,),
        in_specs=[
            _resident(meta_pad.shape),
            pl.BlockSpec((bsz, BLOCK, d), lambda n: (0, jnp.maximum(n - 1, 0), 0)),
            _resident(g.shape), _resident(b.shape),
        ],
        out_specs=[tile, tile],
        out_shape=[jax.ShapeDtypeStruct((t, d), F32), jax.ShapeDtypeStruct((t, d), BF16)],
        compiler_params=_cparams("parallel"),
        name="embed_ln",
    )(meta_pad, x, g, b)


GATE_COL_CHUNK = 1024


def _in_proj_kernel(x_ref, w_ref, gb_ref, qkv_ref, gate_ref):
    x = x_ref[...]
    qkv_ref[...] = jnp.dot(x, w_ref[:, :QKV_WIDTH], preferred_element_type=F32).astype(qkv_ref.dtype)
    gate0 = w_ref.shape[1] - gate_ref.shape[1]
    for c in range(0, gate_ref.shape[1], GATE_COL_CHUNK):
        z = jnp.dot(x, w_ref[:, gate0 + c:gate0 + c + GATE_COL_CHUNK], preferred_element_type=F32)
        z = z + gb_ref[:, c:c + GATE_COL_CHUNK]
        gate_ref[:, c:c + GATE_COL_CHUNK] = _sigmoid(z).astype(gate_ref.dtype)


def _in_proj(hb, w_in, w_layer, gate_bias, layer, tm):
    t, d = hb.shape
    gate_w = gate_bias.shape[2]
    row = lambda w: pl.BlockSpec((tm, w), lambda i: (i, 0))
    return pl.pallas_call(
        _in_proj_kernel,
        grid=(t // tm,),
        in_specs=[row(d), _layer_spec(w_in.shape, w_layer), _layer_spec(gate_bias.shape, layer)],
        out_specs=[row(QKV_WIDTH), row(gate_w)],
        out_shape=[jax.ShapeDtypeStruct((t, QKV_WIDTH), BF16), jax.ShapeDtypeStruct((t, gate_w), BF16)],
        compiler_params=_cparams("parallel"),
        name="in_proj",
    )(hb, w_in, gate_bias)


def _attn_kernel(sink_ref, q_ref, kvm_ref, kvp_ref, kvc_ref, bias_ref, o_ref, *, layer):
    kv_m = kvm_ref[...]
    kv_w = jnp.concatenate([kvp_ref[...], kvc_ref[...]], axis=0)
    rows = Q_PER_KV * BLOCK
    low = lax.broadcasted_iota(jnp.int32, (1, LANES), 1) < HEAD_DIM
    head_of_row = lax.broadcasted_iota(jnp.int32, (rows, 1), 0) // BLOCK
    zero = jnp.zeros((), q_ref.dtype)
    contract_last = (((1,), (1,)), ((), ()))

    def dup_keys(kv, kh):
        k = kv[:, kh * HEAD_DIM:(kh + 1) * HEAD_DIM]
        return jnp.concatenate([k, k], axis=1)

    def dup_values(kv, kh):
        v = kv[:, KV_WIDTH + kh * HEAD_DIM:KV_WIDTH + (kh + 1) * HEAD_DIM]
        return jnp.concatenate([v, v, jnp.ones((kv.shape[0], LANES), kv.dtype)], axis=1)

    def scores(kh):
        cols = [q_ref[:, (kh * Q_PER_KV + 2 * c) * HEAD_DIM:(kh * Q_PER_KV + 2 * c + 2) * HEAD_DIM]
                for c in range(Q_PER_KV // 2)]
        qs = jnp.concatenate([jnp.where(low, cols[g // 2], zero) if g % 2 == 0 else jnp.where(low, zero, cols[g // 2])
                              for g in range(Q_PER_KV)], axis=0)
        s_m = lax.dot_general(qs, dup_keys(kv_m, kh), contract_last, preferred_element_type=F32)
        s_w = lax.dot_general(qs, dup_keys(kv_w, kh), contract_last, preferred_element_type=F32)
        return s_m + bias_ref[kh, :, :BLOCK], s_w + bias_ref[kh, :, BLOCK:]

    def finish(kh, s_m, s_w):
        sink = jnp.zeros((rows, 1), F32)
        for g in range(Q_PER_KV):
            sink = jnp.where(head_of_row == g, sink_ref[layer, kh * Q_PER_KV + g], sink)
        m = jnp.maximum(jnp.maximum(jnp.max(s_m, axis=-1, keepdims=True), jnp.max(s_w, axis=-1, keepdims=True)), sink)
        p_m = jnp.exp2(s_m - m).astype(BF16)
        p_w = jnp.exp2(s_w - m).astype(BF16)
        pv = (jnp.dot(p_m, dup_values(kv_m, kh), preferred_element_type=F32)
              + jnp.dot(p_w, dup_values(kv_w, kh), preferred_element_type=F32))
        denom = pv[:, 2 * HEAD_DIM:] + jnp.exp2(sink - m)
        o = pv[:, :2 * HEAD_DIM] * (1.0 / denom)
        for c in range(Q_PER_KV // 2):
            pair = jnp.where(low, o[2 * c * BLOCK:(2 * c + 1) * BLOCK], o[(2 * c + 1) * BLOCK:(2 * c + 2) * BLOCK])
            o_ref[:, (kh * Q_PER_KV + 2 * c) * HEAD_DIM:(kh * Q_PER_KV + 2 * c + 2) * HEAD_DIM] = pair.astype(o_ref.dtype)

    pending = scores(0)
    for kh in range(N_KV_HEADS):
        ahead = scores(kh + 1) if kh + 1 < N_KV_HEADS else None
        finish(kh, *pending)
        pending = ahead


def _attention(qkv, bias_tables, sinks, layer, bsz, nb):
    t = qkv.shape[0]
    kv_col = ATTN_WIDTH // (2 * KV_WIDTH)
    kv_spec = lambda row_fn: pl.BlockSpec((BLOCK, 2 * KV_WIDTH), lambda n, b: (row_fn(n, b), kv_col))
    return pl.pallas_call(
        functools.partial(_attn_kernel, layer=layer),
        grid=(nb, bsz),
        in_specs=[
            pl.BlockSpec(memory_space=pltpu.SMEM),
            pl.BlockSpec((BLOCK, ATTN_WIDTH), lambda n, b: (n * bsz + b, 0)),
            kv_spec(lambda n, b: b),
            kv_spec(lambda n, b: jnp.maximum(n - 1, 0) * bsz + b),
            kv_spec(lambda n, b: n * bsz + b),
            pl.BlockSpec((None, N_KV_HEADS, Q_PER_KV * BLOCK, 3 * BLOCK),
                         lambda n, b: (jnp.minimum(n, 2), 0, 0, 0)),
        ],
        out_specs=pl.BlockSpec((BLOCK, ATTN_WIDTH), lambda n, b: (n * bsz + b, 0)),
        out_shape=jax.ShapeDtypeStruct((t, ATTN_WIDTH), BF16),
        compiler_params=_cparams("parallel", "parallel"),
        name="swa",
    )(sinks, qkv, qkv, qkv, qkv, bias_tables)


def _t5_bucket(dist):
    n = jnp.maximum(dist, 0)
    max_exact = N_BUCKETS // 2
    nf = jnp.maximum(n, 1).astype(F32)
    large = max_exact + (jnp.log(nf / max_exact) / math.log(MAX_DISTANCE / max_exact)
                         * (N_BUCKETS - max_exact)).astype(jnp.int32)
    large = jnp.minimum(large, N_BUCKETS - 1)
    return jnp.where(n < max_exact, n, large)


def _bias_tables(rel_bias):
    blk = jnp.arange(3, dtype=jnp.int32)[:, None]
    j = jnp.arange(BLOCK, dtype=jnp.int32)[None, :]
    q_pos = blk * BLOCK + j
    k_pos = jnp.concatenate([jnp.broadcast_to(j, (3, BLOCK)), (blk - 1) * BLOCK + j, q_pos], axis=1)
    dist = q_pos[:, :, None] - k_pos[:, None, :]
    kp = k_pos[:, None, :]
    is_meta_seg = (jnp.arange(3 * BLOCK) < BLOCK)[None, None, :]
    valid = (dist >= 0) & jnp.where(is_meta_seg, kp >= META_PAD, (kp >= BLOCK) & (dist < BLOCK))
    onehot = jax.nn.one_hot(_t5_bucket(dist), N_BUCKETS, dtype=F32)
    bias = jnp.einsum("tqkb,bh->thqk", onehot, rel_bias.astype(F32), precision=lax.Precision.HIGHEST)
    bias = jnp.where(valid[:, None], bias * LOG2_E, NEG_INF)
    return bias.reshape(3, N_KV_HEADS, Q_PER_KV * BLOCK, 3 * BLOCK)


def _s5_discretize_kernel(lr_ref, li_ref, ls_ref, bre_ref, bim_ref, ar_ref, ai_ref, bbr_ref, bbi_ref):
    lr, li = lr_ref[...], li_ref[...]
    dt = jnp.exp(ls_ref[...])
    decay = jnp.exp(lr * dt)
    ar, ai = decay * jnp.cos(li * dt), decay * jnp.sin(li * dt)
    den = lr * lr + li * li
    nr, ni = ar - 1.0, ai
    zr = (nr * lr + ni * li) / den
    zi = (ni * lr - nr * li) / den
    bre, bim = bre_ref[...], bim_ref[...]
    ar_ref[...] = ar
    ai_ref[...] = ai
    bbr_ref[...] = zr * bre - zi * bim
    bbi_ref[...] = zr * bim + zi * bre


def _s5_discretize(lam_re, lam_im, log_step, b_re, b_im):
    depth, g, n, p = b_re.shape
    flat = lambda a: a.astype(F32).reshape(depth * g, n * p)
    rep = lambda a: flat(jnp.broadcast_to(a.astype(F32)[..., None], (depth, g, n, p)))
    ls = flat(jnp.broadcast_to(log_step.astype(F32)[:, :, None, None], (depth, g, n, p)))
    shp = jax.ShapeDtypeStruct((depth * g, n * p), F32)
    ar, ai, bbr, bbi = pl.pallas_call(
        _s5_discretize_kernel, out_shape=[shp] * 4, name="s5_discretize",
    )(rep(lam_re), rep(lam_im), ls, flat(b_re), flat(b_im))
    state = lambda a: a.reshape(depth, g, n, p)[..., 0]
    return state(ar), state(ai), bbr.reshape(depth, g, n, p), bbi.reshape(depth, g, n, p)


def _s5_scan_weights(ar, ai, bbr, bbi, c_re, c_im, d_skip):
    depth, g, n, p = bbr.shape
    halves, q_per, gl_per = 2, CHUNKS_PER_SLAB, GROUPS_PER_CHUNK
    gi_per = q_per * gl_per
    slabs = g // (halves * gi_per)
    n_ch = halves * gi_per * p
    gi_of_ch = (jnp.arange(n_ch) // p) % gi_per
    lane_blocks = [(q, ri, gl) for q in range(q_per) for ri in range(2) for gl in range(gl_per)]

    b_ri = jnp.stack([bbr, bbi], axis=1).reshape(depth, 2, halves, slabs, gi_per, n, p)
    b_rows = b_ri.transpose(0, 3, 1, 2, 4, 6, 5).reshape(depth, slabs, 2, n_ch, n)
    wb = jnp.concatenate([jnp.where((gi_of_ch == gl_per * q + gl)[:, None], b_rows[:, :, ri], 0.0)
                          for q, ri, gl in lane_blocks], axis=-1)

    c_ri = jnp.stack([c_re.astype(F32), -c_im.astype(F32)], axis=1).reshape(depth, 2, halves, slabs, gi_per, p, n)
    c_cols = c_ri.transpose(0, 3, 1, 6, 2, 4, 5).reshape(depth, slabs, 2, n, n_ch)
    wc = jnp.concatenate([jnp.where((gi_of_ch == gl_per * q + gl)[None, :], c_cols[:, :, ri], 0.0)
                          for q, ri, gl in lane_blocks], axis=-2)

    def per_seq(a):
        return jnp.repeat(a.astype(F32).reshape(depth, halves, -1), SSM_SEQS // halves, axis=1)

    return wb.astype(BF16), wc.astype(BF16), per_seq(ar), per_seq(ai), per_seq(d_skip)


SCAN_SUB_BLOCKS = 4


def _s5_scan_kernel(x_ref, wu0_ref, wu1_ref, wb_ref, wc_ref, ar_ref, ai_ref, d_ref, *refs, n_batch, casts):
    (y_ref,), cast_jobs, (u_scr, s_ref, y_scr, h_ref) = _split_cast_refs(refs, casts, 1)
    steps = BLOCK
    sub_steps = steps // SCAN_SUB_BLOCKS
    sub_rows = sub_steps * SSM_SEQS
    n_slabs = wb_ref.shape[0]
    half_w = n_slabs * LANES

    @pl.when(pl.program_id(0) == 0)
    def _():
        h_ref[...] = jnp.zeros_like(h_ref)

    def seq_rows(q, b, half):
        return pl.ds(q * sub_rows + half * n_batch + b, sub_steps, stride=SSM_SEQS)

    def block_rows(q):
        return slice(q * sub_rows, (q + 1) * sub_rows)

    def channels(half, j):
        return slice(half * half_w + j * LANES, half * half_w + (j + 1) * LANES)

    half0 = (lax.broadcasted_iota(jnp.int32, (sub_rows, 1), 0) & (SSM_SEQS - 1)) < (SSM_SEQS // 2)

    def project_u(q):
        x = jnp.concatenate([x_ref[b * steps + q * sub_steps:b * steps + (q + 1) * sub_steps, :]
                             for b in range(n_batch)], axis=0)
        u_half = [jnp.dot(x, w[...], preferred_element_type=F32) for w in (wu0_ref, wu1_ref)]
        for b in range(n_batch):
            for half in range(2):
                for j in range(n_slabs):
                    u_scr[j, seq_rows(q, b, half), :] = (
                        u_half[half][b * sub_steps:(b + 1) * sub_steps, j * LANES:(j + 1) * LANES])

    def drive(q, j):
        uj = u_scr[j, block_rows(q), :]
        lhs = jnp.concatenate([jnp.where(half0, uj, 0.0), jnp.where(half0, 0.0, uj)], axis=1).astype(BF16)
        s_ref[block_rows(q), j * SLAB_STATE_LANES:(j + 1) * SLAB_STATE_LANES] = jnp.dot(
            lhs, wb_ref[j], preferred_element_type=F32)

    def recur(q, j):
        chunks = range(j * CHUNKS_PER_SLAB, (j + 1) * CHUNKS_PER_SLAB)
        a_r = [ar_ref[:, c * LANES:(c + 1) * LANES] for c in chunks]
        a_i = [ai_ref[:, c * LANES:(c + 1) * LANES] for c in chunks]
        init = []
        for c in chunks:
            init += [h_ref[:, 2 * c * LANES:(2 * c + 1) * LANES], h_ref[:, (2 * c + 1) * LANES:(2 * c + 2) * LANES]]

        state = init
        for t in range(q * sub_steps, (q + 1) * sub_steps):
            rows_t = slice(t * SSM_SEQS, (t + 1) * SSM_SEQS)
            new = []
            for k, c in enumerate(chunks):
                hr, hi = state[2 * k], state[2 * k + 1]
                re_sl = (rows_t, slice(2 * c * LANES, (2 * c + 1) * LANES))
                im_sl = (rows_t, slice((2 * c + 1) * LANES, (2 * c + 2) * LANES))
                nr = a_r[k] * hr - a_i[k] * hi + s_ref[re_sl]
                ni = a_r[k] * hi + a_i[k] * hr + s_ref[im_sl]
                s_ref[re_sl] = nr
                s_ref[im_sl] = ni
                new += [nr, ni]
            state = new
        for k, c in enumerate(chunks):
            h_ref[:, 2 * c * LANES:(2 * c + 1) * LANES] = state[2 * k]
            h_ref[:, (2 * c + 1) * LANES:(2 * c + 2) * LANES] = state[2 * k + 1]

    def readout(q, j):
        hs = s_ref[block_rows(q), j * SLAB_STATE_LANES:(j + 1) * SLAB_STATE_LANES].astype(BF16)
        both = jnp.dot(hs, wc_ref[j], preferred_element_type=F32)
        yj = jnp.where(half0, both[:, :LANES], both[:, LANES:])
        dj = jnp.tile(d_ref[:, j * LANES:(j + 1) * LANES], (sub_steps, 1))
        y_scr[j, block_rows(q), :] = jax.nn.gelu(yj + dj * u_scr[j, block_rows(q), :])

    def emit(q):
        for b in range(n_batch):
            for half in range(2):
                for j in range(n_slabs):
                    y_ref[b * steps + q * sub_steps:b * steps + (q + 1) * sub_steps, channels(half, j)] = (
                        y_scr[j, seq_rows(q, b, half), :].astype(y_ref.dtype))

    n_sub = SCAN_SUB_BLOCKS
    for s in range(n_sub + 3):
        if s < n_sub:
            project_u(s)
        for j in range(n_slabs):
            if 0 <= s - 1 < n_sub:
                drive(s - 1, j)
            if 0 <= s - 2 < n_sub:
                recur(s - 2, j)
            if 0 <= s - 3 < n_sub:
                readout(s - 3, j)
        if 0 <= s - 3 < n_sub:
            emit(s - 3)
    _run_casts(cast_jobs)


def _s5_scan(hb, w_in, w_layer, wb, wc, a_r, a_i, d_rows, layer, n_batch, cast_jobs=()):
    t, d = hb.shape
    tile = n_batch * BLOCK
    rows = BLOCK * SSM_SEQS
    n_slabs = wb.shape[1]
    state_lanes = n_slabs * SLAB_STATE_LANES
    half_w = n_slabs * LANES
    ssm_w = 2 * half_w
    u_block = QKV_WIDTH // half_w
    assert QKV_WIDTH % half_w == 0
    u_spec = lambda half: pl.BlockSpec((None, d, half_w), lambda i: (w_layer, 0, u_block + half),
                                       pipeline_mode=pl.Buffered(1))
    return pl.pallas_call(
        functools.partial(_s5_scan_kernel, n_batch=n_batch,
                          casts=tuple(job["has_scale"] for job in cast_jobs)),
        grid=(t // tile,),
        in_specs=[
            pl.BlockSpec((tile, d), lambda i: (i, 0)),
            u_spec(0), u_spec(1), _layer_spec(wb.shape, layer), _layer_spec(wc.shape, layer),
            _layer_spec(a_r.shape, layer), _layer_spec(a_i.shape, layer), _layer_spec(d_rows.shape, layer),
        ] + [s for job in cast_jobs for s in job["in_specs"]],
        out_specs=[pl.BlockSpec((tile, ssm_w), lambda i: (i, 0))] + [job["out_spec"] for job in cast_jobs],
        out_shape=[jax.ShapeDtypeStruct((t, ssm_w), BF16)] + [job["out_shape"] for job in cast_jobs],
        scratch_shapes=[pltpu.VMEM((n_slabs, rows, LANES), F32), pltpu.VMEM((rows, state_lanes), F32),
                        pltpu.VMEM((n_slabs, rows, LANES), F32), pltpu.VMEM((SSM_SEQS, state_lanes), F32)],
        compiler_params=_cparams("arbitrary"),
        name="s5_scan",
    )(hb, w_in, w_in, wb, wc, a_r, a_i, d_rows, *[a for job in cast_jobs for a in job["args"]])


def _mix_kernel(ya_ref, ys_ref, gate_ref, h_ref, wglu_ref, wau_ref, wsu_ref, wout_ref, g_ref, b_ref,
                *refs, alpha, n_batch, casts):
    (hf_ref, hb_ref), cast_jobs, _ = _split_cast_refs(refs, casts, 2)
    tm, d = h_ref.shape
    y = ys_ref[...]
    glu = jnp.dot(y, wglu_ref[...], preferred_element_type=F32)
    y_ssm = (y.astype(F32) * _sigmoid(glu)).astype(BF16)
    up_a = jnp.dot(ya_ref[...], wau_ref[...], preferred_element_type=F32)
    up_s = jnp.dot(y_ssm, wsu_ref[...], preferred_element_type=F32)
    merged = gate_ref[:, :d].astype(F32) * up_a + gate_ref[:, d:].astype(F32) * up_s
    mixed = jnp.dot(merged.astype(BF16), wout_ref[...], preferred_element_type=F32)
    out = _layer_norm(alpha * h_ref[...] + mixed, g_ref[...], b_ref[...])
    out = jnp.where(_real_row_mask(pl.program_id(0) * tm, tm, n_batch), out, 0.0)
    hf_ref[...] = out
    hb_ref[...] = out.astype(BF16)
    _run_casts(cast_jobs)


MIX_ROWS = 2 * BLOCK


def _mix(y_attn, y_ssm, gates, h, wglu, wau, wsu, wout, g, b, layer, alpha, n_batch, cast_jobs=()):
    t, d = h.shape
    tm = MIX_ROWS
    row = lambda w: pl.BlockSpec((tm, w), lambda i: (i, 0))
    return pl.pallas_call(
        functools.partial(_mix_kernel, alpha=alpha, n_batch=n_batch,
                          casts=tuple(job["has_scale"] for job in cast_jobs)),
        grid=(t // tm,),
        in_specs=[row(y_attn.shape[1]), row(y_ssm.shape[1]), row(gates.shape[1]), row(d),
                  _layer_spec(wglu.shape, 0), _layer_spec(wau.shape, 0), _layer_spec(wsu.shape, 0),
                  _layer_spec(wout.shape, 0), _layer_spec(g.shape, layer), _layer_spec(b.shape, layer)]
                 + [s for job in cast_jobs for s in job["in_specs"]],
        out_specs=[row(d), row(d)] + [job["out_spec"] for job in cast_jobs],
        out_shape=[jax.ShapeDtypeStruct((t, d), F32), jax.ShapeDtypeStruct((t, d), BF16)]
                  + [job["out_shape"] for job in cast_jobs],
        compiler_params=_cparams("arbitrary"),
        name="mix_ln",
    )(y_attn, y_ssm, gates, h, wglu, wau, wsu, wout, g, b, *[a for job in cast_jobs for a in job["args"]])


def _mlp_kernel(xb_ref, xf_ref, wup_ref, wdn_ref, g_ref, b_ref, *refs, alpha, n_batch, final):
    out_refs, acc_ref = refs[:-1], refs[-1]
    i, j = pl.program_id(0), pl.program_id(1)
    tm, d = xb_ref.shape

    last = pl.num_programs(1) - 1

    @pl.when(j == 0)
    def _():
        acc_ref[...] = alpha * xf_ref[...]

    def accumulate(r):
        a = jnp.maximum(jnp.dot(xb_ref[r, :], wup_ref[...], preferred_element_type=F32), 0.0)
        acc_ref[r, :] += jnp.dot((a * a).astype(BF16), wdn_ref[...], preferred_element_type=F32)

    @pl.when(j < last)
    def _():
        accumulate(slice(None))

    @pl.when(j == last)
    def _():
        half = tm // 2
        for r0 in (0, half):
            r = slice(r0, r0 + half)
            accumulate(r)
            out = _layer_norm(acc_ref[r, :], g_ref[...], b_ref[...])
            if final:
                out_refs[0][r0 // BLOCK:(r0 + half) // BLOCK] = out.reshape(half // BLOCK, BLOCK, d)
            else:
                out = jnp.where(_real_row_mask(i * tm + r0, half, n_batch), out, 0.0)
                out_refs[0][r, :] = out
                out_refs[1][r, :] = out.astype(BF16)


def _mlp(hb, hf, wup, wdn, g, b, layer, alpha, n_batch, final):
    t, d = hf.shape
    dff = wup.shape[2]
    tm = n_batch * BLOCK
    tf = 1024
    first = 1 if final else 0
    row = lambda: pl.BlockSpec((tm, d), lambda i, j: (i + first, 0))
    if final:
        out_specs = [pl.BlockSpec((n_batch, BLOCK, d), lambda i, j: (0, i, 0))]
        out_shape = [jax.ShapeDtypeStruct((n_batch, t // n_batch - BLOCK, d), F32)]
    else:
        out_specs = [row(), row()]
        out_shape = [jax.ShapeDtypeStruct((t, d), F32), jax.ShapeDtypeStruct((t, d), BF16)]
    return pl.pallas_call(
        functools.partial(_mlp_kernel, alpha=alpha, n_batch=n_batch, final=final),
        grid=(t // tm - first, dff // tf),
        in_specs=[row(), row(),
                  pl.BlockSpec((None, d, tf), lambda i, j: (0, 0, j)),
                  pl.BlockSpec((None, tf, d), lambda i, j: (0, j, 0)),
                  _layer_spec(g.shape, layer), _layer_spec(b.shape, layer)],
        out_specs=out_specs,
        out_shape=out_shape,
        scratch_shapes=[pltpu.VMEM((tm, d), F32)],
        compiler_params=_cparams("parallel", "arbitrary"),
        name="mlp_ln",
    )(hb, hf, wup, wdn, g, b)


def kernel(x, meta_tokens, ln_emb_g, ln_emb_b, rel_bias, in_proj, gate_b, attn_sinks, ssm_lambda_re,
           ssm_lambda_im, ssm_log_step, ssm_b_re, ssm_b_im, ssm_c_re, ssm_c_im, ssm_d, ssm_w_glu, w_attn_up,
           w_ssm_up, w_out, ln_mix_g, ln_mix_b, w_mlp_up, w_mlp_down, ln_mlp_g, ln_mlp_b):
    bsz, seq, d = x.shape
    depth = in_proj.shape[0]
    assert seq % BLOCK == 0 and 2 * bsz == SSM_SEQS and meta_tokens.shape[0] == N_META
    nb = seq // BLOCK + 1
    alpha = (2 * depth) ** 0.25
    row2 = lambda a: a.astype(F32).reshape(1, -1)
    rows = lambda a: a.astype(F32)[:, None, :]
    bf = lambda a: a.astype(BF16)

    meta_pad = jnp.concatenate([jnp.zeros((META_PAD, d), x.dtype), meta_tokens.astype(x.dtype)], axis=0)
    hf, hb = _embed_ln(x, meta_pad, row2(ln_emb_g), row2(ln_emb_b))
    bias_tables = _bias_tables(rel_bias)

    q_scale = jnp.where(jnp.arange(in_proj.shape[2]) < ATTN_WIDTH, HEAD_DIM ** -0.5 * LOG2_E, 1.0)
    q_scale = q_scale.astype(in_proj.dtype)[None, :]
    sinks = attn_sinks.astype(F32) * LOG2_E
    ar, ai, bbr, bbi = _s5_discretize(ssm_lambda_re, ssm_lambda_im, ssm_log_step, ssm_b_re, ssm_b_im)
    wb, wc, a_r, a_i, d_rows = _s5_scan_weights(ar, ai, bbr, bbi, ssm_c_re, ssm_c_im, ssm_d)

    t = hf.shape[0]
    scan_steps, mix_steps = t // (bsz * BLOCK), t // MIX_ROWS
    w_in = bf(in_proj[:1] * q_scale)

    for l in range(depth):
        final = l == depth - 1
        qkv, gates = _in_proj(hb, w_in, 0, rows(gate_b), l, bsz * BLOCK)
        y_attn = _attention(qkv, bias_tables, sinks, l, bsz, nb)
        scan_jobs = [_cast_job(w, l, scan_steps) for w in (w_mlp_up, ssm_w_glu, w_attn_up, w_ssm_up, w_out)]
        y_gelu, w_up, w_glu, w_au, w_su, w_o = _s5_scan(hb, w_in, 0, wb, wc, a_r, a_i, d_rows, l, bsz,
                                                        cast_jobs=scan_jobs)
        mix_jobs = [_cast_job(w_mlp_down, l, mix_steps)]
        if not final:
            mix_jobs.append(_cast_job(in_proj, l + 1, mix_steps, scale=q_scale))
        hf, hb, w_dn, *w_next = _mix(y_attn, y_gelu, gates, hf, w_glu, w_au, w_su, w_o, rows(ln_mix_g),
                                     rows(ln_mix_b), l, alpha, bsz, cast_jobs=mix_jobs)
        out = _mlp(hb, hf, w_up, w_dn, rows(ln_mlp_g), rows(ln_mlp_b), l, alpha, bsz, final)
        if not final:
            (hf, hb), (w_in,) = out, w_next

    return out[0]
```

```python
import functools
import math

import jax
import jax.numpy as jnp
from jax import lax
from jax.experimental import pallas as pl
from jax.experimental.pallas import tpu as pltpu

F32 = jnp.float32
BF16 = jnp.bfloat16

N_META = 16
BLOCK = 128
META_PAD = BLOCK - N_META
HEAD_DIM = 64
N_Q_HEADS = 16
N_KV_HEADS = 4
Q_PER_KV = N_Q_HEADS // N_KV_HEADS
ATTN_WIDTH = N_Q_HEADS * HEAD_DIM
KV_WIDTH = N_KV_HEADS * HEAD_DIM
QKV_WIDTH = ATTN_WIDTH + 2 * KV_WIDTH
SSM_GROUP = 16
SSM_STATE = 64
N_BUCKETS = 32
MAX_DISTANCE = 128
LN_EPS = 1e-5
NEG_INF = -1e30
LOG2_E = math.log2(math.e)

LANES = 128
SUBLANES = 8
VMEM_LIMIT_BYTES = 56 * 1024 * 1024

SSM_SEQS = SUBLANES
STATES_PER_CHUNK = LANES
GROUPS_PER_CHUNK = STATES_PER_CHUNK // SSM_STATE
CHUNKS_PER_SLAB = 4
SLAB_STATE_LANES = CHUNKS_PER_SLAB * 2 * STATES_PER_CHUNK


def _cparams(*sem):
    return pltpu.CompilerParams(dimension_semantics=sem, vmem_limit_bytes=VMEM_LIMIT_BYTES)


def _resident(shape):
    nd = len(shape)
    return pl.BlockSpec(shape, lambda *_: (0,) * nd, pipeline_mode=pl.Buffered(1))


def _layer_spec(shape, layer):
    nd = len(shape)
    return pl.BlockSpec((None,) + tuple(shape[1:]), lambda *_: (layer,) + (0,) * (nd - 1),
                        pipeline_mode=pl.Buffered(1))


def _even_parts(total, max_parts, align):
    return max(p for p in range(1, max_parts + 1) if total % p == 0 and (total // p) % align == 0)


def _cast_job(w, layer, n_steps, scale=None):
    _, r, c = w.shape
    parts = _even_parts(r, n_steps, 2 * SUBLANES)
    blk = (None, r // parts, c)
    row = lambda i: jnp.minimum(i, parts - 1)
    in_specs, args = [pl.BlockSpec(blk, lambda i: (layer, row(i), 0))], [w]
    if scale is not None:
        in_specs.append(_resident(scale.shape))
        args.append(scale)
    return dict(in_specs=in_specs, args=args, has_scale=scale is not None,
                out_spec=pl.BlockSpec(blk, lambda i: (0, row(i), 0)),
                out_shape=jax.ShapeDtypeStruct((1, r, c), BF16))


def _split_cast_refs(refs, casts, n_out):
    n_in = sum(2 if s else 1 for s in casts)
    cast_in, rest = refs[:n_in], refs[n_in:]
    outs, cast_out, scratch = rest[:n_out], rest[n_out:n_out + len(casts)], rest[n_out + len(casts):]
    jobs, k = [], 0
    for has_scale, dst in zip(casts, cast_out):
        jobs.append((cast_in[k], cast_in[k + 1] if has_scale else None, dst))
        k += 2 if has_scale else 1
    return outs, jobs, scratch


def _run_casts(jobs):
    for src, scale, dst in jobs:
        v = src[...] if scale is None else src[...] * scale[...]
        dst[...] = v.astype(dst.dtype)


def _layer_norm(x, g, b):
    mu = jnp.mean(x, axis=-1, keepdims=True)
    xc = x - mu
    var = jnp.mean(xc * xc, axis=-1, keepdims=True)
    return xc * lax.rsqrt(var + LN_EPS) * g + b


def _sigmoid(z):
    return 1.0 / (1.0 + jnp.exp(-z))


def _real_row_mask(first_row, n_rows, n_batch):
    r = first_row + lax.broadcasted_iota(jnp.int32, (n_rows, 1), 0)
    is_pad = (r < n_batch * BLOCK) & ((r & (BLOCK - 1)) < META_PAD)
    return jnp.logical_not(is_pad)


def _embed_ln_kernel(meta_ref, x_ref, g_ref, b_ref, hf_ref, hb_ref):
    is_meta = pl.program_id(0) == 0
    rows = lax.broadcasted_iota(jnp.int32, (BLOCK, 1), 0)
    keep = jnp.logical_or(jnp.logical_not(is_meta), rows >= META_PAD)
    for bi in range(x_ref.shape[0]):
        xin = jnp.where(is_meta, meta_ref[...], x_ref[bi])
        y = jnp.where(keep, _layer_norm(xin, g_ref[...], b_ref[...]), 0.0)
        hf_ref[bi * BLOCK:(bi + 1) * BLOCK, :] = y
        hb_ref[bi * BLOCK:(bi + 1) * BLOCK, :] = y.astype(BF16)


def _embed_ln(x, meta_pad, g, b):
    bsz, seq, d = x.shape
    nb = seq // BLOCK + 1
    t = bsz * nb * BLOCK
    tile = pl.BlockSpec((bsz * BLOCK, d), lambda n: (n, 0))
    return pl.pallas_call(
        _embed_ln_kernel,
        grid=(nb,),
        in_specs=[
            _resident(meta_pad.shape),
            pl.BlockSpec((bsz, BLOCK, d), lambda n: (0, jnp.maximum(n - 1, 0), 0)),
            _resident(g.shape), _resident(b.shape),
        ],
        out_specs=[tile, tile],
        out_shape=[jax.ShapeDtypeStruct((t, d), F32), jax.ShapeDtypeStruct((t, d), BF16)],
        compiler_params=_cparams("parallel"),
        name="embed_ln",
    )(meta_pad, x, g, b)


GATE_COL_CHUNK = 1024


def _in_proj_kernel(x_ref, w_ref, gb_ref, qkv_ref, gate_ref):
    x = x_ref[...]
    qkv_ref[...] = jnp.dot(x, w_ref[:, :QKV_WIDTH], preferred_element_type=F32).astype(qkv_ref.dtype)
    gate0 = w_ref.shape[1] - gate_ref.shape[1]
    for c in range(0, gate_ref.shape[1], GATE_COL_CHUNK):
        z = jnp.dot(x, w_ref[:, gate0 + c:gate0 + c + GATE_COL_CHUNK], preferred_element_type=F32)
        z = z + gb_ref[:, c:c + GATE_COL_CHUNK]
        gate_ref[:, c:c + GATE_COL_CHUNK] = _sigmoid(z).astype(gate_ref.dtype)


def _in_proj(hb, w_in, w_layer, gate_bias, layer, tm):
    t, d = hb.shape
    gate_w = gate_bias.shape[2]
    row = lambda w: pl.BlockSpec((tm, w), lambda i: (i, 0))
    return pl.pallas_call(
        _in_proj_kernel,
        grid=(t // tm,),
        in_specs=[row(d), _layer_spec(w_in.shape, w_layer), _layer_spec(gate_bias.shape, layer)],
        out_specs=[row(QKV_WIDTH), row(gate_w)],
        out_shape=[jax.ShapeDtypeStruct((t, QKV_WIDTH), BF16), jax.ShapeDtypeStruct((t, gate_w), BF16)],
        compiler_params=_cparams("parallel"),
        name="in_proj",
    )(hb, w_in, gate_bias)


def _attn_kernel(sink_ref, q_ref, kvm_ref, kvp_ref, kvc_ref, bias_ref, o_ref, *, layer):
    kv_m = kvm_ref[...]
    kv_w = jnp.concatenate([kvp_ref[...], kvc_ref[...]], axis=0)
    rows = Q_PER_KV * BLOCK
    low = lax.broadcasted_iota(jnp.int32, (1, LANES), 1) < HEAD_DIM
    head_of_row = lax.broadcasted_iota(jnp.int32, (rows, 1), 0) // BLOCK
    zero = jnp.zeros((), q_ref.dtype)
    contract_last = (((1,), (1,)), ((), ()))

    def dup_keys(kv, kh):
        k = kv[:, kh * HEAD_DIM:(kh + 1) * HEAD_DIM]
        return jnp.concatenate([k, k], axis=1)

    def dup_values(kv, kh):
        v = kv[:, KV_WIDTH + kh * HEAD_DIM:KV_WIDTH + (kh + 1) * HEAD_DIM]
        return jnp.concatenate([v, v, jnp.ones((kv.shape[0], LANES), kv.dtype)], axis=1)

    def scores(kh):
        cols = [q_ref[:, (kh * Q_PER_KV + 2 * c) * HEAD_DIM:(kh * Q_PER_KV + 2 * c + 2) * HEAD_DIM]
                for c in range(Q_PER_KV // 2)]
        qs = jnp.concatenate([jnp.where(low, cols[g // 2], zero) if g % 2 == 0 else jnp.where(low, zero, cols[g // 2])
                              for g in range(Q_PER_KV)], axis=0)
        s_m = lax.dot_general(qs, dup_keys(kv_m, kh), contract_last, preferred_element_type=F32)
        s_w = lax.dot_general(qs, dup_keys(kv_w, kh), contract_last, preferred_element_type=F32)
        return s_m + bias_ref[kh, :, :BLOCK], s_w + bias_ref[kh, :, BLOCK:]

    def finish(kh, s_m, s_w):
        sink = jnp.zeros((rows, 1), F32)
        for g in range(Q_PER_KV):
            sink = jnp.where(head_of_row == g, sink_ref[layer, kh * Q_PER_KV + g], sink)
        m = jnp.maximum(jnp.maximum(jnp.max(s_m, axis=-1, keepdims=True), jnp.max(s_w, axis=-1, keepdims=True)), sink)
        p_m = jnp.exp2(s_m - m).astype(BF16)
        p_w = jnp.exp2(s_w - m).astype(BF16)
        pv = (jnp.dot(p_m, dup_values(kv_m, kh), preferred_element_type=F32)
              + jnp.dot(p_w, dup_values(kv_w, kh), preferred_element_type=F32))
        denom = pv[:, 2 * HEAD_DIM:] + jnp.exp2(sink - m)
        o = pv[:, :2 * HEAD_DIM] * (1.0 / denom)
        for c in range(Q_PER_KV // 2):
            pair = jnp.where(low, o[2 * c * BLOCK:(2 * c + 1) * BLOCK], o[(2 * c + 1) * BLOCK:(2 * c + 2) * BLOCK])
            o_ref[:, (kh * Q_PER_KV + 2 * c) * HEAD_DIM:(kh * Q_PER_KV + 2 * c + 2) * HEAD_DIM] = pair.astype(o_ref.dtype)

    pending = scores(0)
    for kh in range(N_KV_HEADS):
        ahead = scores(kh + 1) if kh + 1 < N_KV_HEADS else None
        finish(kh, *pending)
        pending = ahead


def _attention(qkv, bias_tables, sinks, layer, bsz, nb):
    t = qkv.shape[0]
    kv_col = ATTN_WIDTH // (2 * KV_WIDTH)
    kv_spec = lambda row_fn: pl.BlockSpec((BLOCK, 2 * KV_WIDTH), lambda n, b: (row_fn(n, b), kv_col))
    return pl.pallas_call(
        functools.partial(_attn_kernel, layer=layer),
        grid=(nb, bsz),
        in_specs=[
            pl.BlockSpec(memory_space=pltpu.SMEM),
            pl.BlockSpec((BLOCK, ATTN_WIDTH), lambda n, b: (n * bsz + b, 0)),
            kv_spec(lambda n, b: b),
            kv_spec(lambda n, b: jnp.maximum(n - 1, 0) * bsz + b),
            kv_spec(lambda n, b: n * bsz + b),
            pl.BlockSpec((None, N_KV_HEADS, Q_PER_KV * BLOCK, 3 * BLOCK),
                         lambda n, b: (jnp.minimum(n, 2), 0, 0, 0)),
        ],
        out_specs=pl.BlockSpec((BLOCK, ATTN_WIDTH), lambda n, b: (n * bsz + b, 0)),
        out_shape=jax.ShapeDtypeStruct((t, ATTN_WIDTH), BF16),
        compiler_params=_cparams("parallel", "parallel"),
        name="swa",
    )(sinks, qkv, qkv, qkv, qkv, bias_tables)


def _t5_bucket(dist):
    n = jnp.maximum(dist, 0)
    max_exact = N_BUCKETS // 2
    nf = jnp.maximum(n, 1).astype(F32)
    large = max_exact + (jnp.log(nf / max_exact) / math.log(MAX_DISTANCE / max_exact)
                         * (N_BUCKETS - max_exact)).astype(jnp.int32)
    large = jnp.minimum(large, N_BUCKETS - 1)
    return jnp.where(n < max_exact, n, large)


def _bias_tables(rel_bias):
    blk = jnp.arange(3, dtype=jnp.int32)[:, None]
    j = jnp.arange(BLOCK, dtype=jnp.int32)[None, :]
    q_pos = blk * BLOCK + j
    k_pos = jnp.concatenate([jnp.broadcast_to(j, (3, BLOCK)), (blk - 1) * BLOCK + j, q_pos], axis=1)
    dist = q_pos[:, :, None] - k_pos[:, None, :]
    kp = k_pos[:, None, :]
    is_meta_seg = (jnp.arange(3 * BLOCK) < BLOCK)[None, None, :]
    valid = (dist >= 0) & jnp.where(is_meta_seg, kp >= META_PAD, (kp >= BLOCK) & (dist < BLOCK))
    onehot = jax.nn.one_hot(_t5_bucket(dist), N_BUCKETS, dtype=F32)
    bias = jnp.einsum("tqkb,bh->thqk", onehot, rel_bias.astype(F32), precision=lax.Precision.HIGHEST)
    bias = jnp.where(valid[:, None], bias * LOG2_E, NEG_INF)
    return bias.reshape(3, N_KV_HEADS, Q_PER_KV * BLOCK, 3 * BLOCK)


def _s5_discretize_kernel(lr_ref, li_ref, ls_ref, bre_ref, bim_ref, ar_ref, ai_ref, bbr_ref, bbi_ref):
    lr, li = lr_ref[...], li_ref[...]
    dt = jnp.exp(ls_ref[...])
    decay = jnp.exp(lr * dt)
    ar, ai = decay * jnp.cos(li * dt), decay * jnp.sin(li * dt)
    den = lr * lr + li * li
    nr, ni = ar - 1.0, ai
    zr = (nr * lr + ni * li) / den
    zi = (ni * lr - nr * li) / den
    bre, bim = bre_ref[...], bim_ref[...]
    ar_ref[...] = ar
    ai_ref[...] = ai
    bbr_ref[...] = zr * bre - zi * bim
    bbi_ref[...] = zr * bim + zi * bre


def _s5_discretize(lam_re, lam_im, log_step, b_re, b_im):
    depth, g, n, p = b_re.shape
    flat = lambda a: a.astype(F32).reshape(depth * g, n * p)
    rep = lambda a: flat(jnp.broadcast_to(a.astype(F32)[..., None], (depth, g, n, p)))
    ls = flat(jnp.broadcast_to(log_step.astype(F32)[:, :, None, None], (depth, g, n, p)))
    shp = jax.ShapeDtypeStruct((depth * g, n * p), F32)
    ar, ai, bbr, bbi = pl.pallas_call(
        _s5_discretize_kernel, out_shape=[shp] * 4, name="s5_discretize",
    )(rep(lam_re), rep(lam_im), ls, flat(b_re), flat(b_im))
    state = lambda a: a.reshape(depth, g, n, p)[..., 0]
    return state(ar), state(ai), bbr.reshape(depth, g, n, p), bbi.reshape(depth, g, n, p)


def _s5_scan_weights(ar, ai, bbr, bbi, c_re, c_im, d_skip):
    depth, g, n, p = bbr.shape
    halves, q_per, gl_per = 2, CHUNKS_PER_SLAB, GROUPS_PER_CHUNK
    gi_per = q_per * gl_per
    slabs = g // (halves * gi_per)
    n_ch = halves * gi_per * p
    gi_of_ch = (jnp.arange(n_ch) // p) % gi_per
    lane_blocks = [(q, ri, gl) for q in range(q_per) for ri in range(2) for gl in range(gl_per)]

    b_ri = jnp.stack([bbr, bbi], axis=1).reshape(depth, 2, halves, slabs, gi_per, n, p)
    b_rows = b_ri.transpose(0, 3, 1, 2, 4, 6, 5).reshape(depth, slabs, 2, n_ch, n)
    wb = jnp.concatenate([jnp.where((gi_of_ch == gl_per * q + gl)[:, None], b_rows[:, :, ri], 0.0)
                          for q, ri, gl in lane_blocks], axis=-1)

    c_ri = jnp.stack([c_re.astype(F32), -c_im.astype(F32)], axis=1).reshape(depth, 2, halves, slabs, gi_per, p, n)
    c_cols = c_ri.transpose(0, 3, 1, 6, 2, 4, 5).reshape(depth, slabs, 2, n, n_ch)
    wc = jnp.concatenate([jnp.where((gi_of_ch == gl_per * q + gl)[None, :], c_cols[:, :, ri], 0.0)
                          for q, ri, gl in lane_blocks], axis=-2)

    def per_seq(a):
        return jnp.repeat(a.astype(F32).reshape(depth, halves, -1), SSM_SEQS // halves, axis=1)

    return wb.astype(BF16), wc.astype(BF16), per_seq(ar), per_seq(ai), per_seq(d_skip)


SCAN_SUB_BLOCKS = 4


def _s5_scan_kernel(x_ref, wu0_ref, wu1_ref, wb_ref, wc_ref, ar_ref, ai_ref, d_ref, *refs, n_batch, casts):
    (y_ref,), cast_jobs, (u_scr, s_ref, y_scr, h_ref) = _split_cast_refs(refs, casts, 1)
    steps = BLOCK
    sub_steps = steps // SCAN_SUB_BLOCKS
    sub_rows = sub_steps * SSM_SEQS
    n_slabs = wb_ref.shape[0]
    half_w = n_slabs * LANES

    @pl.when(pl.program_id(0) == 0)
    def _():
        h_ref[...] = jnp.zeros_like(h_ref)

    def seq_rows(q, b, half):
        return pl.ds(q * sub_rows + half * n_batch + b, sub_steps, stride=SSM_SEQS)

    def block_rows(q):
        return slice(q * sub_rows, (q + 1) * sub_rows)

    def channels(half, j):
        return slice(half * half_w + j * LANES, half * half_w + (j + 1) * LANES)

    half0 = (lax.broadcasted_iota(jnp.int32, (sub_rows, 1), 0) & (SSM_SEQS - 1)) < (SSM_SEQS // 2)

    def project_u(q):
        x = jnp.concatenate([x_ref[b * steps + q * sub_steps:b * steps + (q + 1) * sub_steps, :]
                             for b in range(n_batch)], axis=0)
        u_half = [jnp.dot(x, w[...], preferred_element_type=F32) for w in (wu0_ref, wu1_ref)]
        for b in range(n_batch):
            for half in range(2):
                for j in range(n_slabs):
                    u_scr[j, seq_rows(q, b, half), :] = (
                        u_half[half][b * sub_steps:(b + 1) * sub_steps, j * LANES:(j + 1) * LANES])

    def drive(q, j):
        uj = u_scr[j, block_rows(q), :]
        lhs = jnp.concatenate([jnp.where(half0, uj, 0.0), jnp.where(half0, 0.0, uj)], axis=1).astype(BF16)
        s_ref[block_rows(q), j * SLAB_STATE_LANES:(j + 1) * SLAB_STATE_LANES] = jnp.dot(
            lhs, wb_ref[j], preferred_element_type=F32)

    def recur(q, j):
        chunks = range(j * CHUNKS_PER_SLAB, (j + 1) * CHUNKS_PER_SLAB)
        a_r = [ar_ref[:, c * LANES:(c + 1) * LANES] for c in chunks]
        a_i = [ai_ref[:, c * LANES:(c + 1) * LANES] for c in chunks]
        init = []
        for c in chunks:
            init += [h_ref[:, 2 * c * LANES:(2 * c + 1) * LANES], h_ref[:, (2 * c + 1) * LANES:(2 * c + 2) * LANES]]

        state = init
        for t in range(q * sub_steps, (q + 1) * sub_steps):
            rows_t = slice(t * SSM_SEQS, (t + 1) * SSM_SEQS)
            new = []
            for k, c in enumerate(chunks):
                hr, hi = state[2 * k], state[2 * k + 1]
                re_sl = (rows_t, slice(2 * c * LANES, (2 * c + 1) * LANES))
                im_sl = (rows_t, slice((2 * c + 1) * LANES, (2 * c + 2) * LANES))
                nr = a_r[k] * hr - a_i[k] * hi + s_ref[re_sl]
                ni = a_r[k] * hi + a_i[k] * hr + s_ref[im_sl]
                s_ref[re_sl] = nr
                s_ref[im_sl] = ni
                new += [nr, ni]
            state = new
        for k, c in enumerate(chunks):
            h_ref[:, 2 * c * LANES:(2 * c + 1) * LANES] = state[2 * k]
            h_ref[:, (2 * c + 1) * LANES:(2 * c + 2) * LANES] = state[2 * k + 1]

    def readout(q, j):
        hs = s_ref[block_rows(q), j * SLAB_STATE_LANES:(j + 1) * SLAB_STATE_LANES].astype(BF16)
        both = jnp.dot(hs, wc_ref[j], preferred_element_type=F32)
        yj = jnp.where(half0, both[:, :LANES], both[:, LANES:])
        dj = jnp.tile(d_ref[:, j * LANES:(j + 1) * LANES], (sub_steps, 1))
        y_scr[j, block_rows(q), :] = jax.nn.gelu(yj + dj * u_scr[j, block_rows(q), :])

    def emit(q):
        for b in range(n_batch):
            for half in range(2):
                for j in range(n_slabs):
                    y_ref[b * steps + q * sub_steps:b * steps + (q + 1) * sub_steps, channels(half, j)] = (
                        y_scr[j, seq_rows(q, b, half), :].astype(y_ref.dtype))

    n_sub = SCAN_SUB_BLOCKS
    for s in range(n_sub + 3):
        if s < n_sub:
            project_u(s)
        for j in range(n_slabs):
            if 0 <= s - 1 < n_sub:
                drive(s - 1, j)
            if 0 <= s - 2 < n_sub:
                recur(s - 2, j)
            if 0 <= s - 3 < n_sub:
                readout(s - 3, j)
        if 0 <= s - 3 < n_sub:
            emit(s - 3)
    _run_casts(cast_jobs)


def _s5_scan(hb, w_in, w_layer, wb, wc, a_r, a_i, d_rows, layer, n_batch, cast_jobs=()):
    t, d = hb.shape
    tile = n_batch * BLOCK
    rows = BLOCK * SSM_SEQS
    n_slabs = wb.shape[1]
    state_lanes = n_slabs * SLAB_STATE_LANES
    half_w = n_slabs * LANES
    ssm_w = 2 * half_w
    u_block = QKV_WIDTH // half_w
    assert QKV_WIDTH % half_w == 0
    u_spec = lambda half: pl.BlockSpec((None, d, half_w), lambda i: (w_layer, 0, u_block + half),
                                       pipeline_mode=pl.Buffered(1))
    return pl.pallas_call(
        functools.partial(_s5_scan_kernel, n_batch=n_batch,
                          casts=tuple(job["has_scale"] for job in cast_jobs)),
        grid=(t // tile,),
        in_specs=[
            pl.BlockSpec((tile, d), lambda i: (i, 0)),
            u_spec(0), u_spec(1), _layer_spec(wb.shape, layer), _layer_spec(wc.shape, layer),
            _layer_spec(a_r.shape, layer), _layer_spec(a_i.shape, layer), _layer_spec(d_rows.shape, layer),
        ] + [s for job in cast_jobs for s in job["in_specs"]],
        out_specs=[pl.BlockSpec((tile, ssm_w), lambda i: (i, 0))] + [job["out_spec"] for job in cast_jobs],
        out_shape=[jax.ShapeDtypeStruct((t, ssm_w), BF16)] + [job["out_shape"] for job in cast_jobs],
        scratch_shapes=[pltpu.VMEM((n_slabs, rows, LANES), F32), pltpu.VMEM((rows, state_lanes), F32),
                        pltpu.VMEM((n_slabs, rows, LANES), F32), pltpu.VMEM((SSM_SEQS, state_lanes), F32)],
        compiler_params=_cparams("arbitrary"),
        name="s5_scan",
    )(hb, w_in, w_in, wb, wc, a_r, a_i, d_rows, *[a for job in cast_jobs for a in job["args"]])


def _mix_kernel(ya_ref, ys_ref, gate_ref, h_ref, wglu_ref, wau_ref, wsu_ref, wout_ref, g_ref, b_ref,
                *refs, alpha, n_batch, casts):
    (hf_ref, hb_ref), cast_jobs, _ = _split_cast_refs(refs, casts, 2)
    tm, d = h_ref.shape
    y = ys_ref[...]
    glu = jnp.dot(y, wglu_ref[...], preferred_element_type=F32)
    y_ssm = (y.astype(F32) * _sigmoid(glu)).astype(BF16)
    up_a = jnp.dot(ya_ref[...], wau_ref[...], preferred_element_type=F32)
    up_s = jnp.dot(y_ssm, wsu_ref[...], preferred_element_type=F32)
    merged = gate_ref[:, :d].astype(F32) * up_a + gate_ref[:, d:].astype(F32) * up_s
    mixed = jnp.dot(merged.astype(BF16), wout_ref[...], preferred_element_type=F32)
    out = _layer_norm(alpha * h_ref[...] + mixed, g_ref[...], b_ref[...])
    out = jnp.where(_real_row_mask(pl.program_id(0) * tm, tm, n_batch), out, 0.0)
    hf_ref[...] = out
    hb_ref[...] = out.astype(BF16)
    _run_casts(cast_jobs)


MIX_ROWS = 2 * BLOCK


def _mix(y_attn, y_ssm, gates, h, wglu, wau, wsu, wout, g, b, layer, alpha, n_batch, cast_jobs=()):
    t, d = h.shape
    tm = MIX_ROWS
    row = lambda w: pl.BlockSpec((tm, w), lambda i: (i, 0))
    return pl.pallas_call(
        functools.partial(_mix_kernel, alpha=alpha, n_batch=n_batch,
                          casts=tuple(job["has_scale"] for job in cast_jobs)),
        grid=(t // tm,),
        in_specs=[row(y_attn.shape[1]), row(y_ssm.shape[1]), row(gates.shape[1]), row(d),
                  _layer_spec(wglu.shape, 0), _layer_spec(wau.shape, 0), _layer_spec(wsu.shape, 0),
                  _layer_spec(wout.shape, 0), _layer_spec(g.shape, layer), _layer_spec(b.shape, layer)]
                 + [s for job in cast_jobs for s in job["in_specs"]],
        out_specs=[row(d), row(d)] + [job["out_spec"] for job in cast_jobs],
        out_shape=[jax.ShapeDtypeStruct((t, d), F32), jax.ShapeDtypeStruct((t, d), BF16)]
                  + [job["out_shape"] for job in cast_jobs],
        compiler_params=_cparams("arbitrary"),
        name="mix_ln",
    )(y_attn, y_ssm, gates, h, wglu, wau, wsu, wout, g, b, *[a for job in cast_jobs for a in job["args"]])


def _mlp_kernel(xb_ref, xf_ref, wup_ref, wdn_ref, g_ref, b_ref, *refs, alpha, n_batch, final):
    out_refs, acc_ref = refs[:-1], refs[-1]
    i, j = pl.program_id(0), pl.program_id(1)
    tm, d = xb_ref.shape

    last = pl.num_programs(1) - 1

    def accumulate(r, first=False):
        a = jnp.maximum(jnp.dot(xb_ref[r, :], wup_ref[...], preferred_element_type=F32), 0.0)
        part = jnp.dot((a * a).astype(BF16), wdn_ref[...], preferred_element_type=F32)
        acc_ref[r, :] = (alpha * xf_ref[r, :] if first else acc_ref[r, :]) + part

    @pl.when(j == 0)
    def _():
        accumulate(slice(None), first=True)

    @pl.when((j > 0) & (j < last))
    def _():
        accumulate(slice(None))

    @pl.when(j == last)
    def _():
        half = tm // 2
        for r0 in (0, half):
            r = slice(r0, r0 + half)
            accumulate(r)
            out = _layer_norm(acc_ref[r, :], g_ref[...], b_ref[...])
            if final:
                out_refs[0][r0 // BLOCK:(r0 + half) // BLOCK] = out.reshape(half // BLOCK, BLOCK, d)
            else:
                out = jnp.where(_real_row_mask(i * tm + r0, half, n_batch), out, 0.0)
                out_refs[0][r, :] = out
                out_refs[1][r, :] = out.astype(BF16)


def _mlp(hb, hf, wup, wdn, g, b, layer, alpha, n_batch, final):
    t, d = hf.shape
    dff = wup.shape[2]
    tm = n_batch * BLOCK
    tf = 1024
    assert dff % tf == 0 and dff // tf >= 2
    first = 1 if final else 0
    row = lambda: pl.BlockSpec((tm, d), lambda i, j: (i + first, 0))
    if final:
        out_specs = [pl.BlockSpec((n_batch, BLOCK, d), lambda i, j: (0, i, 0))]
        out_shape = [jax.ShapeDtypeStruct((n_batch, t // n_batch - BLOCK, d), F32)]
    else:
        out_specs = [row(), row()]
        out_shape = [jax.ShapeDtypeStruct((t, d), F32), jax.ShapeDtypeStruct((t, d), BF16)]
    return pl.pallas_call(
        functools.partial(_mlp_kernel, alpha=alpha, n_batch=n_batch, final=final),
        grid=(t // tm - first, dff // tf),
        in_specs=[row(), row(),
                  pl.BlockSpec((None, d, tf), lambda i, j: (0, 0, j)),
                  pl.BlockSpec((None, tf, d), lambda i, j: (0, j, 0)),
                  _layer_spec(g.shape, layer), _layer_spec(b.shape, layer)],
        out_specs=out_specs,
        out_shape=out_shape,
        scratch_shapes=[pltpu.VMEM((tm, d), F32)],
        compiler_params=_cparams("parallel", "arbitrary"),
        name="mlp_ln",
    )(hb, hf, wup, wdn, g, b)


def kernel(x, meta_tokens, ln_emb_g, ln_emb_b, rel_bias, in_proj, gate_b, attn_sinks, ssm_lambda_re,
           ssm_lambda_im, ssm_log_step, ssm_b_re, ssm_b_im, ssm_c_re, ssm_c_im, ssm_d, ssm_w_glu, w_attn_up,
           w_ssm_up, w_out, ln_mix_g, ln_mix_b, w_mlp_up, w_mlp_down, ln_mlp_g, ln_mlp_b):
    bsz, seq, d = x.shape
    depth = in_proj.shape[0]
    assert seq % BLOCK == 0 and 2 * bsz == SSM_SEQS and meta_tokens.shape[0] == N_META
    nb = seq // BLOCK + 1
    alpha = (2 * depth) ** 0.25
    row2 = lambda a: a.astype(F32).reshape(1, -1)
    rows = lambda a: a.astype(F32)[:, None, :]
    bf = lambda a: a.astype(BF16)

    meta_pad = jnp.concatenate([jnp.zeros((META_PAD, d), x.dtype), meta_tokens.astype(x.dtype)], axis=0)
    hf, hb = _embed_ln(x, meta_pad, row2(ln_emb_g), row2(ln_emb_b))
    bias_tables = _bias_tables(rel_bias)

    q_scale = jnp.where(jnp.arange(in_proj.shape[2]) < ATTN_WIDTH, HEAD_DIM ** -0.5 * LOG2_E, 1.0)
    q_scale = q_scale.astype(in_proj.dtype)[None, :]
    sinks = attn_sinks.astype(F32) * LOG2_E
    ar, ai, bbr, bbi = _s5_discretize(ssm_lambda_re, ssm_lambda_im, ssm_log_step, ssm_b_re, ssm_b_im)
    wb, wc, a_r, a_i, d_rows = _s5_scan_weights(ar, ai, bbr, bbi, ssm_c_re, ssm_c_im, ssm_d)

    t = hf.shape[0]
    scan_steps, mix_steps = t // (bsz * BLOCK), t // MIX_ROWS
    w_in = bf(in_proj[:1] * q_scale)

    for l in range(depth):
        final = l == depth - 1
        qkv, gates = _in_proj(hb, w_in, 0, rows(gate_b), l, bsz * BLOCK)
        y_attn = _attention(qkv, bias_tables, sinks, l, bsz, nb)
        scan_jobs = [_cast_job(w, l, scan_steps) for w in (w_mlp_up, ssm_w_glu, w_attn_up, w_ssm_up, w_out)]
        y_gelu, w_up, w_glu, w_au, w_su, w_o = _s5_scan(hb, w_in, 0, wb, wc, a_r, a_i, d_rows, l, bsz,
                                                        cast_jobs=scan_jobs)
        mix_jobs = [_cast_job(w_mlp_down, l, mix_steps)]
        if not final:
            mix_jobs.append(_cast_job(in_proj, l + 1, mix_steps, scale=q_scale))
        hf, hb, w_dn, *w_next = _mix(y_attn, y_gelu, gates, hf, w_glu, w_au, w_su, w_o, rows(ln_mix_g),
                                     rows(ln_mix_b), l, alpha, bsz, cast_jobs=mix_jobs)
        out = _mlp(hb, hf, w_up, w_dn, rows(ln_mlp_g), rows(ln_mlp_b), l, alpha, bsz, final)
        if not final:
            (hf, hb), (w_in,) = out, w_next

    return out[0]
```

```python
import functools
import math

import jax
import jax.numpy as jnp
from jax import lax
from jax.experimental import pallas as pl
from jax.experimental.pallas import tpu as pltpu

F32 = jnp.float32
BF16 = jnp.bfloat16

N_META = 16
BLOCK = 128
META_PAD = BLOCK - N_META
HEAD_DIM = 64
N_Q_HEADS = 16
N_KV_HEADS = 4
Q_PER_KV = N_Q_HEADS // N_KV_HEADS
ATTN_WIDTH = N_Q_HEADS * HEAD_DIM
KV_WIDTH = N_KV_HEADS * HEAD_DIM
QKV_WIDTH = ATTN_WIDTH + 2 * KV_WIDTH
SSM_STATE = 64
N_BUCKETS = 32
MAX_DISTANCE = 128
LN_EPS = 1e-5
NEG_INF = -1e30
LOG2_E = math.log2(math.e)

LANES = 128
SUBLANES = 8
V7X_VMEM_BYTES = 64 * 1024 * 1024
VMEM_LIMIT_BYTES = V7X_VMEM_BYTES - 8 * 1024 * 1024

SSM_SEQS = SUBLANES
STATES_PER_CHUNK = LANES
GROUPS_PER_CHUNK = STATES_PER_CHUNK // SSM_STATE
CHUNKS_PER_SLAB = 4
SLAB_STATE_LANES = CHUNKS_PER_SLAB * 2 * STATES_PER_CHUNK


def _cparams(*sem):
    return pltpu.CompilerParams(dimension_semantics=sem, vmem_limit_bytes=VMEM_LIMIT_BYTES)


def _resident(shape):
    nd = len(shape)
    return pl.BlockSpec(shape, lambda *_: (0,) * nd, pipeline_mode=pl.Buffered(1))


def _layer_spec(shape, layer):
    nd = len(shape)
    return pl.BlockSpec((None,) + tuple(shape[1:]), lambda *_: (layer,) + (0,) * (nd - 1),
                        pipeline_mode=pl.Buffered(1))


def _even_parts(total, max_parts, align):
    return max(p for p in range(1, max_parts + 1) if total % p == 0 and (total // p) % align == 0)


def _cast_job(w, layer, n_steps, scale=None):
    _, r, c = w.shape
    parts = _even_parts(r, n_steps, 2 * SUBLANES)
    blk = (None, r // parts, c)
    row = lambda i: jnp.minimum(i, parts - 1)
    in_specs, args = [pl.BlockSpec(blk, lambda i: (layer, row(i), 0))], [w]
    if scale is not None:
        in_specs.append(_resident(scale.shape))
        args.append(scale)
    return dict(in_specs=in_specs, args=args, has_scale=scale is not None,
                out_spec=pl.BlockSpec(blk, lambda i: (0, row(i), 0)),
                out_shape=jax.ShapeDtypeStruct((1, r, c), BF16))


def _split_cast_refs(refs, casts, n_out):
    n_in = sum(2 if s else 1 for s in casts)
    cast_in, rest = refs[:n_in], refs[n_in:]
    outs, cast_out, scratch = rest[:n_out], rest[n_out:n_out + len(casts)], rest[n_out + len(casts):]
    jobs, k = [], 0
    for has_scale, dst in zip(casts, cast_out):
        jobs.append((cast_in[k], cast_in[k + 1] if has_scale else None, dst))
        k += 2 if has_scale else 1
    return outs, jobs, scratch


def _run_casts(jobs):
    for src, scale, dst in jobs:
        v = src[...] if scale is None else src[...] * scale[...]
        dst[...] = v.astype(dst.dtype)


def _layer_norm(x, g, b):
    mu = jnp.mean(x, axis=-1, keepdims=True)
    xc = x - mu
    var = jnp.mean(xc * xc, axis=-1, keepdims=True)
    return xc * lax.rsqrt(var + LN_EPS) * g + b


def _sigmoid(z):
    return 1.0 / (1.0 + jnp.exp(-z))


def _real_row_mask(first_row, n_rows, n_batch):
    r = first_row + lax.broadcasted_iota(jnp.int32, (n_rows, 1), 0)
    is_pad = (r < n_batch * BLOCK) & ((r & (BLOCK - 1)) < META_PAD)
    return jnp.logical_not(is_pad)


def _embed_ln_kernel(meta_ref, x_ref, g_ref, b_ref, hf_ref, hb_ref):
    is_meta = pl.program_id(0) == 0
    rows = lax.broadcasted_iota(jnp.int32, (BLOCK, 1), 0)
    keep = jnp.logical_or(jnp.logical_not(is_meta), rows >= META_PAD)
    for bi in range(x_ref.shape[0]):
        xin = jnp.where(is_meta, meta_ref[...], x_ref[bi])
        y = jnp.where(keep, _layer_norm(xin, g_ref[...], b_ref[...]), 0.0)
        hf_ref[bi * BLOCK:(bi + 1) * BLOCK, :] = y
        hb_ref[bi * BLOCK:(bi + 1) * BLOCK, :] = y.astype(BF16)


def _embed_ln(x, meta_pad, g, b):
    bsz, seq, d = x.shape
    nb = seq // BLOCK + 1
    t = bsz * nb * BLOCK
    tile = pl.BlockSpec((bsz * BLOCK, d), lambda n: (n, 0))
    return pl.pallas_call(
        _embed_ln_kernel,
        grid=(nb,),
        in_specs=[
            _resident(meta_pad.shape),
            pl.BlockSpec((bsz, BLOCK, d), lambda n: (0, jnp.maximum(n - 1, 0), 0)),
            _resident(g.shape), _resident(b.shape),
        ],
        out_specs=[tile, tile],
        out_shape=[jax.ShapeDtypeStruct((t, d), F32), jax.ShapeDtypeStruct((t, d), BF16)],
        compiler_params=_cparams("parallel"),
        name="embed_ln",
    )(meta_pad, x, g, b)


GATE_COL_CHUNK = 1024


def _in_proj_kernel(x_ref, w_ref, gb_ref, qkv_ref, gate_ref):
    x = x_ref[...]
    qkv_ref[...] = jnp.dot(x, w_ref[:, :QKV_WIDTH], preferred_element_type=F32).astype(qkv_ref.dtype)
    gate0 = w_ref.shape[1] - gate_ref.shape[1]
    for c in range(0, gate_ref.shape[1], GATE_COL_CHUNK):
        z = jnp.dot(x, w_ref[:, gate0 + c:gate0 + c + GATE_COL_CHUNK], preferred_element_type=F32)
        z = z + gb_ref[:, c:c + GATE_COL_CHUNK]
        gate_ref[:, c:c + GATE_COL_CHUNK] = _sigmoid(z).astype(gate_ref.dtype)


def _in_proj(hb, w_in, w_layer, gate_bias, layer, tm):
    t, d = hb.shape
    gate_w = gate_bias.shape[2]
    row = lambda w: pl.BlockSpec((tm, w), lambda i: (i, 0))
    return pl.pallas_call(
        _in_proj_kernel,
        grid=(t // tm,),
        in_specs=[row(d), _layer_spec(w_in.shape, w_layer), _layer_spec(gate_bias.shape, layer)],
        out_specs=[row(QKV_WIDTH), row(gate_w)],
        out_shape=[jax.ShapeDtypeStruct((t, QKV_WIDTH), BF16), jax.ShapeDtypeStruct((t, gate_w), BF16)],
        compiler_params=_cparams("parallel"),
        name="in_proj",
    )(hb, w_in, gate_bias)


def _attn_kernel(sink_ref, q_ref, kvm_ref, kvp_ref, kvc_ref, bias_ref, o_ref, *, layer):
    kv_m = kvm_ref[...]
    kv_w = jnp.concatenate([kvp_ref[...], kvc_ref[...]], axis=0)
    rows = Q_PER_KV * BLOCK
    low = lax.broadcasted_iota(jnp.int32, (1, LANES), 1) < HEAD_DIM
    head_of_row = lax.broadcasted_iota(jnp.int32, (rows, 1), 0) // BLOCK
    zero = jnp.zeros((), q_ref.dtype)
    contract_last = (((1,), (1,)), ((), ()))

    def dup_keys(kv, kh):
        k = kv[:, kh * HEAD_DIM:(kh + 1) * HEAD_DIM]
        return jnp.concatenate([k, k], axis=1)

    def dup_values(kv, kh):
        v = kv[:, KV_WIDTH + kh * HEAD_DIM:KV_WIDTH + (kh + 1) * HEAD_DIM]
        return jnp.concatenate([v, v, jnp.ones((kv.shape[0], LANES), kv.dtype)], axis=1)

    def scores(kh):
        cols = [q_ref[:, (kh * Q_PER_KV + 2 * c) * HEAD_DIM:(kh * Q_PER_KV + 2 * c + 2) * HEAD_DIM]
                for c in range(Q_PER_KV // 2)]
        qs = jnp.concatenate([jnp.where(low, cols[g // 2], zero) if g % 2 == 0 else jnp.where(low, zero, cols[g // 2])
                              for g in range(Q_PER_KV)], axis=0)
        s_m = lax.dot_general(qs, dup_keys(kv_m, kh), contract_last, preferred_element_type=F32)
        s_w = lax.dot_general(qs, dup_keys(kv_w, kh), contract_last, preferred_element_type=F32)
        return s_m + bias_ref[kh, :, :BLOCK], s_w + bias_ref[kh, :, BLOCK:]

    def finish(kh, s_m, s_w):
        sink = jnp.zeros((rows, 1), F32)
        for g in range(Q_PER_KV):
            sink = jnp.where(head_of_row == g, sink_ref[layer, kh * Q_PER_KV + g], sink)
        m = jnp.maximum(jnp.maximum(jnp.max(s_m, axis=-1, keepdims=True), jnp.max(s_w, axis=-1, keepdims=True)), sink)
        p_m = jnp.exp2(s_m - m).astype(BF16)
        p_w = jnp.exp2(s_w - m).astype(BF16)
        pv = (jnp.dot(p_m, dup_values(kv_m, kh), preferred_element_type=F32)
              + jnp.dot(p_w, dup_values(kv_w, kh), preferred_element_type=F32))
        denom = pv[:, 2 * HEAD_DIM:] + jnp.exp2(sink - m)
        o = pv[:, :2 * HEAD_DIM] * (1.0 / denom)
        for c in range(Q_PER_KV // 2):
            pair = jnp.where(low, o[2 * c * BLOCK:(2 * c + 1) * BLOCK], o[(2 * c + 1) * BLOCK:(2 * c + 2) * BLOCK])
            o_ref[:, (kh * Q_PER_KV + 2 * c) * HEAD_DIM:(kh * Q_PER_KV + 2 * c + 2) * HEAD_DIM] = pair.astype(o_ref.dtype)

    pending = scores(0)
    for kh in range(N_KV_HEADS):
        ahead = scores(kh + 1) if kh + 1 < N_KV_HEADS else None
        finish(kh, *pending)
        pending = ahead


def _attention(qkv, bias_tables, sinks, layer, bsz, nb):
    t = qkv.shape[0]
    kv_col = ATTN_WIDTH // (2 * KV_WIDTH)
    kv_spec = lambda row_fn: pl.BlockSpec((BLOCK, 2 * KV_WIDTH), lambda n, b: (row_fn(n, b), kv_col))
    return pl.pallas_call(
        functools.partial(_attn_kernel, layer=layer),
        grid=(nb, bsz),
        in_specs=[
            pl.BlockSpec(memory_space=pltpu.SMEM),
            pl.BlockSpec((BLOCK, ATTN_WIDTH), lambda n, b: (n * bsz + b, 0)),
            kv_spec(lambda n, b: b),
            kv_spec(lambda n, b: jnp.maximum(n - 1, 0) * bsz + b),
            kv_spec(lambda n, b: n * bsz + b),
            pl.BlockSpec((None, N_KV_HEADS, Q_PER_KV * BLOCK, 3 * BLOCK),
                         lambda n, b: (jnp.minimum(n, 2), 0, 0, 0)),
        ],
        out_specs=pl.BlockSpec((BLOCK, ATTN_WIDTH), lambda n, b: (n * bsz + b, 0)),
        out_shape=jax.ShapeDtypeStruct((t, ATTN_WIDTH), BF16),
        compiler_params=_cparams("parallel", "parallel"),
        name="swa",
    )(sinks, qkv, qkv, qkv, qkv, bias_tables)


def _t5_bucket(dist):
    n = jnp.maximum(dist, 0)
    max_exact = N_BUCKETS // 2
    nf = jnp.maximum(n, 1).astype(F32)
    large = max_exact + (jnp.log(nf / max_exact) / math.log(MAX_DISTANCE / max_exact)
                         * (N_BUCKETS - max_exact)).astype(jnp.int32)
    large = jnp.minimum(large, N_BUCKETS - 1)
    return jnp.where(n < max_exact, n, large)


def _bias_tables(rel_bias):
    blk = jnp.arange(3, dtype=jnp.int32)[:, None]
    j = jnp.arange(BLOCK, dtype=jnp.int32)[None, :]
    q_pos = blk * BLOCK + j
    k_pos = jnp.concatenate([jnp.broadcast_to(j, (3, BLOCK)), (blk - 1) * BLOCK + j, q_pos], axis=1)
    dist = q_pos[:, :, None] - k_pos[:, None, :]
    kp = k_pos[:, None, :]
    is_meta_seg = (jnp.arange(3 * BLOCK) < BLOCK)[None, None, :]
    valid = (dist >= 0) & jnp.where(is_meta_seg, kp >= META_PAD, (kp >= BLOCK) & (dist < BLOCK))
    onehot = jax.nn.one_hot(_t5_bucket(dist), N_BUCKETS, dtype=F32)
    bias = jnp.einsum("tqkb,bh->thqk", onehot, rel_bias.astype(F32), precision=lax.Precision.HIGHEST)
    bias = jnp.where(valid[:, None], bias * LOG2_E, NEG_INF)
    return bias.reshape(3, N_KV_HEADS, Q_PER_KV * BLOCK, 3 * BLOCK)


def _s5_discretize_kernel(lr_ref, li_ref, ls_ref, bre_ref, bim_ref, ar_ref, ai_ref, bbr_ref, bbi_ref):
    lr, li = lr_ref[...], li_ref[...]
    dt = jnp.exp(ls_ref[...])
    decay = jnp.exp(lr * dt)
    ar, ai = decay * jnp.cos(li * dt), decay * jnp.sin(li * dt)
    den = lr * lr + li * li
    nr, ni = ar - 1.0, ai
    zr = (nr * lr + ni * li) / den
    zi = (ni * lr - nr * li) / den
    bre, bim = bre_ref[...], bim_ref[...]
    ar_ref[...] = ar
    ai_ref[...] = ai
    bbr_ref[...] = zr * bre - zi * bim
    bbi_ref[...] = zr * bim + zi * bre


def _s5_discretize(lam_re, lam_im, log_step, b_re, b_im):
    depth, g, n, p = b_re.shape
    flat = lambda a: a.astype(F32).reshape(depth * g, n * p)
    rep = lambda a: flat(jnp.broadcast_to(a.astype(F32)[..., None], (depth, g, n, p)))
    ls = flat(jnp.broadcast_to(log_step.astype(F32)[:, :, None, None], (depth, g, n, p)))
    shp = jax.ShapeDtypeStruct((depth * g, n * p), F32)
    ar, ai, bbr, bbi = pl.pallas_call(
        _s5_discretize_kernel, out_shape=[shp] * 4, name="s5_discretize",
    )(rep(lam_re), rep(lam_im), ls, flat(b_re), flat(b_im))
    state = lambda a: a.reshape(depth, g, n, p)[..., 0]
    return state(ar), state(ai), bbr.reshape(depth, g, n, p), bbi.reshape(depth, g, n, p)


def _s5_scan_weights(ar, ai, bbr, bbi, c_re, c_im, d_skip):
    depth, g, n, p = bbr.shape
    halves, q_per, gl_per = 2, CHUNKS_PER_SLAB, GROUPS_PER_CHUNK
    gi_per = q_per * gl_per
    slabs = g // (halves * gi_per)
    n_ch = halves * gi_per * p
    gi_of_ch = (jnp.arange(n_ch) // p) % gi_per
    lane_blocks = [(q, ri, gl) for q in range(q_per) for ri in range(2) for gl in range(gl_per)]

    b_ri = jnp.stack([bbr, bbi], axis=1).reshape(depth, 2, halves, slabs, gi_per, n, p)
    b_rows = b_ri.transpose(0, 3, 1, 2, 4, 6, 5).reshape(depth, slabs, 2, n_ch, n)
    wb = jnp.concatenate([jnp.where((gi_of_ch == gl_per * q + gl)[:, None], b_rows[:, :, ri], 0.0)
                          for q, ri, gl in lane_blocks], axis=-1)

    c_ri = jnp.stack([c_re.astype(F32), -c_im.astype(F32)], axis=1).reshape(depth, 2, halves, slabs, gi_per, p, n)
    c_cols = c_ri.transpose(0, 3, 1, 6, 2, 4, 5).reshape(depth, slabs, 2, n, n_ch)
    wc = jnp.concatenate([jnp.where((gi_of_ch == gl_per * q + gl)[None, :], c_cols[:, :, ri], 0.0)
                          for q, ri, gl in lane_blocks], axis=-2)

    def per_seq(a):
        return jnp.repeat(a.astype(F32).reshape(depth, halves, -1), SSM_SEQS // halves, axis=1)

    return wb.astype(BF16), wc.astype(BF16), per_seq(ar), per_seq(ai), per_seq(d_skip)


SCAN_SUB_BLOCKS = 4


def _s5_scan_kernel(x_ref, wu0_ref, wu1_ref, wb_ref, wc_ref, ar_ref, ai_ref, d_ref, *refs, n_batch, casts):
    (y_ref,), cast_jobs, (u_scr, s_ref, y_scr, h_ref) = _split_cast_refs(refs, casts, 1)
    steps = BLOCK
    sub_steps = steps // SCAN_SUB_BLOCKS
    sub_rows = sub_steps * SSM_SEQS
    n_slabs = wb_ref.shape[0]
    half_w = n_slabs * LANES

    @pl.when(pl.program_id(0) == 0)
    def _():
        h_ref[...] = jnp.zeros_like(h_ref)

    def seq_rows(q, b, half):
        return pl.ds(q * sub_rows + half * n_batch + b, sub_steps, stride=SSM_SEQS)

    def block_rows(q):
        return slice(q * sub_rows, (q + 1) * sub_rows)

    def channels(half, j):
        return slice(half * half_w + j * LANES, half * half_w + (j + 1) * LANES)

    half0 = (lax.broadcasted_iota(jnp.int32, (sub_rows, 1), 0) & (SSM_SEQS - 1)) < (SSM_SEQS // 2)

    def project_u(q):
        x = jnp.concatenate([x_ref[b * steps + q * sub_steps:b * steps + (q + 1) * sub_steps, :]
                             for b in range(n_batch)], axis=0)
        u_half = [jnp.dot(x, w[...], preferred_element_type=F32) for w in (wu0_ref, wu1_ref)]
        for b in range(n_batch):
            for half in range(2):
                for j in range(n_slabs):
                    u_scr[j, seq_rows(q, b, half), :] = (
                        u_half[half][b * sub_steps:(b + 1) * sub_steps, j * LANES:(j + 1) * LANES])

    def drive(q, j):
        uj = u_scr[j, block_rows(q), :]
        lhs = jnp.concatenate([jnp.where(half0, uj, 0.0), jnp.where(half0, 0.0, uj)], axis=1).astype(BF16)
        s_ref[block_rows(q), j * SLAB_STATE_LANES:(j + 1) * SLAB_STATE_LANES] = jnp.dot(
            lhs, wb_ref[j], preferred_element_type=F32)

    def recur(q, j):
        chunks = range(j * CHUNKS_PER_SLAB, (j + 1) * CHUNKS_PER_SLAB)
        a_r = [ar_ref[:, c * LANES:(c + 1) * LANES] for c in chunks]
        a_i = [ai_ref[:, c * LANES:(c + 1) * LANES] for c in chunks]
        init = []
        for c in chunks:
            init += [h_ref[:, 2 * c * LANES:(2 * c + 1) * LANES], h_ref[:, (2 * c + 1) * LANES:(2 * c + 2) * LANES]]

        state = init
        for t in range(q * sub_steps, (q + 1) * sub_steps):
            rows_t = slice(t * SSM_SEQS, (t + 1) * SSM_SEQS)
            new = []
            for k, c in enumerate(chunks):
                hr, hi = state[2 * k], state[2 * k + 1]
                re_sl = (rows_t, slice(2 * c * LANES, (2 * c + 1) * LANES))
                im_sl = (rows_t, slice((2 * c + 1) * LANES, (2 * c + 2) * LANES))
                nr = a_r[k] * hr - a_i[k] * hi + s_ref[re_sl]
                ni = a_r[k] * hi + a_i[k] * hr + s_ref[im_sl]
                s_ref[re_sl] = nr
                s_ref[im_sl] = ni
                new += [nr, ni]
            state = new
        for k, c in enumerate(chunks):
            h_ref[:, 2 * c * LANES:(2 * c + 1) * LANES] = state[2 * k]
            h_ref[:, (2 * c + 1) * LANES:(2 * c + 2) * LANES] = state[2 * k + 1]

    def readout(q, j):
        hs = s_ref[block_rows(q), j * SLAB_STATE_LANES:(j + 1) * SLAB_STATE_LANES].astype(BF16)
        both = jnp.dot(hs, wc_ref[j], preferred_element_type=F32)
        yj = jnp.where(half0, both[:, :LANES], both[:, LANES:])
        dj = jnp.tile(d_ref[:, j * LANES:(j + 1) * LANES], (sub_steps, 1))
        y_scr[j, block_rows(q), :] = jax.nn.gelu(yj + dj * u_scr[j, block_rows(q), :])

    def emit(q):
        for b in range(n_batch):
            for half in range(2):
                for j in range(n_slabs):
                    y_ref[b * steps + q * sub_steps:b * steps + (q + 1) * sub_steps, channels(half, j)] = (
                        y_scr[j, seq_rows(q, b, half), :].astype(y_ref.dtype))

    n_sub = SCAN_SUB_BLOCKS
    for s in range(n_sub + 3):
        if s < n_sub:
            project_u(s)
        for j in range(n_slabs):
            if 0 <= s - 1 < n_sub:
                drive(s - 1, j)
            if 0 <= s - 2 < n_sub:
                recur(s - 2, j)
            if 0 <= s - 3 < n_sub:
                readout(s - 3, j)
        if 0 <= s - 3 < n_sub:
            emit(s - 3)
    _run_casts(cast_jobs)


def _s5_scan(hb, w_in, w_layer, wb, wc, a_r, a_i, d_rows, layer, n_batch, cast_jobs=()):
    t, d = hb.shape
    tile = n_batch * BLOCK
    rows = BLOCK * SSM_SEQS
    n_slabs = wb.shape[1]
    state_lanes = n_slabs * SLAB_STATE_LANES
    half_w = n_slabs * LANES
    ssm_w = 2 * half_w
    u_block = QKV_WIDTH // half_w
    assert QKV_WIDTH % half_w == 0 and BLOCK % SCAN_SUB_BLOCKS == 0
    u_spec = lambda half: pl.BlockSpec((None, d, half_w), lambda i: (w_layer, 0, u_block + half),
                                       pipeline_mode=pl.Buffered(1))
    return pl.pallas_call(
        functools.partial(_s5_scan_kernel, n_batch=n_batch,
                          casts=tuple(job["has_scale"] for job in cast_jobs)),
        grid=(t // tile,),
        in_specs=[
            pl.BlockSpec((tile, d), lambda i: (i, 0)),
            u_spec(0), u_spec(1), _layer_spec(wb.shape, layer), _layer_spec(wc.shape, layer),
            _layer_spec(a_r.shape, layer), _layer_spec(a_i.shape, layer), _layer_spec(d_rows.shape, layer),
        ] + [s for job in cast_jobs for s in job["in_specs"]],
        out_specs=[pl.BlockSpec((tile, ssm_w), lambda i: (i, 0))] + [job["out_spec"] for job in cast_jobs],
        out_shape=[jax.ShapeDtypeStruct((t, ssm_w), BF16)] + [job["out_shape"] for job in cast_jobs],
        scratch_shapes=[pltpu.VMEM((n_slabs, rows, LANES), F32), pltpu.VMEM((rows, state_lanes), F32),
                        pltpu.VMEM((n_slabs, rows, LANES), F32), pltpu.VMEM((SSM_SEQS, state_lanes), F32)],
        compiler_params=_cparams("arbitrary"),
        name="s5_scan",
    )(hb, w_in, w_in, wb, wc, a_r, a_i, d_rows, *[a for job in cast_jobs for a in job["args"]])


def _mix_kernel(ya_ref, ys_ref, gate_ref, h_ref, wglu_ref, wau_ref, wsu_ref, wout_ref, g_ref, b_ref,
                *refs, alpha, n_batch, casts):
    (hf_ref, hb_ref), cast_jobs, _ = _split_cast_refs(refs, casts, 2)
    tm, d = h_ref.shape
    y = ys_ref[...]
    glu = jnp.dot(y, wglu_ref[...], preferred_element_type=F32)
    y_ssm = (y.astype(F32) * _sigmoid(glu)).astype(BF16)
    up_a = jnp.dot(ya_ref[...], wau_ref[...], preferred_element_type=F32)
    up_s = jnp.dot(y_ssm, wsu_ref[...], preferred_element_type=F32)
    merged = gate_ref[:, :d].astype(F32) * up_a + gate_ref[:, d:].astype(F32) * up_s
    mixed = jnp.dot(merged.astype(BF16), wout_ref[...], preferred_element_type=F32)
    out = _layer_norm(alpha * h_ref[...] + mixed, g_ref[...], b_ref[...])
    out = jnp.where(_real_row_mask(pl.program_id(0) * tm, tm, n_batch), out, 0.0)
    hf_ref[...] = out
    hb_ref[...] = out.astype(BF16)
    _run_casts(cast_jobs)


MIX_ROWS = 2 * BLOCK


def _mix(y_attn, y_ssm, gates, h, wglu, wau, wsu, wout, g, b, layer, alpha, n_batch, cast_jobs=()):
    t, d = h.shape
    tm = MIX_ROWS
    row = lambda w: pl.BlockSpec((tm, w), lambda i: (i, 0))
    return pl.pallas_call(
        functools.partial(_mix_kernel, alpha=alpha, n_batch=n_batch,
                          casts=tuple(job["has_scale"] for job in cast_jobs)),
        grid=(t // tm,),
        in_specs=[row(y_attn.shape[1]), row(y_ssm.shape[1]), row(gates.shape[1]), row(d),
                  _layer_spec(wglu.shape, 0), _layer_spec(wau.shape, 0), _layer_spec(wsu.shape, 0),
                  _layer_spec(wout.shape, 0), _layer_spec(g.shape, layer), _layer_spec(b.shape, layer)]
                 + [s for job in cast_jobs for s in job["in_specs"]],
        out_specs=[row(d), row(d)] + [job["out_spec"] for job in cast_jobs],
        out_shape=[jax.ShapeDtypeStruct((t, d), F32), jax.ShapeDtypeStruct((t, d), BF16)]
                  + [job["out_shape"] for job in cast_jobs],
        compiler_params=_cparams("arbitrary"),
        name="mix_ln",
    )(y_attn, y_ssm, gates, h, wglu, wau, wsu, wout, g, b, *[a for job in cast_jobs for a in job["args"]])


def _mlp_kernel(xb_ref, xf_ref, wup_ref, wdn_ref, g_ref, b_ref, *refs, alpha, n_batch, final):
    out_refs, acc_ref = refs[:-1], refs[-1]
    i, j = pl.program_id(0), pl.program_id(1)
    tm, d = xb_ref.shape

    last = pl.num_programs(1) - 1

    def accumulate(r, first=False):
        a = jnp.maximum(jnp.dot(xb_ref[r, :], wup_ref[...], preferred_element_type=F32), 0.0)
        part = jnp.dot((a * a).astype(BF16), wdn_ref[...], preferred_element_type=F32)
        acc_ref[r, :] = (alpha * xf_ref[r, :] if first else acc_ref[r, :]) + part

    @pl.when(j == 0)
    def _():
        accumulate(slice(None), first=True)

    @pl.when((j > 0) & (j < last))
    def _():
        accumulate(slice(None))

    @pl.when(j == last)
    def _():
        half = tm // 2
        for r0 in (0, half):
            r = slice(r0, r0 + half)
            accumulate(r)
            out = _layer_norm(acc_ref[r, :], g_ref[...], b_ref[...])
            if final:
                out_refs[0][r0 // BLOCK:(r0 + half) // BLOCK] = out.reshape(half // BLOCK, BLOCK, d)
            else:
                out = jnp.where(_real_row_mask(i * tm + r0, half, n_batch), out, 0.0)
                out_refs[0][r, :] = out
                out_refs[1][r, :] = out.astype(BF16)


MLP_FF_CHUNK = 1024


def _mlp(hb, hf, wup, wdn, g, b, layer, alpha, n_batch, final):
    t, d = hf.shape
    dff = wup.shape[2]
    tm = n_batch * BLOCK
    tf = MLP_FF_CHUNK
    assert dff % tf == 0 and dff // tf >= 2
    first = 1 if final else 0
    row = lambda: pl.BlockSpec((tm, d), lambda i, j: (i + first, 0))
    if final:
        out_specs = [pl.BlockSpec((n_batch, BLOCK, d), lambda i, j: (0, i, 0))]
        out_shape = [jax.ShapeDtypeStruct((n_batch, t // n_batch - BLOCK, d), F32)]
    else:
        out_specs = [row(), row()]
        out_shape = [jax.ShapeDtypeStruct((t, d), F32), jax.ShapeDtypeStruct((t, d), BF16)]
    return pl.pallas_call(
        functools.partial(_mlp_kernel, alpha=alpha, n_batch=n_batch, final=final),
        grid=(t // tm - first, dff // tf),
        in_specs=[row(), row(),
                  pl.BlockSpec((None, d, tf), lambda i, j: (0, 0, j)),
                  pl.BlockSpec((None, tf, d), lambda i, j: (0, j, 0)),
                  _layer_spec(g.shape, layer), _layer_spec(b.shape, layer)],
        out_specs=out_specs,
        out_shape=out_shape,
        scratch_shapes=[pltpu.VMEM((tm, d), F32)],
        compiler_params=_cparams("parallel", "arbitrary"),
        name="mlp_ln",
    )(hb, hf, wup, wdn, g, b)


def kernel(x, meta_tokens, ln_emb_g, ln_emb_b, rel_bias, in_proj, gate_b, attn_sinks, ssm_lambda_re,
           ssm_lambda_im, ssm_log_step, ssm_b_re, ssm_b_im, ssm_c_re, ssm_c_im, ssm_d, ssm_w_glu, w_attn_up,
           w_ssm_up, w_out, ln_mix_g, ln_mix_b, w_mlp_up, w_mlp_down, ln_mlp_g, ln_mlp_b):
    bsz, seq, d = x.shape
    depth = in_proj.shape[0]
    assert seq % BLOCK == 0 and 2 * bsz == SSM_SEQS and meta_tokens.shape[0] == N_META
    nb = seq // BLOCK + 1
    alpha = (2 * depth) ** 0.25
    row2 = lambda a: a.astype(F32).reshape(1, -1)
    rows = lambda a: a.astype(F32)[:, None, :]
    bf = lambda a: a.astype(BF16)

    meta_pad = jnp.concatenate([jnp.zeros((META_PAD, d), x.dtype), meta_tokens.astype(x.dtype)], axis=0)
    hf, hb = _embed_ln(x, meta_pad, row2(ln_emb_g), row2(ln_emb_b))
    bias_tables = _bias_tables(rel_bias)

    q_scale = jnp.where(jnp.arange(in_proj.shape[2]) < ATTN_WIDTH, HEAD_DIM ** -0.5 * LOG2_E, 1.0)
    q_scale = q_scale.astype(in_proj.dtype)[None, :]
    sinks = attn_sinks.astype(F32) * LOG2_E
    ar, ai, bbr, bbi = _s5_discretize(ssm_lambda_re, ssm_lambda_im, ssm_log_step, ssm_b_re, ssm_b_im)
    wb, wc, a_r, a_i, d_rows = _s5_scan_weights(ar, ai, bbr, bbi, ssm_c_re, ssm_c_im, ssm_d)

    t = hf.shape[0]
    scan_steps, mix_steps = t // (bsz * BLOCK), t // MIX_ROWS
    w_in = bf(in_proj[:1] * q_scale)

    for l in range(depth):
        final = l == depth - 1
        qkv, gates = _in_proj(hb, w_in, 0, rows(gate_b), l, bsz * BLOCK)
        y_attn = _attention(qkv, bias_tables, sinks, l, bsz, nb)
        scan_jobs = [_cast_job(w, l, scan_steps) for w in (w_mlp_up, ssm_w_glu, w_attn_up, w_ssm_up, w_out)]
        y_gelu, w_up, w_glu, w_au, w_su, w_o = _s5_scan(hb, w_in, 0, wb, wc, a_r, a_i, d_rows, l, bsz,
                                                        cast_jobs=scan_jobs)
        mix_jobs = [_cast_job(w_mlp_down, l, mix_steps)]
        if not final:
            mix_jobs.append(_cast_job(in_proj, l + 1, mix_steps, scale=q_scale))
        hf, hb, w_dn, *w_next = _mix(y_attn, y_gelu, gates, hf, w_glu, w_au, w_su, w_o, rows(ln_mix_g),
                                     rows(ln_mix_b), l, alpha, bsz, cast_jobs=mix_jobs)
        out = _mlp(hb, hf, w_up, w_dn, rows(ln_mlp_g), rows(ln_mlp_b), l, alpha, bsz, final)
        if not final:
            (hf, hb), (w_in,) = out, w_next

    return out[0]
```

```python
import functools
import math

import jax
import jax.numpy as jnp
from jax import lax
from jax.experimental import pallas as pl
from jax.experimental.pallas import tpu as pltpu

F32 = jnp.float32
BF16 = jnp.bfloat16

N_META = 16
BLOCK = 128
META_PAD = BLOCK - N_META
HEAD_DIM = 64
N_Q_HEADS = 16
N_KV_HEADS = 4
Q_PER_KV = N_Q_HEADS // N_KV_HEADS
ATTN_WIDTH = N_Q_HEADS * HEAD_DIM
KV_WIDTH = N_KV_HEADS * HEAD_DIM
QKV_WIDTH = ATTN_WIDTH + 2 * KV_WIDTH
SSM_STATE = 64
N_BUCKETS = 32
MAX_DISTANCE = 128
LN_EPS = 1e-5
NEG_INF = -1e30
LOG2_E = math.log2(math.e)

LANES = 128
SUBLANES = 8
V7X_VMEM_BYTES = 64 * 1024 * 1024
VMEM_LIMIT_BYTES = V7X_VMEM_BYTES - 4 * 1024 * 1024

SSM_SEQS = SUBLANES
STATES_PER_CHUNK = LANES
GROUPS_PER_CHUNK = STATES_PER_CHUNK // SSM_STATE
CHUNKS_PER_SLAB = 4
SLAB_STATE_LANES = CHUNKS_PER_SLAB * 2 * STATES_PER_CHUNK


def _cparams(*sem):
    return pltpu.CompilerParams(dimension_semantics=sem, vmem_limit_bytes=VMEM_LIMIT_BYTES)


def _resident(shape):
    nd = len(shape)
    return pl.BlockSpec(shape, lambda *_: (0,) * nd, pipeline_mode=pl.Buffered(1))


def _layer_spec(shape, layer):
    nd = len(shape)
    return pl.BlockSpec((None,) + tuple(shape[1:]), lambda *_: (layer,) + (0,) * (nd - 1),
                        pipeline_mode=pl.Buffered(1))


def _even_parts(total, max_parts, align):
    return max(p for p in range(1, max_parts + 1) if total % p == 0 and (total // p) % align == 0)


def _cast_job(w, layer, n_steps, scale=None):
    _, r, c = w.shape
    parts = _even_parts(r, n_steps, 2 * SUBLANES)
    blk = (None, r // parts, c)
    row = lambda i: jnp.minimum(i, parts - 1)
    in_specs, args = [pl.BlockSpec(blk, lambda i: (layer, row(i), 0))], [w]
    if scale is not None:
        in_specs.append(_resident(scale.shape))
        args.append(scale)
    return dict(in_specs=in_specs, args=args, has_scale=scale is not None,
                out_spec=pl.BlockSpec(blk, lambda i: (0, row(i), 0)),
                out_shape=jax.ShapeDtypeStruct((1, r, c), BF16))


def _split_cast_refs(refs, casts, n_out):
    n_in = sum(2 if s else 1 for s in casts)
    cast_in, rest = refs[:n_in], refs[n_in:]
    outs, cast_out, scratch = rest[:n_out], rest[n_out:n_out + len(casts)], rest[n_out + len(casts):]
    jobs, k = [], 0
    for has_scale, dst in zip(casts, cast_out):
        jobs.append((cast_in[k], cast_in[k + 1] if has_scale else None, dst))
        k += 2 if has_scale else 1
    return outs, jobs, scratch


def _run_casts(jobs):
    for src, scale, dst in jobs:
        v = src[...] if scale is None else src[...] * scale[...]
        dst[...] = v.astype(dst.dtype)


def _layer_norm(x, g, b):
    mu = jnp.mean(x, axis=-1, keepdims=True)
    xc = x - mu
    var = jnp.mean(xc * xc, axis=-1, keepdims=True)
    return xc * lax.rsqrt(var + LN_EPS) * g + b


def _sigmoid(z):
    return 1.0 / (1.0 + jnp.exp(-z))


def _real_row_mask(first_row, n_rows, n_batch):
    r = first_row + lax.broadcasted_iota(jnp.int32, (n_rows, 1), 0)
    is_pad = (r < n_batch * BLOCK) & ((r & (BLOCK - 1)) < META_PAD)
    return jnp.logical_not(is_pad)


def _embed_ln_kernel(meta_ref, x_ref, g_ref, b_ref, hf_ref):
    is_meta = pl.program_id(0) == 0
    rows = lax.broadcasted_iota(jnp.int32, (BLOCK, 1), 0)
    keep = jnp.logical_or(jnp.logical_not(is_meta), rows >= META_PAD)
    for bi in range(x_ref.shape[0]):
        xin = jnp.where(is_meta, meta_ref[...], x_ref[bi])
        hf_ref[bi * BLOCK:(bi + 1) * BLOCK, :] = jnp.where(keep, _layer_norm(xin, g_ref[...], b_ref[...]), 0.0)


def _embed_ln(x, meta_pad, g, b):
    bsz, seq, d = x.shape
    nb = seq // BLOCK + 1
    t = bsz * nb * BLOCK
    tile = pl.BlockSpec((bsz * BLOCK, d), lambda n: (n, 0))
    return pl.pallas_call(
        _embed_ln_kernel,
        grid=(nb,),
        in_specs=[
            _resident(meta_pad.shape),
            pl.BlockSpec((bsz, BLOCK, d), lambda n: (0, jnp.maximum(n - 1, 0), 0)),
            _resident(g.shape), _resident(b.shape),
        ],
        out_specs=tile,
        out_shape=jax.ShapeDtypeStruct((t, d), F32),
        compiler_params=_cparams("parallel"),
        name="embed_ln",
    )(meta_pad, x, g, b)


GATE_COL_CHUNK = 1024


def _in_proj_kernel(x_ref, w_ref, gb_ref, qkv_ref, gate_ref):
    x = x_ref[...].astype(BF16)
    qkv_ref[...] = jnp.dot(x, w_ref[:, :QKV_WIDTH], preferred_element_type=F32).astype(qkv_ref.dtype)
    gate0 = w_ref.shape[1] - gate_ref.shape[1]
    for c in range(0, gate_ref.shape[1], GATE_COL_CHUNK):
        z = jnp.dot(x, w_ref[:, gate0 + c:gate0 + c + GATE_COL_CHUNK], preferred_element_type=F32)
        z = z + gb_ref[:, c:c + GATE_COL_CHUNK]
        gate_ref[:, c:c + GATE_COL_CHUNK] = _sigmoid(z).astype(gate_ref.dtype)


def _in_proj(hb, w_in, w_layer, gate_bias, layer, tm):
    t, d = hb.shape
    gate_w = gate_bias.shape[2]
    row = lambda w: pl.BlockSpec((tm, w), lambda i: (i, 0))
    return pl.pallas_call(
        _in_proj_kernel,
        grid=(t // tm,),
        in_specs=[row(d), _layer_spec(w_in.shape, w_layer), _layer_spec(gate_bias.shape, layer)],
        out_specs=[row(QKV_WIDTH), row(gate_w)],
        out_shape=[jax.ShapeDtypeStruct((t, QKV_WIDTH), BF16), jax.ShapeDtypeStruct((t, gate_w), BF16)],
        compiler_params=_cparams("parallel"),
        name="in_proj",
    )(hb, w_in, gate_bias)


def _attn_kernel(sink_ref, q_ref, kvm_ref, kvp_ref, kvc_ref, bias_ref, o_ref, *, layer):
    kv_m = kvm_ref[...]
    kv_w = jnp.concatenate([kvp_ref[...], kvc_ref[...]], axis=0)
    rows = Q_PER_KV * BLOCK
    low = lax.broadcasted_iota(jnp.int32, (1, LANES), 1) < HEAD_DIM
    head_of_row = lax.broadcasted_iota(jnp.int32, (rows, 1), 0) // BLOCK
    zero = jnp.zeros((), q_ref.dtype)
    contract_last = (((1,), (1,)), ((), ()))

    def dup_keys(kv, kh):
        k = kv[:, kh * HEAD_DIM:(kh + 1) * HEAD_DIM]
        return jnp.concatenate([k, k], axis=1)

    def dup_values(kv, kh):
        v = kv[:, KV_WIDTH + kh * HEAD_DIM:KV_WIDTH + (kh + 1) * HEAD_DIM]
        return jnp.concatenate([v, v, jnp.ones((kv.shape[0], LANES), kv.dtype)], axis=1)

    def scores(kh):
        cols = [q_ref[:, (kh * Q_PER_KV + 2 * c) * HEAD_DIM:(kh * Q_PER_KV + 2 * c + 2) * HEAD_DIM]
                for c in range(Q_PER_KV // 2)]
        qs = jnp.concatenate([jnp.where(low, cols[g // 2], zero) if g % 2 == 0 else jnp.where(low, zero, cols[g // 2])
                              for g in range(Q_PER_KV)], axis=0)
        s_m = lax.dot_general(qs, dup_keys(kv_m, kh), contract_last, preferred_element_type=F32)
        s_w = lax.dot_general(qs, dup_keys(kv_w, kh), contract_last, preferred_element_type=F32)
        return s_m + bias_ref[kh, :, :BLOCK], s_w + bias_ref[kh, :, BLOCK:]

    def finish(kh, s_m, s_w):
        sink = jnp.zeros((rows, 1), F32)
        for g in range(Q_PER_KV):
            sink = jnp.where(head_of_row == g, sink_ref[layer, kh * Q_PER_KV + g], sink)
        m = jnp.maximum(jnp.maximum(jnp.max(s_m, axis=-1, keepdims=True), jnp.max(s_w, axis=-1, keepdims=True)), sink)
        p_m = jnp.exp2(s_m - m).astype(BF16)
        p_w = jnp.exp2(s_w - m).astype(BF16)
        pv = (jnp.dot(p_m, dup_values(kv_m, kh), preferred_element_type=F32)
              + jnp.dot(p_w, dup_values(kv_w, kh), preferred_element_type=F32))
        denom = pv[:, 2 * HEAD_DIM:] + jnp.exp2(sink - m)
        o = pv[:, :2 * HEAD_DIM] * (1.0 / denom)
        for c in range(Q_PER_KV // 2):
            pair = jnp.where(low, o[2 * c * BLOCK:(2 * c + 1) * BLOCK], o[(2 * c + 1) * BLOCK:(2 * c + 2) * BLOCK])
            o_ref[:, (kh * Q_PER_KV + 2 * c) * HEAD_DIM:(kh * Q_PER_KV + 2 * c + 2) * HEAD_DIM] = pair.astype(o_ref.dtype)

    pending = scores(0)
    for kh in range(N_KV_HEADS):
        ahead = scores(kh + 1) if kh + 1 < N_KV_HEADS else None
        finish(kh, *pending)
        pending = ahead


def _attention(qkv, bias_tables, sinks, layer, bsz, nb):
    t = qkv.shape[0]
    kv_col = ATTN_WIDTH // (2 * KV_WIDTH)
    kv_spec = lambda row_fn: pl.BlockSpec((BLOCK, 2 * KV_WIDTH), lambda n, b: (row_fn(n, b), kv_col))
    return pl.pallas_call(
        functools.partial(_attn_kernel, layer=layer),
        grid=(nb, bsz),
        in_specs=[
            pl.BlockSpec(memory_space=pltpu.SMEM),
            pl.BlockSpec((BLOCK, ATTN_WIDTH), lambda n, b: (n * bsz + b, 0)),
            kv_spec(lambda n, b: b),
            kv_spec(lambda n, b: jnp.maximum(n - 1, 0) * bsz + b),
            kv_spec(lambda n, b: n * bsz + b),
            pl.BlockSpec((None, N_KV_HEADS, Q_PER_KV * BLOCK, 3 * BLOCK),
                         lambda n, b: (jnp.minimum(n, 2), 0, 0, 0)),
        ],
        out_specs=pl.BlockSpec((BLOCK, ATTN_WIDTH), lambda n, b: (n * bsz + b, 0)),
        out_shape=jax.ShapeDtypeStruct((t, ATTN_WIDTH), BF16),
        compiler_params=_cparams("parallel", "parallel"),
        name="swa",
    )(sinks, qkv, qkv, qkv, qkv, bias_tables)


def _t5_bucket(dist):
    n = jnp.maximum(dist, 0)
    max_exact = N_BUCKETS // 2
    nf = jnp.maximum(n, 1).astype(F32)
    large = max_exact + (jnp.log(nf / max_exact) / math.log(MAX_DISTANCE / max_exact)
                         * (N_BUCKETS - max_exact)).astype(jnp.int32)
    large = jnp.minimum(large, N_BUCKETS - 1)
    return jnp.where(n < max_exact, n, large)


def _bias_tables(rel_bias):
    blk = jnp.arange(3, dtype=jnp.int32)[:, None]
    j = jnp.arange(BLOCK, dtype=jnp.int32)[None, :]
    q_pos = blk * BLOCK + j
    k_pos = jnp.concatenate([jnp.broadcast_to(j, (3, BLOCK)), (blk - 1) * BLOCK + j, q_pos], axis=1)
    dist = q_pos[:, :, None] - k_pos[:, None, :]
    kp = k_pos[:, None, :]
    is_meta_seg = (jnp.arange(3 * BLOCK) < BLOCK)[None, None, :]
    valid = (dist >= 0) & jnp.where(is_meta_seg, kp >= META_PAD, (kp >= BLOCK) & (dist < BLOCK))
    onehot = jax.nn.one_hot(_t5_bucket(dist), N_BUCKETS, dtype=F32)
    bias = jnp.einsum("tqkb,bh->thqk", onehot, rel_bias.astype(F32), precision=lax.Precision.HIGHEST)
    bias = jnp.where(valid[:, None], bias * LOG2_E, NEG_INF)
    return bias.reshape(3, N_KV_HEADS, Q_PER_KV * BLOCK, 3 * BLOCK)


def _s5_discretize_kernel(lr_ref, li_ref, ls_ref, bre_ref, bim_ref, ar_ref, ai_ref, bbr_ref, bbi_ref):
    lr, li = lr_ref[...], li_ref[...]
    dt = jnp.exp(ls_ref[...])
    decay = jnp.exp(lr * dt)
    ar, ai = decay * jnp.cos(li * dt), decay * jnp.sin(li * dt)
    den = lr * lr + li * li
    nr, ni = ar - 1.0, ai
    zr = (nr * lr + ni * li) / den
    zi = (ni * lr - nr * li) / den
    bre, bim = bre_ref[...], bim_ref[...]
    ar_ref[...] = ar
    ai_ref[...] = ai
    bbr_ref[...] = zr * bre - zi * bim
    bbi_ref[...] = zr * bim + zi * bre


def _s5_discretize(lam_re, lam_im, log_step, b_re, b_im):
    depth, g, n, p = b_re.shape
    flat = lambda a: a.astype(F32).reshape(depth * g, n * p)
    rep = lambda a: flat(jnp.broadcast_to(a.astype(F32)[..., None], (depth, g, n, p)))
    ls = flat(jnp.broadcast_to(log_step.astype(F32)[:, :, None, None], (depth, g, n, p)))
    shp = jax.ShapeDtypeStruct((depth * g, n * p), F32)
    ar, ai, bbr, bbi = pl.pallas_call(
        _s5_discretize_kernel, out_shape=[shp] * 4, name="s5_discretize",
    )(rep(lam_re), rep(lam_im), ls, flat(b_re), flat(b_im))
    state = lambda a: a.reshape(depth, g, n, p)[..., 0]
    return state(ar), state(ai), bbr.reshape(depth, g, n, p), bbi.reshape(depth, g, n, p)


def _s5_scan_weights(ar, ai, bbr, bbi, c_re, c_im, d_skip):
    depth, g, n, p = bbr.shape
    halves, q_per, gl_per = 2, CHUNKS_PER_SLAB, GROUPS_PER_CHUNK
    gi_per = q_per * gl_per
    slabs = g // (halves * gi_per)
    n_ch = halves * gi_per * p
    gi_of_ch = (jnp.arange(n_ch) // p) % gi_per
    lane_blocks = [(q, ri, gl) for q in range(q_per) for ri in range(2) for gl in range(gl_per)]

    b_ri = jnp.stack([bbr, bbi], axis=1).reshape(depth, 2, halves, slabs, gi_per, n, p)
    b_rows = b_ri.transpose(0, 3, 1, 2, 4, 6, 5).reshape(depth, slabs, 2, n_ch, n)
    wb = jnp.concatenate([jnp.where((gi_of_ch == gl_per * q + gl)[:, None], b_rows[:, :, ri], 0.0)
                          for q, ri, gl in lane_blocks], axis=-1)

    c_ri = jnp.stack([c_re.astype(F32), -c_im.astype(F32)], axis=1).reshape(depth, 2, halves, slabs, gi_per, p, n)
    c_cols = c_ri.transpose(0, 3, 1, 6, 2, 4, 5).reshape(depth, slabs, 2, n, n_ch)
    wc = jnp.concatenate([jnp.where((gi_of_ch == gl_per * q + gl)[None, :], c_cols[:, :, ri], 0.0)
                          for q, ri, gl in lane_blocks], axis=-2)

    def per_seq(a):
        return jnp.repeat(a.astype(F32).reshape(depth, halves, -1), SSM_SEQS // halves, axis=1)

    return wb.astype(BF16), wc.astype(BF16), per_seq(ar), per_seq(ai), per_seq(d_skip)


SCAN_SUB_BLOCKS = 4


def _s5_scan_kernel(x_ref, wu0_ref, wu1_ref, wb_ref, wc_ref, ar_ref, ai_ref, d_ref, *refs, n_batch, casts):
    (y_ref,), cast_jobs, (u_scr, s_ref, y_scr, h_ref) = _split_cast_refs(refs, casts, 1)
    steps = BLOCK
    sub_steps = steps // SCAN_SUB_BLOCKS
    sub_rows = sub_steps * SSM_SEQS
    n_slabs = wb_ref.shape[0]
    half_w = n_slabs * LANES

    @pl.when(pl.program_id(0) == 0)
    def _():
        h_ref[...] = jnp.zeros_like(h_ref)

    def seq_rows(q, b, half):
        return pl.ds(q * sub_rows + half * n_batch + b, sub_steps, stride=SSM_SEQS)

    def block_rows(q):
        return slice(q * sub_rows, (q + 1) * sub_rows)

    def channels(half, j):
        return slice(half * half_w + j * LANES, half * half_w + (j + 1) * LANES)

    half0 = (lax.broadcasted_iota(jnp.int32, (sub_rows, 1), 0) & (SSM_SEQS - 1)) < (SSM_SEQS // 2)

    def project_u(q):
        x = jnp.concatenate([x_ref[b * steps + q * sub_steps:b * steps + (q + 1) * sub_steps, :]
                             for b in range(n_batch)], axis=0).astype(BF16)
        u_half = [jnp.dot(x, w[...], preferred_element_type=F32) for w in (wu0_ref, wu1_ref)]
        for b in range(n_batch):
            for half in range(2):
                for j in range(n_slabs):
                    u_scr[j, seq_rows(q, b, half), :] = (
                        u_half[half][b * sub_steps:(b + 1) * sub_steps, j * LANES:(j + 1) * LANES])

    def drive(q, j):
        uj = u_scr[j, block_rows(q), :]
        lhs = jnp.concatenate([jnp.where(half0, uj, 0.0), jnp.where(half0, 0.0, uj)], axis=1).astype(BF16)
        s_ref[block_rows(q), j * SLAB_STATE_LANES:(j + 1) * SLAB_STATE_LANES] = jnp.dot(
            lhs, wb_ref[j], preferred_element_type=F32)

    def recur(q, j):
        chunks = range(j * CHUNKS_PER_SLAB, (j + 1) * CHUNKS_PER_SLAB)
        a_r = [ar_ref[:, c * LANES:(c + 1) * LANES] for c in chunks]
        a_i = [ai_ref[:, c * LANES:(c + 1) * LANES] for c in chunks]
        init = []
        for c in chunks:
            init += [h_ref[:, 2 * c * LANES:(2 * c + 1) * LANES], h_ref[:, (2 * c + 1) * LANES:(2 * c + 2) * LANES]]

        state = init
        for t in range(q * sub_steps, (q + 1) * sub_steps):
            rows_t = slice(t * SSM_SEQS, (t + 1) * SSM_SEQS)
            new = []
            for k, c in enumerate(chunks):
                hr, hi = state[2 * k], state[2 * k + 1]
                re_sl = (rows_t, slice(2 * c * LANES, (2 * c + 1) * LANES))
                im_sl = (rows_t, slice((2 * c + 1) * LANES, (2 * c + 2) * LANES))
                nr = a_r[k] * hr - a_i[k] * hi + s_ref[re_sl]
                ni = a_r[k] * hi + a_i[k] * hr + s_ref[im_sl]
                s_ref[re_sl] = nr
                s_ref[im_sl] = ni
                new += [nr, ni]
            state = new
        for k, c in enumerate(chunks):
            h_ref[:, 2 * c * LANES:(2 * c + 1) * LANES] = state[2 * k]
            h_ref[:, (2 * c + 1) * LANES:(2 * c + 2) * LANES] = state[2 * k + 1]

    def readout(q, j):
        hs = s_ref[block_rows(q), j * SLAB_STATE_LANES:(j + 1) * SLAB_STATE_LANES].astype(BF16)
        both = jnp.dot(hs, wc_ref[j], preferred_element_type=F32)
        yj = jnp.where(half0, both[:, :LANES], both[:, LANES:])
        dj = jnp.tile(d_ref[:, j * LANES:(j + 1) * LANES], (sub_steps, 1))
        y_scr[j, block_rows(q), :] = jax.nn.gelu(yj + dj * u_scr[j, block_rows(q), :])

    def emit(q):
        for b in range(n_batch):
            for half in range(2):
                for j in range(n_slabs):
                    y_ref[b * steps + q * sub_steps:b * steps + (q + 1) * sub_steps, channels(half, j)] = (
                        y_scr[j, seq_rows(q, b, half), :].astype(y_ref.dtype))

    n_sub = SCAN_SUB_BLOCKS
    for s in range(n_sub + 3):
        if s < n_sub:
            project_u(s)
        for j in range(n_slabs):
            if 0 <= s - 1 < n_sub:
                drive(s - 1, j)
            if 0 <= s - 2 < n_sub:
                recur(s - 2, j)
            if 0 <= s - 3 < n_sub:
                readout(s - 3, j)
        if 0 <= s - 3 < n_sub:
            emit(s - 3)
    _run_casts(cast_jobs)


def _s5_scan(hb, w_in, w_layer, wb, wc, a_r, a_i, d_rows, layer, n_batch, cast_jobs=()):
    t, d = hb.shape
    tile = n_batch * BLOCK
    rows = BLOCK * SSM_SEQS
    n_slabs = wb.shape[1]
    state_lanes = n_slabs * SLAB_STATE_LANES
    half_w = n_slabs * LANES
    ssm_w = 2 * half_w
    u_block = QKV_WIDTH // half_w
    assert QKV_WIDTH % half_w == 0 and BLOCK % SCAN_SUB_BLOCKS == 0
    u_spec = lambda half: pl.BlockSpec((None, d, half_w), lambda i: (w_layer, 0, u_block + half),
                                       pipeline_mode=pl.Buffered(1))
    return pl.pallas_call(
        functools.partial(_s5_scan_kernel, n_batch=n_batch,
                          casts=tuple(job["has_scale"] for job in cast_jobs)),
        grid=(t // tile,),
        in_specs=[
            pl.BlockSpec((tile, d), lambda i: (i, 0)),
            u_spec(0), u_spec(1), _layer_spec(wb.shape, layer), _layer_spec(wc.shape, layer),
            _layer_spec(a_r.shape, layer), _layer_spec(a_i.shape, layer), _layer_spec(d_rows.shape, layer),
        ] + [s for job in cast_jobs for s in job["in_specs"]],
        out_specs=[pl.BlockSpec((tile, ssm_w), lambda i: (i, 0))] + [job["out_spec"] for job in cast_jobs],
        out_shape=[jax.ShapeDtypeStruct((t, ssm_w), BF16)] + [job["out_shape"] for job in cast_jobs],
        scratch_shapes=[pltpu.VMEM((n_slabs, rows, LANES), F32), pltpu.VMEM((rows, state_lanes), F32),
                        pltpu.VMEM((n_slabs, rows, LANES), F32), pltpu.VMEM((SSM_SEQS, state_lanes), F32)],
        compiler_params=_cparams("arbitrary"),
        name="s5_scan",
    )(hb, w_in, w_in, wb, wc, a_r, a_i, d_rows, *[a for job in cast_jobs for a in job["args"]])


def _mix_kernel(ya_ref, ys_ref, gate_ref, h_ref, wglu_ref, wau_ref, wsu_ref, wout_ref, g_ref, b_ref,
                *refs, alpha, n_batch, casts):
    (hf_ref, hb_ref), cast_jobs, _ = _split_cast_refs(refs, casts, 2)
    tm, d = h_ref.shape
    y = ys_ref[...]
    glu = jnp.dot(y, wglu_ref[...], preferred_element_type=F32)
    y_ssm = (y.astype(F32) * _sigmoid(glu)).astype(BF16)
    up_a = jnp.dot(ya_ref[...], wau_ref[...], preferred_element_type=F32)
    up_s = jnp.dot(y_ssm, wsu_ref[...], preferred_element_type=F32)
    merged = gate_ref[:, :d].astype(F32) * up_a + gate_ref[:, d:].astype(F32) * up_s
    mixed = jnp.dot(merged.astype(BF16), wout_ref[...], preferred_element_type=F32)
    out = _layer_norm(alpha * h_ref[...] + mixed, g_ref[...], b_ref[...])
    out = jnp.where(_real_row_mask(pl.program_id(0) * tm, tm, n_batch), out, 0.0)
    hf_ref[...] = out
    hb_ref[...] = out.astype(BF16)
    _run_casts(cast_jobs)


MIX_ROWS = 3 * BLOCK


def _mix(y_attn, y_ssm, gates, h, wglu, wau, wsu, wout, g, b, layer, alpha, n_batch, cast_jobs=()):
    t, d = h.shape
    tm = MIX_ROWS
    row = lambda w: pl.BlockSpec((tm, w), lambda i: (i, 0))
    return pl.pallas_call(
        functools.partial(_mix_kernel, alpha=alpha, n_batch=n_batch,
                          casts=tuple(job["has_scale"] for job in cast_jobs)),
        grid=(t // tm,),
        in_specs=[row(y_attn.shape[1]), row(y_ssm.shape[1]), row(gates.shape[1]), row(d),
                  _layer_spec(wglu.shape, 0), _layer_spec(wau.shape, 0), _layer_spec(wsu.shape, 0),
                  _layer_spec(wout.shape, 0), _layer_spec(g.shape, layer), _layer_spec(b.shape, layer)]
                 + [s for job in cast_jobs for s in job["in_specs"]],
        out_specs=[row(d), row(d)] + [job["out_spec"] for job in cast_jobs],
        out_shape=[jax.ShapeDtypeStruct((t, d), F32), jax.ShapeDtypeStruct((t, d), BF16)]
                  + [job["out_shape"] for job in cast_jobs],
        compiler_params=_cparams("arbitrary"),
        name="mix_ln",
    )(y_attn, y_ssm, gates, h, wglu, wau, wsu, wout, g, b, *[a for job in cast_jobs for a in job["args"]])


def _mlp_kernel(xb_ref, xf_ref, wup_ref, wdn_ref, g_ref, b_ref, *refs, alpha, n_batch, final):
    out_refs, acc_ref = refs[:-1], refs[-1]
    i, j = pl.program_id(0), pl.program_id(1)
    tm, d = xb_ref.shape

    last = pl.num_programs(1) - 1

    def accumulate(r, first=False):
        a = jnp.maximum(jnp.dot(xb_ref[r, :], wup_ref[...], preferred_element_type=F32), 0.0)
        part = jnp.dot((a * a).astype(BF16), wdn_ref[...], preferred_element_type=F32)
        acc_ref[r, :] = (alpha * xf_ref[r, :] if first else acc_ref[r, :]) + part

    @pl.when(j == 0)
    def _():
        accumulate(slice(None), first=True)

    @pl.when((j > 0) & (j < last))
    def _():
        accumulate(slice(None))

    @pl.when(j == last)
    def _():
        half = tm // 2
        for r0 in (0, half):
            r = slice(r0, r0 + half)
            accumulate(r)
            out = _layer_norm(acc_ref[r, :], g_ref[...], b_ref[...])
            if final:
                out_refs[0][r0 // BLOCK:(r0 + half) // BLOCK] = out.reshape(half // BLOCK, BLOCK, d)
            else:
                out = jnp.where(_real_row_mask(i * tm + r0, half, n_batch), out, 0.0)
                out_refs[0][r, :] = out
                out_refs[1][r, :] = out.astype(BF16)


MLP_FF_CHUNK = 1024


def _mlp(hb, hf, wup, wdn, g, b, layer, alpha, n_batch, final):
    t, d = hf.shape
    dff = wup.shape[2]
    tm = n_batch * BLOCK
    tf = MLP_FF_CHUNK
    assert dff % tf == 0 and dff // tf >= 2
    first = 1 if final else 0
    row = lambda: pl.BlockSpec((tm, d), lambda i, j: (i + first, 0))
    if final:
        out_specs = [pl.BlockSpec((n_batch, BLOCK, d), lambda i, j: (0, i, 0))]
        out_shape = [jax.ShapeDtypeStruct((n_batch, t // n_batch - BLOCK, d), F32)]
    else:
        out_specs = [row(), row()]
        out_shape = [jax.ShapeDtypeStruct((t, d), F32), jax.ShapeDtypeStruct((t, d), BF16)]
    return pl.pallas_call(
        functools.partial(_mlp_kernel, alpha=alpha, n_batch=n_batch, final=final),
        grid=(t // tm - first, dff // tf),
        in_specs=[row(), row(),
                  pl.BlockSpec((None, d, tf), lambda i, j: (0, 0, j)),
                  pl.BlockSpec((None, tf, d), lambda i, j: (0, j, 0)),
                  _layer_spec(g.shape, layer), _layer_spec(b.shape, layer)],
        out_specs=out_specs,
        out_shape=out_shape,
        scratch_shapes=[pltpu.VMEM((tm, d), F32)],
        compiler_params=_cparams("parallel", "arbitrary"),
        name="mlp_ln",
    )(hb, hf, wup, wdn, g, b)


def kernel(x, meta_tokens, ln_emb_g, ln_emb_b, rel_bias, in_proj, gate_b, attn_sinks, ssm_lambda_re,
           ssm_lambda_im, ssm_log_step, ssm_b_re, ssm_b_im, ssm_c_re, ssm_c_im, ssm_d, ssm_w_glu, w_attn_up,
           w_ssm_up, w_out, ln_mix_g, ln_mix_b, w_mlp_up, w_mlp_down, ln_mlp_g, ln_mlp_b):
    bsz, seq, d = x.shape
    depth = in_proj.shape[0]
    assert seq % BLOCK == 0 and 2 * bsz == SSM_SEQS and meta_tokens.shape[0] == N_META
    nb = seq // BLOCK + 1
    alpha = (2 * depth) ** 0.25
    row2 = lambda a: a.astype(F32).reshape(1, -1)
    rows = lambda a: a.astype(F32)[:, None, :]
    bf = lambda a: a.astype(BF16)

    meta_pad = jnp.concatenate([jnp.zeros((META_PAD, d), x.dtype), meta_tokens.astype(x.dtype)], axis=0)
    hf = _embed_ln(x, meta_pad, row2(ln_emb_g), row2(ln_emb_b))
    hb = hf
    bias_tables = _bias_tables(rel_bias)

    q_scale = jnp.where(jnp.arange(in_proj.shape[2]) < ATTN_WIDTH, HEAD_DIM ** -0.5 * LOG2_E, 1.0)
    q_scale = q_scale.astype(in_proj.dtype)[None, :]
    sinks = attn_sinks.astype(F32) * LOG2_E
    ar, ai, bbr, bbi = _s5_discretize(ssm_lambda_re, ssm_lambda_im, ssm_log_step, ssm_b_re, ssm_b_im)
    wb, wc, a_r, a_i, d_rows = _s5_scan_weights(ar, ai, bbr, bbi, ssm_c_re, ssm_c_im, ssm_d)

    t = hf.shape[0]
    scan_steps, mix_steps = t // (bsz * BLOCK), t // MIX_ROWS
    w_in = bf(in_proj[:1] * q_scale)

    for l in range(depth):
        final = l == depth - 1
        qkv, gates = _in_proj(hb, w_in, 0, rows(gate_b), l, bsz * BLOCK)
        y_attn = _attention(qkv, bias_tables, sinks, l, bsz, nb)
        scan_jobs = [_cast_job(w, l, scan_steps) for w in (w_mlp_up, ssm_w_glu, w_attn_up, w_ssm_up, w_out)]
        y_gelu, w_up, w_glu, w_au, w_su, w_o = _s5_scan(hb, w_in, 0, wb, wc, a_r, a_i, d_rows, l, bsz,
                                                        cast_jobs=scan_jobs)
        mix_jobs = [_cast_job(w_mlp_down, l, mix_steps)]
        if not final:
            mix_jobs.append(_cast_job(in_proj, l + 1, mix_steps, scale=q_scale))
        hf, hb, w_dn, *w_next = _mix(y_attn, y_gelu, gates, hf, w_glu, w_au, w_su, w_o, rows(ln_mix_g),
                                     rows(ln_mix_b), l, alpha, bsz, cast_jobs=mix_jobs)
        out = _mlp(hb, hf, w_up, w_dn, rows(ln_mlp_g), rows(ln_mlp_b), l, alpha, bsz, final)
        if not final:
            (hf, hb), (w_in,) = out, w_next

    return out[0]
```
